```python
import math
import jax, jax.numpy as jnp
from jax import lax
import numpy as np

D_MODEL = 1024
BATCH = 8
SEQ = 2048
DEPTH = 4

GRID_W = 64
CTX_LEN = 256

CHUNK = 128
GM_WIDTH = D_MODEL
GM_GROUPS = 8
GM_GROUP_DIM = GM_WIDTH // GM_GROUPS
DA_HEADS = 8
DA_HEAD_DIM = 64
DA_V_DIM = 2 * DA_HEAD_DIM
DA_QK_WIDTH = DA_HEADS * 2 * DA_HEAD_DIM
DA_V_WIDTH = DA_HEADS * DA_V_DIM
ATTN_BLOCK = 128
ROPE_BASE = 10000.0
RW_HEAD = 64
RW_WIDTH = D_MODEL
RW_HEADS = RW_WIDTH // RW_HEAD
DECAY_LORA = 64
AAA_LORA = 64
GATE_LORA = 160
N_BRANCH = 3
D_FF = ((8 * D_MODEL + 3 * 256 - 1) // (3 * 256)) * 256

RMS_EPS = 1e-6
GN_EPS = 64e-5

GM_COLS = 2 * GM_WIDTH
DA_COLS = 2 * DA_QK_WIDTH + DA_V_WIDTH
RW_COLS = 3 * RW_WIDTH + 2 * DECAY_LORA + 2 * AAA_LORA + GATE_LORA
GATE_COLS = N_BRANCH * D_MODEL
P_TOTAL = GM_COLS + DA_COLS + RW_COLS + GATE_COLS
P_SPLITS = [GM_COLS, GM_COLS + DA_COLS, GM_COLS + DA_COLS + RW_COLS]
RW_SPLITS = [RW_WIDTH, 2 * RW_WIDTH, 3 * RW_WIDTH, 3 * RW_WIDTH + 2 * DECAY_LORA,
             3 * RW_WIDTH + 2 * DECAY_LORA + 2 * AAA_LORA]

kernel_name = 'hybrid_gmlp_diffattn_rwkv7_dit_block'


def rms_norm(x, g):
    xf = x.astype(jnp.float32)
    y = xf * lax.rsqrt(jnp.mean(xf * xf, axis=-1, keepdims=True) + RMS_EPS)
    return (y * g.astype(jnp.float32)).astype(x.dtype)


def axial_rope_tables(n_rows):
    row = jnp.repeat(jnp.arange(n_rows), GRID_W).astype(jnp.float32)
    col = jnp.tile(jnp.arange(GRID_W), n_rows).astype(jnp.float32)
    axis_dim = DA_HEAD_DIM // 2
    inv_freq = ROPE_BASE ** (-jnp.arange(0, axis_dim, 2, dtype=jnp.float32) / axis_dim)
    ang = jnp.concatenate([row[:, None] * inv_freq, col[:, None] * inv_freq], axis=-1)
    ang = jnp.concatenate([ang, ang], axis=-1)
    return jnp.cos(ang), jnp.sin(ang)


def apply_rope(t, cos, sin):
    half = DA_HEAD_DIM // 2
    rot = jnp.concatenate([-t[..., half:], t[..., :half]], axis=-1)
    cs = cos[None, :, None, None, :]
    sn = sin[None, :, None, None, :]
    return (t * cs + rot * sn).astype(t.dtype)


def centred_shift(p):
    prev = jnp.pad(p[:, :-1], ((0, 0), (1, 0), (0, 0)))
    nxt = jnp.pad(p[:, 1:], ((0, 0), (0, 1), (0, 0)))
    return 0.5 * (prev + nxt)


def chunk_gmlp(p_gm, v_g, w_s, b_s):
    B, T, _ = p_gm.shape
    u = jax.nn.gelu(p_gm[..., :GM_WIDTH], approximate=False)
    v = rms_norm(jax.nn.gelu(p_gm[..., GM_WIDTH:], approximate=False), v_g)
    v = v.reshape(B, T // CHUNK, CHUNK, GM_GROUPS, GM_GROUP_DIM)
    f = jnp.einsum('gts,bnsgc->bntgc', w_s, v) + b_s.T[None, None, :, :, None]
    return u * f.reshape(B, T, GM_WIDTH)


def qk_heads(p, g):
    return rms_norm(p.reshape(p.shape[0], p.shape[1], DA_HEADS, 2, DA_HEAD_DIM), g)


def diff_attend(q, k, v, lam):
    s = jnp.einsum('bqhcd,bkhcd->bhcqk', q, k).astype(jnp.float32) * (DA_HEAD_DIM ** -0.5)
    p = jax.nn.softmax(s, axis=-1)
    pd = (p[:, :, 0] - lam * p[:, :, 1]).astype(v.dtype)
    return jnp.einsum('bhqk,bkhe->bqhe', pd, v)


def diff_out(o, subln_g, lam_init):
    o = rms_norm(o, subln_g) * (1.0 - lam_init)
    return o.reshape(o.shape[0], o.shape[1], DA_V_WIDTH)


def rwkv_prep(p_rw, lp):
    B, T, _ = p_rw.shape
    z = p_rw + lp['rw_mu'] * (centred_shift(p_rw) - p_rw)
    zr, zk, zv, zw, za, zg = jnp.split(z, RW_SPLITS, axis=-1)
    zw = zw.reshape(B, T, 2, DECAY_LORA)
    za = za.reshape(B, T, 2, AAA_LORA)
    w_log = -jax.nn.softplus(-(lp['rw_w0'] + jnp.einsum('btdr,drc->btdc', jnp.tanh(zw), lp['rw_w2']))) - 0.5
    decay = jnp.exp(-jnp.exp(w_log.astype(jnp.float32)))
    a = jax.nn.sigmoid((lp['rw_a0'] + jnp.einsum('btdr,drc->btdc', za, lp['rw_a2'])).astype(jnp.float32))
    kk = (zk * lp['rw_kk']).astype(jnp.float32).reshape(B, T, RW_HEADS, RW_HEAD)
    kk = kk / jnp.maximum(jnp.sqrt(jnp.sum(kk * kk, axis=-1, keepdims=True)), 1e-12)
    kd = zk.astype(jnp.float32)[:, :, None, :] * (1.0 + (a - 1.0) * lp['rw_ka'].astype(jnp.float32))
    heads = lambda t: t.reshape(t.shape[:-1] + (RW_HEADS, RW_HEAD))
    return dict(r=heads(zr.astype(jnp.float32)), v=heads(zv.astype(jnp.float32)), kk=kk,
                decay=heads(decay), kd=heads(kd), a=heads(a), zg=zg)


def rwkv_scan(w, k, v, a_vec, b_vec, r, state0, reverse, with_outputs):
    xs = (w, k, v, a_vec, b_vec) + ((r,) if with_outputs else ())
    xs = tuple(jnp.moveaxis(t, 1, 0) for t in xs)

    def step(S, inp):
        w_t, k_t, v_t, a_t, b_t = inp[:5]
        sa = jnp.einsum('bhij,bhj->bhi', S, a_t)
        S = S * w_t[:, :, None, :] + sa[..., None] * b_t[:, :, None, :] + v_t[..., None] * k_t[:, :, None, :]
        y = jnp.einsum('bhij,bhj->bhi', S, inp[5]) if with_outputs else None
        return S, y

    S, ys = lax.scan(step, state0, xs, reverse=reverse)
    return S, (jnp.moveaxis(ys, 0, 1) if with_outputs else None)


def rwkv_run(pr, states0, with_outputs):
    finals, ys = [], []
    for d in range(2):
        S, y = rwkv_scan(pr['decay'][:, :, d], pr['kd'][:, :, d], pr['v'], -pr['kk'],
                         pr['kk'] * pr['a'][:, :, d], pr['r'], states0[d], d == 1, with_outputs)
        finals.append(S)
        ys.append(y)
    return finals, (ys[0] + ys[1] if with_outputs else None)


def rwkv_out(pr, y, lp, dtype):
    B, T = y.shape[0], y.shape[1]
    mean = jnp.mean(y, axis=-1, keepdims=True)
    var = jnp.mean(jnp.square(y - mean), axis=-1, keepdims=True)
    yn = ((y - mean) * lax.rsqrt(var + GN_EPS)).reshape(B, T, RW_WIDTH) * lp['rw_ln_w'] + lp['rw_ln_b']
    kd_sum = pr['kd'][:, :, 0] + pr['kd'][:, :, 1]
    bonus = jnp.sum(pr['r'] * kd_sum * lp['rw_rk'].astype(jnp.float32), axis=-1, keepdims=True) * pr['v']
    g = jax.nn.sigmoid(pr['zg']) @ lp['rw_g2']
    return ((yn + bonus.reshape(B, T, RW_WIDTH)) * g).astype(dtype)


def merge_branches(a, b, c, gate_p, lp):
    gates = jax.nn.sigmoid(gate_p).reshape(gate_p.shape[0], gate_p.shape[1], N_BRANCH, D_MODEL)
    m = (gates[:, :, 0] * (a @ lp['w_br_a']) + gates[:, :, 1] * (b @ lp['w_br_b'])
         + gates[:, :, 2] * (c @ lp['w_br_c']))
    return m @ lp['w_o']


def token_mixers(h_lat, h_ctx, lp, cos, sin, lam_init, ctx_out):
    B, T, _ = h_lat.shape
    gm_l, da_l, rw_l, gt_l = jnp.split(h_lat @ lp['w_in'], P_SPLITS, axis=-1)
    gm_c, da_c, rw_c, gt_c = jnp.split(h_ctx @ lp['w_in'], P_SPLITS, axis=-1)

    a_l = chunk_gmlp(gm_l, lp['gm_v_g'], lp['gm_ws'], lp['gm_bs'])

    q_l = apply_rope(qk_heads(da_l[..., :DA_QK_WIDTH], lp['da_q_g']), cos, sin)
    k_l = apply_rope(qk_heads(da_l[..., DA_QK_WIDTH:2 * DA_QK_WIDTH], lp['da_k_g']), cos, sin)
    v_l = da_l[..., 2 * DA_QK_WIDTH:].reshape(B, T, DA_HEADS, DA_V_DIM)
    k_c = qk_heads(da_c[..., DA_QK_WIDTH:2 * DA_QK_WIDTH], lp['da_k_g'])
    v_c = da_c[..., 2 * DA_QK_WIDTH:].reshape(B, -1, DA_HEADS, DA_V_DIM)
    lq1, lk1, lq2, lk2 = lp['da_lambda'].astype(jnp.float32)
    lam = jnp.exp(jnp.sum(lq1 * lk1)) - jnp.exp(jnp.sum(lq2 * lk2)) + lam_init
    k_all = jnp.concatenate([k_l, k_c], axis=1)
    v_all = jnp.concatenate([v_l, v_c], axis=1)
    n_blk = T // ATTN_BLOCK
    qb = jnp.moveaxis(q_l.reshape(B, n_blk, ATTN_BLOCK, DA_HEADS, 2, DA_HEAD_DIM), 1, 0)
    ob = lax.map(lambda qi: diff_attend(qi, k_all, v_all, lam), qb)
    o_l = jnp.moveaxis(ob, 0, 1).reshape(B, T, DA_HEADS, DA_V_DIM)
    b_l = diff_out(o_l, lp['da_subln_g'], lam_init)

    pr_c = rwkv_prep(rw_c, lp)
    zero = jnp.zeros((B, RW_HEADS, RW_HEAD, RW_HEAD), jnp.float32)
    states_c, y_c = rwkv_run(pr_c, (zero, zero), ctx_out)
    pr_l = rwkv_prep(rw_l, lp)
    _, y_l = rwkv_run(pr_l, states_c, True)
    c_l = rwkv_out(pr_l, y_l, lp, h_lat.dtype)

    out_l = merge_branches(a_l, b_l, c_l, gt_l, lp)
    if not ctx_out:
        return out_l, None

    a_c = chunk_gmlp(gm_c, lp['gm_v_g'], lp['gm_ws'], lp['gm_bs'])
    q_c = qk_heads(da_c[..., :DA_QK_WIDTH], lp['da_q_g'])
    b_c = diff_out(diff_attend(q_c, k_c, v_c, lam), lp['da_subln_g'], lam_init)
    c_c = rwkv_out(pr_c, y_c, lp, h_ctx.dtype)
    out_c = merge_branches(a_c, b_c, c_c, gt_c, lp)
    return out_l, out_c


def swiglu(h, wi, wo):
    gt, up = jnp.split(h @ wi, 2, axis=-1)
    return (jax.nn.silu(gt) * up) @ wo


def setup_inputs(seed: int = 0) -> dict:
    key = jax.random.key(seed)
    ks = iter(jax.random.split(key, 40))
    nrm = lambda shape, scale: jax.random.normal(next(ks), shape, jnp.float32) * scale
    uni = lambda shape, lo, hi: jax.random.uniform(next(ks), shape, jnp.float32, lo, hi)
    L, D = DEPTH, D_MODEL
    return {
        'x': nrm((BATCH, SEQ, D), 1.0),
        'c': nrm((BATCH, D), 1.0),
        'ctx': nrm((BATCH, CTX_LEN, D), 1.0),
        'c_ctx': nrm((D,), 1.0),
        'ada_w': nrm((L, D, 6 * D), D ** -0.5),
        'ada_b': nrm((L, 6 * D), 0.02),
        'norm1_g': 1.0 + nrm((L, D), 0.02),
        'norm2_g': 1.0 + nrm((L, D), 0.02),
        'w_in': nrm((L, D, P_TOTAL), D ** -0.5),
        'gm_v_g': 1.0 + nrm((L, GM_WIDTH), 0.02),
        'gm_ws': nrm((L, GM_GROUPS, CHUNK, CHUNK), CHUNK ** -0.5),
        'gm_bs': 1.0 + nrm((L, GM_GROUPS, CHUNK), 0.02),
        'da_q_g': 1.0 + nrm((L, DA_HEAD_DIM), 0.02),
        'da_k_g': 1.0 + nrm((L, DA_HEAD_DIM), 0.02),
        'da_lambda': nrm((L, 4, DA_HEAD_DIM), 0.1),
        'da_subln_g': 1.0 + nrm((L, DA_V_DIM), 0.02),
        'rw_mu': uni((L, RW_COLS), 0.0, 1.0),
        'rw_w0': uni((L, 2, RW_WIDTH), -5.0, 1.0),
        'rw_w2': nrm((L, 2, DECAY_LORA, RW_WIDTH), 0.1),
        'rw_a0': nrm((L, 2, RW_WIDTH), 0.1),
        'rw_a2': nrm((L, 2, AAA_LORA, RW_WIDTH), 0.5 * AAA_LORA ** -0.5),
        'rw_g2': nrm((L, GATE_LORA, RW_WIDTH), GATE_LORA ** -0.5),
        'rw_kk': 0.85 + nrm((L, RW_WIDTH), 0.02),
        'rw_ka': 1.0 + nrm((L, RW_WIDTH), 0.02),
        'rw_rk': nrm((L, RW_HEADS, RW_HEAD), 0.1),
        'rw_ln_w': 1.0 + nrm((L, RW_WIDTH), 0.02),
        'rw_ln_b': nrm((L, RW_WIDTH), 0.02),
        'w_br_a': nrm((L, GM_WIDTH, D), GM_WIDTH ** -0.5),
        'w_br_b': nrm((L, DA_V_WIDTH, D), DA_V_WIDTH ** -0.5),
        'w_br_c': nrm((L, RW_WIDTH, D), RW_WIDTH ** -0.5),
        'w_o': nrm((L, D, D), D ** -0.5),
        'ffn_wi': nrm((L, D, 2 * D_FF), D ** -0.5),
        'ffn_wo': nrm((L, D_FF, D), D_FF ** -0.5),
    }


def reference(x, c, ctx, c_ctx, ada_w, ada_b, norm1_g, norm2_g, w_in, gm_v_g, gm_ws, gm_bs,
              da_q_g, da_k_g, da_lambda, da_subln_g, rw_mu, rw_w0, rw_w2, rw_a0, rw_a2, rw_g2,
              rw_kk, rw_ka, rw_rk, rw_ln_w, rw_ln_b, w_br_a, w_br_b, w_br_c, w_o, ffn_wi, ffn_wo):
    n_rows = x.shape[1] // GRID_W
    cos, sin = axial_rope_tables(n_rows)
    xc = ctx
    for l in range(DEPTH):
        ctx_out = l < DEPTH - 1
        lam_init = 0.8 - 0.6 * math.exp(-0.3 * l)
        mod_l = jax.nn.silu(c) @ ada_w[l] + ada_b[l]
        mod_c = jax.nn.silu(c_ctx) @ ada_w[l] + ada_b[l]
        sh1, sc1, g1, sh2, sc2, g2 = jnp.split(mod_l[:, None, :], 6, axis=-1)
        csh1, csc1, cg1, csh2, csc2, cg2 = jnp.split(mod_c, 6, axis=-1)
        lp = dict(w_in=w_in[l], gm_v_g=gm_v_g[l], gm_ws=gm_ws[l], gm_bs=gm_bs[l],
                  da_q_g=da_q_g[l], da_k_g=da_k_g[l], da_lambda=da_lambda[l], da_subln_g=da_subln_g[l],
                  rw_mu=rw_mu[l], rw_w0=rw_w0[l], rw_w2=rw_w2[l], rw_a0=rw_a0[l], rw_a2=rw_a2[l],
                  rw_g2=rw_g2[l], rw_kk=rw_kk[l], rw_ka=rw_ka[l], rw_rk=rw_rk[l],
                  rw_ln_w=rw_ln_w[l], rw_ln_b=rw_ln_b[l], w_br_a=w_br_a[l], w_br_b=w_br_b[l],
                  w_br_c=w_br_c[l], w_o=w_o[l])
        h_lat = rms_norm(x, norm1_g[l]) * (1.0 + sc1) + sh1
        h_ctx = rms_norm(xc, norm1_g[l]) * (1.0 + csc1) + csh1
        mix_l, mix_c = token_mixers(h_lat, h_ctx, lp, cos, sin, lam_init, ctx_out)
        x = x + g1 * mix_l
        x = x + g2 * swiglu(rms_norm(x, norm2_g[l]) * (1.0 + sc2) + sh2, ffn_wi[l], ffn_wo[l])
        if ctx_out:
            xc = xc + cg1 * mix_c
            xc = xc + cg2 * swiglu(rms_norm(xc, norm2_g[l]) * (1.0 + csc2) + csh2, ffn_wi[l], ffn_wo[l])
    return x
```

```python
import functools
import math

import jax
import jax.numpy as jnp
from jax import lax
from jax.experimental import pallas as pl
from jax.experimental.pallas import tpu as pltpu

D_MODEL = 1024
GRID_W = 64
CHUNK = 128
GM_GROUPS = 8
DA_HEADS = 8
DA_HEAD_DIM = 64
DA_V_DIM = 128
ROPE_BASE = 10000.0
RW_HEAD = 64
RW_HEADS = 16
DECAY_LORA = 64
AAA_LORA = 64
GATE_LORA = 160
D_FF = 2816
RMS_EPS = 1e-6
GN_EPS = 64e-5

COL_GM = 0
COL_DA = 2048
COL_GATE = 5120
COL_RW = 8192
COL_TAIL = 11264
NP_COLS = 11776
PROJ_TN = 512
RW_CHUNK = 64
LANES = 128
VMEM_LIMIT = 56 * 1024 * 1024

BF16 = jnp.bfloat16
F32 = jnp.float32


def _dot(a, b):
    return jnp.dot(a, b, preferred_element_type=F32)


def _dot_nt(a, b):
    return lax.dot_general(a, b, (((1,), (1,)), ((), ())), preferred_element_type=F32)


def _split_dot(x, m):
    hi = x.astype(BF16)
    lo = (x - hi.astype(F32)).astype(BF16)
    return _dot(hi, m) + _dot(lo, m)


def _split3_dot_left(m, x):
    hi = x.astype(BF16)
    r1 = x - hi.astype(F32)
    mid = r1.astype(BF16)
    lo = (r1 - mid.astype(F32)).astype(BF16)
    return _dot(m, hi) + _dot(m, mid) + _dot(m, lo)


def _group_sum(x, bd, width=LANES):
    parts = [_split_dot(x[:, i:i + width], bd) for i in range(0, x.shape[1], width)]
    return parts[0] if len(parts) == 1 else jnp.concatenate(parts, axis=1)


def _cparams(sem):
    return pltpu.CompilerParams(dimension_semantics=sem, vmem_limit_bytes=VMEM_LIMIT)


def _ada_kernel(c_ref, w_ref, b_ref, o_ref):
    c = c_ref[...]
    s = c * jax.nn.sigmoid(c)
    o_ref[...] = jnp.dot(s, w_ref[...], precision=lax.Precision.HIGHEST,
                         preferred_element_type=F32) + b_ref[...]


def ada_mod(cc, ada_w, ada_b):
    L, D, N = ada_w.shape
    tn = 1536
    return pl.pallas_call(
        _ada_kernel,
        grid=(L, N // tn),
        in_specs=[pl.BlockSpec((16, D), lambda l, j: (0, 0)),
                  pl.BlockSpec((None, D, tn), lambda l, j: (l, 0, j)),
                  pl.BlockSpec((None, 1, tn), lambda l, j: (l, 0, j))],
        out_specs=pl.BlockSpec((None, 16, tn), lambda l, j: (l, 0, j)),
        out_shape=jax.ShapeDtypeStruct((L, 16, N), F32),
        compiler_params=_cparams(("arbitrary", "arbitrary")),
        name="ada_mod",
    )(cc, ada_w, ada_b.reshape(L, 1, N))


def _modulated_norm(x, g, m_ref, piece, row0, n_ctx):
    rows = row0 + lax.broadcasted_iota(jnp.int32, (x.shape[0], 1), 0)
    is_lat = rows >= n_ctx
    sh = jnp.where(is_lat, m_ref[2 * piece + 1:2 * piece + 2, :], m_ref[2 * piece:2 * piece + 1, :])
    sc = jnp.where(is_lat, m_ref[2 * piece + 3:2 * piece + 4, :], m_ref[2 * piece + 2:2 * piece + 3, :])
    y = x * lax.rsqrt(jnp.mean(x * x, axis=-1, keepdims=True) + RMS_EPS) * g
    return y * (1.0 + sc) + sh


def _gate_rows(m_ref, piece, row0, nrows, n_ctx):
    rows = row0 + lax.broadcasted_iota(jnp.int32, (nrows, 1), 0)
    return jnp.where(rows >= n_ctx, m_ref[2 * piece + 1:2 * piece + 2, :], m_ref[2 * piece:2 * piece + 1, :])


def _proj_kernel(x_ref, m_ref, g_ref, w_ref, mu_ref, o_ref, h_ref, p_ref, *, n_ctx, rb, rw_j0):
    S = x_ref.shape[0]
    j = pl.program_id(1)
    nblk = S // rb

    @pl.when(j == 0)
    def _():
        def body(i, carry):
            r0 = pl.multiple_of(i * rb, rb)
            x = x_ref[pl.ds(r0, rb), :]
            h_ref[pl.ds(r0, rb), :] = _modulated_norm(x, g_ref[...], m_ref, 0, r0, n_ctx).astype(BF16)
            return carry
        lax.fori_loop(0, nblk, body, 0)

    @pl.when(j < rw_j0)
    def _():
        def body(i, carry):
            r0 = pl.multiple_of(i * rb, rb)
            o_ref[pl.ds(r0, rb), :] = _dot(h_ref[pl.ds(r0, rb), :], w_ref[...])
            return carry
        lax.fori_loop(0, nblk, body, 0)

    @pl.when(j >= rw_j0)
    def _():
        zero = jnp.zeros((8, p_ref.shape[1]), F32)
        p_ref[0:8, :] = zero
        p_ref[S + 8:S + 16, :] = zero

        def body(i, carry):
            r0 = pl.multiple_of(i * rb, rb)
            p_ref[pl.ds(r0 + 8, rb), :] = _dot(h_ref[pl.ds(r0, rb), :], w_ref[...])
            return carry
        lax.fori_loop(0, nblk, body, 0)
        mu = mu_ref[...]
        for i in range(nblk):
            r0 = i * rb
            rows = r0 + lax.broadcasted_iota(jnp.int32, (rb, 1), 0)
            cur = p_ref[r0 + 8:r0 + 8 + rb, :]
            prev = jnp.where((rows == 0) | (rows == n_ctx), 0.0, p_ref[r0 + 7:r0 + 7 + rb, :])
            nxt = jnp.where((rows == n_ctx - 1) | (rows == S - 1), 0.0, p_ref[r0 + 9:r0 + 9 + rb, :])
            o_ref[r0:r0 + rb, :] = cur + mu * (0.5 * (prev + nxt) - cur)


def proj_in(xs, mod, g, w, mu, n_ctx):
    B, S, D = xs.shape
    NP = w.shape[1]
    tn = PROJ_TN
    kern = functools.partial(_proj_kernel, n_ctx=n_ctx, rb=384, rw_j0=COL_RW // tn)
    return pl.pallas_call(
        kern,
        grid=(B, NP // tn),
        in_specs=[pl.BlockSpec((None, S, D), lambda b, j: (b, 0, 0)),
                  pl.BlockSpec((None, 12, D), lambda b, j: (b, 0, 0)),
                  pl.BlockSpec((1, D), lambda b, j: (0, 0)),
                  pl.BlockSpec((D, tn), lambda b, j: (0, j)),
                  pl.BlockSpec((1, tn), lambda b, j: (0, j))],
        out_specs=pl.BlockSpec((None, S, tn), lambda b, j: (b, 0, j)),
        out_shape=jax.ShapeDtypeStruct((B, S, NP), F32),
        scratch_shapes=[pltpu.VMEM((S, D), BF16), pltpu.VMEM((S + 16, tn), F32)],
        compiler_params=_cparams(("arbitrary", "arbitrary")),
        name="proj_in",
    )(xs, mod, g, w, mu)


def _gelu(x):
    return 0.5 * x * (1.0 + lax.erf(x * (1.0 / math.sqrt(2.0))))


def _gmlp_kernel(p_ref, vg_ref, ws_ref, bs_ref, o_ref):
    W = vg_ref.shape[1]
    u = _gelu(p_ref[:, :W])
    v = _gelu(p_ref[:, W:])
    v = v * lax.rsqrt(jnp.mean(v * v, axis=-1, keepdims=True) + RMS_EPS) * vg_ref[...]
    vb = v.astype(BF16)
    gd = W // GM_GROUPS
    for g in range(GM_GROUPS):
        sl = slice(g * gd, (g + 1) * gd)
        f = _dot(ws_ref[g], vb[:, sl]) + bs_ref[:, sl]
        o_ref[:, sl] = (u[:, sl] * f).astype(o_ref.dtype)


def gmlp(P, vg, ws, bs_full):
    B, S, _ = P.shape
    W = vg.shape[1]
    return pl.pallas_call(
        _gmlp_kernel,
        grid=(B, S // CHUNK),
        in_specs=[pl.BlockSpec((None, CHUNK, 2 * W), lambda b, i: (b, i, 0)),
                  pl.BlockSpec((1, W), lambda b, i: (0, 0)),
                  pl.BlockSpec((GM_GROUPS, CHUNK, CHUNK), lambda b, i: (0, 0, 0)),
                  pl.BlockSpec((CHUNK, W), lambda b, i: (0, 0))],
        out_specs=pl.BlockSpec((None, CHUNK, W), lambda b, i: (b, i, 0)),
        out_shape=jax.ShapeDtypeStruct((B, S, W), BF16),
        compiler_params=_cparams(("arbitrary", "arbitrary")),
        name="gmlp",
    )(P, vg, ws, bs_full)


def _qk_kernel(q_ref, k_ref, v_ref, cos_ref, sa_ref, sb_ref, qg_ref, kg_ref, bd_ref,
               qo_ref, ko_ref, vo_ref):
    W = q_ref.shape[1]
    reps = W // LANES
    cos = jnp.tile(cos_ref[...], (1, reps))
    sin_a = jnp.tile(sa_ref[...], (1, reps))
    sin_b = jnp.tile(sb_ref[...], (1, reps))
    bd = bd_ref[...]
    half = DA_HEAD_DIM // 2

    def prep(t, g, scale):
        ms = _group_sum(t * t, bd) * (1.0 / DA_HEAD_DIM)
        t = t * lax.rsqrt(ms + RMS_EPS) * g
        t = t * cos + pltpu.roll(t, W - half, 1) * sin_a + pltpu.roll(t, half, 1) * sin_b
        return t * scale if scale != 1.0 else t

    qo_ref[...] = prep(q_ref[...], qg_ref[...], DA_HEAD_DIM ** -0.5).astype(BF16)
    ko_ref[...] = prep(k_ref[...], kg_ref[...], 1.0).astype(BF16)
    vo_ref[...] = v_ref[...].astype(BF16)


def qk_prep(P, cos, sin_a, sin_b, qg, kg, bd, tt=256):
    B, S, _ = P.shape
    W = D_MODEL
    c0 = COL_DA // W
    tok = lambda c: pl.BlockSpec((None, tt, W), lambda b, i, c=c: (b, i, c))
    tab = pl.BlockSpec((tt, LANES), lambda b, i: (i, 0))
    row = pl.BlockSpec((1, W), lambda b, i: (0, 0))
    out = pl.BlockSpec((None, tt, W), lambda b, i: (b, i, 0))
    shp = jax.ShapeDtypeStruct((B, S, W), BF16)
    return pl.pallas_call(
        _qk_kernel,
        grid=(B, S // tt),
        in_specs=[tok(c0), tok(c0 + 1), tok(c0 + 2), tab, tab, tab, row, row,
                  pl.BlockSpec((LANES, LANES), lambda b, i: (0, 0))],
        out_specs=[out, out, out],
        out_shape=[shp, shp, shp],
        compiler_params=_cparams(("arbitrary", "arbitrary")),
        name="qk_prep",
    )(P, P, P, cos, sin_a, sin_b, qg, kg, bd)


def _attn_kernel(q_ref, k_ref, v_ref, lam_ref, sg_ref, o_ref, *, n_ctx, lam_init):
    i = pl.program_id(2)
    tq = q_ref.shape[0]
    lp = lam_ref[...]
    lam = (jnp.exp(jnp.sum(lp[0:1] * lp[1:2], axis=-1, keepdims=True))
           - jnp.exp(jnp.sum(lp[2:3] * lp[3:4], axis=-1, keepdims=True)) + lam_init)

    def attend(nk):
        q = q_ref[...]
        lane = lax.broadcasted_iota(jnp.int32, q.shape, 1)
        zero = jnp.zeros_like(q)
        qs = jnp.concatenate([jnp.where(lane < DA_HEAD_DIM, q, zero),
                              jnp.where(lane >= DA_HEAD_DIM, q, zero)], axis=0)
        s = _dot_nt(qs, k_ref[0:nk, :])
        e = jnp.exp(s - jnp.max(s, axis=-1, keepdims=True))
        inv = 1.0 / jnp.sum(e, axis=-1, keepdims=True)
        pd = e[:tq] * inv[:tq] - e[tq:] * (lam * inv[tq:])
        o = _dot(pd.astype(BF16), v_ref[0:nk, :])
        o = o * lax.rsqrt(jnp.mean(o * o, axis=-1, keepdims=True) + RMS_EPS) * sg_ref[...]
        o_ref[...] = (o * (1.0 - lam_init)).astype(o_ref.dtype)

    @pl.when(i * tq < n_ctx)
    def _():
        attend(n_ctx)

    @pl.when(i * tq >= n_ctx)
    def _():
        attend(k_ref.shape[0])


def attention(qn, kn, vb, lam_p, sg, n_ctx, lam_init, tq=256):
    B, S, W = qn.shape
    H = W // LANES
    kern = functools.partial(_attn_kernel, n_ctx=n_ctx, lam_init=lam_init)
    kv = pl.BlockSpec((None, S, LANES), lambda b, h, i: (b, 0, h))
    return pl.pallas_call(
        kern,
        grid=(B, H, S // tq),
        in_specs=[pl.BlockSpec((None, tq, LANES), lambda b, h, i: (b, i, h)), kv, kv,
                  pl.BlockSpec((4, DA_HEAD_DIM), lambda b, h, i: (0, 0)),
                  pl.BlockSpec((1, LANES), lambda b, h, i: (0, 0))],
        out_specs=pl.BlockSpec((None, tq, LANES), lambda b, h, i: (b, i, h)),
        out_shape=jax.ShapeDtypeStruct((B, S, W), BF16),
        compiler_params=_cparams(("arbitrary", "arbitrary", "arbitrary")),
        name="attention",
    )(qn, kn, vb, lam_p, sg)


def _neg_softplus_neg(x):
    y = -x
    return -(jnp.maximum(y, 0.0) + jnp.log1p(jnp.exp(-jnp.abs(y))))


def _rw_prep_kernel(zk_ref, zt_ref, w0_ref, w2_ref, a0_ref, a2_ref, kk_ref, bd_ref,
                    kkn_ref, lw_ref, as_ref):
    zt = zt_ref[...]
    tw = jnp.tanh(zt[:, 0:LANES]).astype(BF16)
    za = zt[:, LANES:2 * LANES].astype(BF16)
    for d in range(2):
        w_log = _neg_softplus_neg(w0_ref[d:d + 1, :] + _dot(tw, w2_ref[d])) - 0.5
        lw_ref[d] = -jnp.exp(w_log)
        as_ref[d] = jax.nn.sigmoid(a0_ref[d:d + 1, :] + _dot(za, a2_ref[d]))
    kk = zk_ref[...] * kk_ref[...]
    nrm = jnp.sqrt(_group_sum(kk * kk, bd_ref[...]))
    kkn_ref[...] = kk / jnp.maximum(nrm, 1e-12)


def rw_prep(P, w0, w2p, a0, a2p, kkp, bd, tt=256):
    B, S, _ = P.shape
    W = D_MODEL
    tokw = pl.BlockSpec((None, tt, W), lambda b, i: (b, i, COL_RW // W + 1))
    tail = pl.BlockSpec((None, tt, 512), lambda b, i: (b, i, COL_TAIL // 512))
    full = lambda shape: pl.BlockSpec(shape, lambda b, i: (0,) * len(shape))
    out2 = pl.BlockSpec((2, None, tt, W), lambda b, i: (0, b, i, 0))
    return pl.pallas_call(
        _rw_prep_kernel,
        grid=(B, S // tt),
        in_specs=[tokw, tail, full((2, W)), full((2, LANES, W)), full((2, W)), full((2, LANES, W)),
                  full((1, W)), full((LANES, LANES))],
        out_specs=[pl.BlockSpec((None, tt, W), lambda b, i: (b, i, 0)), out2, out2],
        out_shape=[jax.ShapeDtypeStruct((B, S, W), F32),
                   jax.ShapeDtypeStruct((2, B, S, W), F32),
                   jax.ShapeDtypeStruct((2, B, S, W), F32)],
        compiler_params=_cparams(("arbitrary", "arbitrary")),
        name="rw_prep",
    )(P, P, w0, w2p, a0, a2p, kkp, bd)


def _rw_scan_kernel(r_ref, zk_ref, v_ref, kkn_ref, lw_ref, as_ref, ka_ref, y_ref, h_ref):
    d = pl.program_id(0)
    c = pl.program_id(2)
    C = RW_CHUNK

    @pl.when(c == 0)
    def _():
        h_ref[...] = jnp.zeros_like(h_ref)

    sgn = 1 - 2 * d
    ri = lax.broadcasted_iota(jnp.int32, (C, C), 0)
    ci = lax.broadcasted_iota(jnp.int32, (C, C), 1)
    incl_c = jnp.where(sgn * (ci - ri) <= 0, 1.0, 0.0).astype(BF16)
    lw = lw_ref[...]
    L = _split3_dot_left(incl_c, lw)
    ltot = jnp.sum(lw, axis=0, keepdims=True)
    lmid = 0.5 * ltot
    e_r = jnp.exp(L - lmid)
    e_a = jnp.exp(L - lw - lmid)
    e_b = jnp.exp(lmid - L)
    e_mid = jnp.exp(lmid)
    asig = as_ref[...]
    kkn = kkn_ref[...]
    zk = zk_ref[...]
    kd = zk * (1.0 + (asig - 1.0) * ka_ref[...])
    a_vec = -kkn
    b_vec = kkn * asig
    rr = r_ref[...]
    a_in = a_vec * e_a
    r_in = rr * e_r
    b_in = b_vec * e_b
    k_in = kd * e_b
    a_h = a_in * e_mid
    r_h = r_in * e_mid
    b_o = b_in * e_mid
    k_o = k_in * e_mid
    wtot = jnp.exp(ltot)
    vv = v_ref[...]

    R2 = 2 * C
    rb = lax.broadcasted_iota(jnp.int32, (R2, R2), 0)
    cb = lax.broadcasted_iota(jnp.int32, (R2, R2), 1)
    same = (rb // C) == (cb // C)
    order = sgn * (cb - rb)
    strict = same & (order < 0)
    incl = same & (order <= 0)
    eye = jnp.where(rb == cb, 1.0, 0.0)
    levels = int(math.log2(C))
    same_blk = [(rb >> k) == (cb >> k) for k in range(levels + 1)]
    joins = [same_blk[k + 1] & ~same_blk[k] for k in range(levels)]
    lane =lax.broadcasted_iota(jnp.int32, (C, LANES), 1)
    first = lane < RW_HEAD

    def stack(x):
        return jnp.concatenate([jnp.where(first, x, 0.0), jnp.where(first, 0.0, x)], axis=0)

    n_pairs = r_ref.shape[1] // LANES
    for p in range(n_pairs):
        sl = slice(p * LANES, (p + 1) * LANES)
        A_st, R_st, B_st, K_st = (stack(x[:, sl]).astype(BF16) for x in (a_in, r_in, b_in, k_in))
        Ah_st, Rh_st, Bo_st, Ko_st = (stack(x[:, sl]).astype(BF16) for x in (a_h, r_h, b_o, k_o))
        V_st = stack(vv[:, sl])
        V_bf = V_st.astype(BF16)
        G1 = _dot_nt(jnp.concatenate([A_st, R_st], axis=0), jnp.concatenate([B_st, K_st], axis=0))
        Aab = jnp.where(strict, G1[:R2, :R2], 0.0)
        Aak = jnp.where(strict, G1[:R2, R2:], 0.0)
        Arb = jnp.where(incl, G1[R2:, :R2], 0.0)
        Ark = jnp.where(incl, G1[R2:, R2:], 0.0)
        Tm = eye + jnp.where(joins[0], Aab, 0.0)
        for join in joins[1:]:
            TE = _dot(Tm.astype(BF16), jnp.where(join, Aab, 0.0).astype(BF16))
            Tm = Tm + _dot(TE.astype(BF16), Tm.astype(BF16))
        Hp = h_ref[p]
        G2 = _dot_nt(jnp.concatenate([Ah_st, Rh_st], axis=0), Hp.astype(BF16))
        rhs = G2[:R2] + _dot(Aak.astype(BF16), V_bf)
        U = _dot(Tm.astype(BF16), rhs.astype(BF16))
        UV = jnp.concatenate([U, V_st], axis=0)
        Y_st = G2[R2:] + _dot(jnp.concatenate([Arb, Ark], axis=1).astype(BF16), UV.astype(BF16))
        y_ref[:, sl] = Y_st[:C] + Y_st[C:]
        BK = jnp.concatenate([Bo_st, Ko_st], axis=0)
        h_ref[p] = Hp * wtot[:, sl] + _dot(UV.T.astype(BF16), BK)


def rw_scan(P, kkn, lw, asig, ka, n_ctx):
    B, S, _ = P.shape
    W = D_MODEL
    C = RW_CHUNK
    nc = S // C
    ncc = n_ctx // C

    def cidx(d, c):
        rev = jnp.where(c < ncc, ncc - 1 - c, nc - 1 + ncc - c)
        return jnp.where(d == 0, c, rev)

    pcol = lambda col: pl.BlockSpec((None, C, W), lambda d, b, c, col=col: (b, cidx(d, c), col))
    tok = pl.BlockSpec((None, C, W), lambda d, b, c: (b, cidx(d, c), 0))
    dtok = pl.BlockSpec((None, None, C, W), lambda d, b, c: (d, b, cidx(d, c), 0))
    c0 = COL_RW // W
    return pl.pallas_call(
        _rw_scan_kernel,
        grid=(2, B, nc),
        in_specs=[pcol(c0), pcol(c0 + 1), pcol(c0 + 2), tok, dtok, dtok,
                  pl.BlockSpec((1, W), lambda d, b, c: (0, 0))],
        out_specs=dtok,
        out_shape=jax.ShapeDtypeStruct((2, B, S, W), F32),
        scratch_shapes=[pltpu.VMEM((W // LANES, LANES, LANES), F32)],
        compiler_params=_cparams(("arbitrary", "arbitrary", "arbitrary")),
        name="rw_scan",
    )(P, P, P, kkn, lw, asig, ka)


def _merge_kernel(x_ref, m_ref, y_ref, as_ref, r_ref, zk_ref, v_ref, zt_ref, ga_ref, gb_ref, gc_ref,
                  a_ref, b_ref, lnw_ref, lnb_ref, rk_ref, ka_ref, g2_ref, bd_ref,
                  wa_ref, wb_ref, wc_ref, wo_ref, o_ref, *, n_ctx):
    tt = x_ref.shape[0]
    row0 = pl.program_id(1) * tt
    bd = bd_ref[...]
    inv_n = 1.0 / RW_HEAD
    y = y_ref[0] + y_ref[1]
    mean = _group_sum(y, bd) * inv_n
    yc = y - mean
    var = _group_sum(yc * yc, bd) * inv_n
    yn = yc * lax.rsqrt(var + GN_EPS) * lnw_ref[...] + lnb_ref[...]
    zk = zk_ref[...]
    ka = ka_ref[...]
    kd_sum = zk * (1.0 + (as_ref[0] - 1.0) * ka) + zk * (1.0 + (as_ref[1] - 1.0) * ka)
    vv = v_ref[...]
    bonus = _group_sum(r_ref[...] * kd_sum * rk_ref[...], bd) * vv
    g = _dot(jax.nn.sigmoid(zt_ref[:, 2 * LANES:]).astype(BF16), g2_ref[...])
    c_br = ((yn + bonus) * g).astype(BF16)
    m = (jax.nn.sigmoid(ga_ref[...]) * _dot(a_ref[...], wa_ref[...])
         + jax.nn.sigmoid(gb_ref[...]) * _dot(b_ref[...], wb_ref[...])
         + jax.nn.sigmoid(gc_ref[...]) * _dot(c_br, wc_ref[...]))
    out = _dot(m.astype(BF16), wo_ref[...])
    o_ref[...] = x_ref[...] + _gate_rows(m_ref, 2, row0, tt, n_ctx) * out


def merge(xs, mod, y, asig, P, a_br, b_br, lnw, lnb, rk, ka, g2p, bd, wa, wb, wc, wo, n_ctx, tt=256):
    B, S, W = xs.shape
    tok = pl.BlockSpec((None, tt, W), lambda b, i: (b, i, 0))
    pcol = lambda col: pl.BlockSpec((None, tt, W), lambda b, i, col=col: (b, i, col))
    dtok = pl.BlockSpec((2, None, tt, W), lambda b, i: (0, b, i, 0))
    row = pl.BlockSpec((1, W), lambda b, i: (0, 0))
    wsq = pl.BlockSpec((W, W), lambda b, i: (0, 0))
    cr, cg = COL_RW // W, COL_GATE // W
    kern = functools.partial(_merge_kernel, n_ctx=n_ctx)
    return pl.pallas_call(
        kern,
        grid=(B, S // tt),
        in_specs=[tok, pl.BlockSpec((None, 12, W), lambda b, i: (b, 0, 0)), dtok, dtok,
                  pcol(cr), pcol(cr + 1), pcol(cr + 2),
                  pl.BlockSpec((None, tt, 512), lambda b, i: (b, i, COL_TAIL // 512)),
                  pcol(cg), pcol(cg + 1), pcol(cg + 2), tok, tok,
                  row, row, row, row,
                  pl.BlockSpec((2 * LANES, W), lambda b, i: (0, 0)),
                  pl.BlockSpec((LANES, LANES), lambda b, i: (0, 0)),
                  wsq, wsq, wsq, wsq],
        out_specs=tok,
        out_shape=jax.ShapeDtypeStruct((B, S, W), F32),
        compiler_params=_cparams(("arbitrary", "arbitrary")),
        name="merge",
    )(xs, mod, y, asig, P, P, P, P, P, P, P, a_br, b_br, lnw, lnb, rk, ka, g2p, bd, wa, wb, wc, wo)


def _ffn_kernel(x_ref, m_ref, g_ref, wg_ref, wu_ref, wo_ref, o_ref, h_ref, acc_ref, *, n_ctx, tiles_per_seq):
    j = pl.program_id(1)
    tm = x_ref.shape[0]
    row0 = (pl.program_id(0) % tiles_per_seq) * tm

    @pl.when(j == 0)
    def _():
        h_ref[...] = _modulated_norm(x_ref[...], g_ref[...], m_ref, 3, row0, n_ctx).astype(BF16)
        acc_ref[...] = jnp.zeros_like(acc_ref)

    h = h_ref[...]
    gt = _dot(h, wg_ref[...])
    up = _dot(h, wu_ref[...])
    act = (gt * jax.nn.sigmoid(gt) * up).astype(BF16)
    acc_ref[...] += _dot(act, wo_ref[...])

    @pl.when(j == pl.num_programs(1) - 1)
    def _():
        o_ref[...] = x_ref[...] + _gate_rows(m_ref, 5, row0, tm, n_ctx) * acc_ref[...]


def ffn(xs, mod, g, wi, wo, n_ctx, tm=768, tf=256):
    B, S, W = xs.shape
    FF = wo.shape[0]
    tps = S // tm
    nf = FF // tf
    kern = functools.partial(_ffn_kernel, n_ctx=n_ctx, tiles_per_seq=tps)
    tok = pl.BlockSpec((None, tm, W), lambda t, j: (t // tps, t % tps, 0))
    return pl.pallas_call(
        kern,
        grid=(B * tps, nf),
        in_specs=[tok, pl.BlockSpec((None, 12, W), lambda t, j: (t // tps, 0, 0)),
                  pl.BlockSpec((1, W), lambda t, j: (0, 0)),
                  pl.BlockSpec((W, tf), lambda t, j: (0, j)),
                  pl.BlockSpec((W, tf), lambda t, j: (0, nf + j)),
                  pl.BlockSpec((tf, W), lambda t, j: (j, 0))],
        out_specs=tok,
        out_shape=jax.ShapeDtypeStruct((B, S, W), F32),
        scratch_shapes=[pltpu.VMEM((tm, W), BF16), pltpu.VMEM((tm, W), F32)],
        compiler_params=_cparams(("arbitrary", "arbitrary")),
        name="ffn",
    )(xs, mod, g, wi, wi, wo)


def _rope_tables(n_rows, n_ctx):
    row = jnp.repeat(jnp.arange(n_rows), GRID_W).astype(F32)
    col = jnp.tile(jnp.arange(GRID_W), n_rows).astype(F32)
    axis_dim = DA_HEAD_DIM // 2
    inv_freq = ROPE_BASE ** (-jnp.arange(0, axis_dim, 2, dtype=F32) / axis_dim)
    ang = jnp.concatenate([row[:, None] * inv_freq, col[:, None] * inv_freq], axis=-1)
    ang = jnp.concatenate([ang, ang], axis=-1)
    cos, sin = jnp.cos(ang), jnp.sin(ang)
    cos = jnp.concatenate([jnp.ones((n_ctx, DA_HEAD_DIM), F32), cos], axis=0)
    sin = jnp.concatenate([jnp.zeros((n_ctx, DA_HEAD_DIM), F32), sin], axis=0)
    first = jnp.arange(DA_HEAD_DIM) < DA_HEAD_DIM // 2
    sin_a = jnp.where(first, -sin, 0.0)
    sin_b = jnp.where(first, 0.0, sin)
    dup = lambda t: jnp.concatenate([t, t], axis=-1)
    return dup(cos), dup(sin_a), dup(sin_b)


def kernel(x, c, ctx, c_ctx, ada_w, ada_b, norm1_g, norm2_g, w_in, gm_v_g, gm_ws, gm_bs,
           da_q_g, da_k_g, da_lambda, da_subln_g, rw_mu, rw_w0, rw_w2, rw_a0, rw_a2, rw_g2,
           rw_kk, rw_ka, rw_rk, rw_ln_w, rw_ln_b, w_br_a, w_br_b, w_br_c, w_o, ffn_wi, ffn_wo):
    B, T, D = x.shape
    n_ctx = ctx.shape[1]
    L = ada_w.shape[0]
    cos, sin_a, sin_b = _rope_tables(T // GRID_W, n_ctx)
    xs = jnp.concatenate([ctx, x], axis=1)

    cc = jnp.zeros((16, D), F32).at[:B].set(c).at[B].set(c_ctx)
    mods = ada_mod(cc, ada_w, ada_b)
    mod_l = mods[:, :B].reshape(L, B, 6, 1, D)
    mod_c = jnp.broadcast_to(mods[:, B].reshape(L, 1, 6, 1, D), (L, B, 6, 1, D))
    mod12 = jnp.concatenate([mod_c, mod_l], axis=3).reshape(L, B, 12, D)

    gi = jnp.arange(LANES) // RW_HEAD
    bd = (gi[:, None] == gi[None, :]).astype(BF16)
    row = lambda t: t.reshape(1, -1)
    g0, d0, r0, t0 = 0, 2048, 5120, 5120 + 3072
    k0 = r0 + 3488

    for l in range(L):
        lam_init = 0.8 - 0.6 * math.exp(-0.3 * l)
        wl = w_in[l]
        w_cat = jnp.concatenate([wl[:, g0:d0], wl[:, d0:r0], wl[:, k0:], wl[:, r0:t0],
                                 wl[:, t0:k0], jnp.zeros((D, 96), F32)], axis=1).astype(BF16)
        mu = jnp.concatenate([jnp.zeros((COL_RW,), F32), rw_mu[l], jnp.zeros((96,), F32)]).reshape(1, -1)
        P = proj_in(xs, mod12[l], row(norm1_g[l]), w_cat, mu, n_ctx)

        bs_full = jnp.repeat(gm_bs[l].T, CHUNK, axis=1)
        a_br = gmlp(P, row(gm_v_g[l]), gm_ws[l].astype(BF16), bs_full)

        tile = lambda t, n: jnp.tile(t, n).reshape(1, -1)
        qn, kn, vb = qk_prep(P, cos, sin_a, sin_b, tile(da_q_g[l], 16), tile(da_k_g[l], 16), bd)
        b_br = attention(qn, kn, vb, da_lambda[l], row(da_subln_g[l]), n_ctx, lam_init)

        zpad = jnp.zeros((DECAY_LORA, D), F32)
        w2p = jnp.stack([jnp.concatenate([rw_w2[l, 0], zpad]), jnp.concatenate([zpad, rw_w2[l, 1]])]).astype(BF16)
        a2p = jnp.stack([jnp.concatenate([rw_a2[l, 0], zpad]), jnp.concatenate([zpad, rw_a2[l, 1]])]).astype(BF16)
        kkn, lw, asig = rw_prep(P, rw_w0[l], w2p, rw_a0[l], a2p, row(rw_kk[l]), bd)
        y = rw_scan(P, kkn, lw, asig, row(rw_ka[l]), n_ctx)

        g2p = jnp.concatenate([rw_g2[l], jnp.zeros((2 * LANES - GATE_LORA, D), F32)]).astype(BF16)
        xs = merge(xs, mod12[l], y, asig, P, a_br, b_br, row(rw_ln_w[l]), row(rw_ln_b[l]),
                   row(rw_rk[l]), row(rw_ka[l]), g2p, bd,
                   w_br_a[l].astype(BF16), w_br_b[l].astype(BF16), w_br_c[l].astype(BF16),
                   w_o[l].astype(BF16), n_ctx)
        xs = ffn(xs, mod12[l], row(norm2_g[l]), ffn_wi[l].astype(BF16), ffn_wo[l].astype(BF16), n_ctx)
    return xs[:, n_ctx:]
```

```python
import functools
import math

import jax
import jax.numpy as jnp
from jax import lax
from jax.experimental import pallas as pl
from jax.experimental.pallas import tpu as pltpu

D_MODEL = 1024
GRID_W = 64
CHUNK = 128
GM_GROUPS = 8
DA_HEADS = 8
DA_HEAD_DIM = 64
DA_V_DIM = 128
ROPE_BASE = 10000.0
RW_HEAD = 64
RW_HEADS = 16
DECAY_LORA = 64
AAA_LORA = 64
GATE_LORA = 160
D_FF = 2816
RMS_EPS = 1e-6
GN_EPS = 64e-5

COL_GM = 0
COL_DA = 2048
COL_GATE = 5120
COL_RW = 8192
COL_TAIL = 11264
NP_COLS = 11776
PROJ_TN = 512
RW_CHUNK = 64
LANES = 128
VMEM_LIMIT = 56 * 1024 * 1024

BF16 = jnp.bfloat16
F32 = jnp.float32


def _dot(a, b):
    return jnp.dot(a, b, preferred_element_type=F32)


def _dot_nt(a, b):
    return lax.dot_general(a, b, (((1,), (1,)), ((), ())), preferred_element_type=F32)


def _split_dot(x, m):
    hi = x.astype(BF16)
    lo = (x - hi.astype(F32)).astype(BF16)
    return _dot(hi, m) + _dot(lo, m)


def _split3_dot_left(m, x):
    hi = x.astype(BF16)
    r1 = x - hi.astype(F32)
    mid = r1.astype(BF16)
    lo = (r1 - mid.astype(F32)).astype(BF16)
    return _dot(m, hi) + _dot(m, mid) + _dot(m, lo)


def _group_sum(x, bd, width=LANES):
    parts = [_split_dot(x[:, i:i + width], bd) for i in range(0, x.shape[1], width)]
    return parts[0] if len(parts) == 1 else jnp.concatenate(parts, axis=1)


def _cparams(sem):
    return pltpu.CompilerParams(dimension_semantics=sem, vmem_limit_bytes=VMEM_LIMIT)


def _ada_kernel(c_ref, w_ref, b_ref, o_ref):
    c = c_ref[...]
    s = c * jax.nn.sigmoid(c)
    o_ref[...] = jnp.dot(s, w_ref[...], precision=lax.Precision.HIGHEST,
                         preferred_element_type=F32) + b_ref[...]


def ada_mod(cc, ada_w, ada_b):
    L, D, N = ada_w.shape
    tn = 1536
    return pl.pallas_call(
        _ada_kernel,
        grid=(L, N // tn),
        in_specs=[pl.BlockSpec((16, D), lambda l, j: (0, 0)),
                  pl.BlockSpec((None, D, tn), lambda l, j: (l, 0, j)),
                  pl.BlockSpec((None, 1, tn), lambda l, j: (l, 0, j))],
        out_specs=pl.BlockSpec((None, 16, tn), lambda l, j: (l, 0, j)),
        out_shape=jax.ShapeDtypeStruct((L, 16, N), F32),
        compiler_params=_cparams(("arbitrary", "arbitrary")),
        name="ada_mod",
    )(cc, ada_w, ada_b.reshape(L, 1, N))


def _modulated_norm(x, g, m_ref, piece, row0, n_ctx):
    rows = row0 + lax.broadcasted_iota(jnp.int32, (x.shape[0], 1), 0)
    is_lat = rows >= n_ctx
    sh = jnp.where(is_lat, m_ref[2 * piece + 1:2 * piece + 2, :], m_ref[2 * piece:2 * piece + 1, :])
    sc = jnp.where(is_lat, m_ref[2 * piece + 3:2 * piece + 4, :], m_ref[2 * piece + 2:2 * piece + 3, :])
    y = x * lax.rsqrt(jnp.mean(x * x, axis=-1, keepdims=True) + RMS_EPS) * g
    return y * (1.0 + sc) + sh


def _gate_rows(m_ref, piece, row0, nrows, n_ctx):
    rows = row0 + lax.broadcasted_iota(jnp.int32, (nrows, 1), 0)
    return jnp.where(rows >= n_ctx, m_ref[2 * piece + 1:2 * piece + 2, :], m_ref[2 * piece:2 * piece + 1, :])


def _proj_kernel(x_ref, m_ref, g_ref, w_ref, mu_ref, o_ref, h_ref, p_ref, *, n_ctx, rb, rw_j0):
    S = x_ref.shape[0]
    j = pl.program_id(1)
    nblk = S // rb

    @pl.when(j == 0)
    def _():
        def body(i, carry):
            r0 = pl.multiple_of(i * rb, rb)
            x = x_ref[pl.ds(r0, rb), :]
            h_ref[pl.ds(r0, rb), :] = _modulated_norm(x, g_ref[...], m_ref, 0, r0, n_ctx).astype(BF16)
            return carry
        lax.fori_loop(0, nblk, body, 0)

    @pl.when(j < rw_j0)
    def _():
        def body(i, carry):
            r0 = pl.multiple_of(i * rb, rb)
            o_ref[pl.ds(r0, rb), :] = _dot(h_ref[pl.ds(r0, rb), :], w_ref[...])
            return carry
        lax.fori_loop(0, nblk, body, 0)

    @pl.when(j >= rw_j0)
    def _():
        zero = jnp.zeros((8, p_ref.shape[1]), F32)
        p_ref[0:8, :] = zero
        p_ref[S + 8:S + 16, :] = zero

        def body(i, carry):
            r0 = pl.multiple_of(i * rb, rb)
            p_ref[pl.ds(r0 + 8, rb), :] = _dot(h_ref[pl.ds(r0, rb), :], w_ref[...])
            return carry
        lax.fori_loop(0, nblk, body, 0)
        mu = mu_ref[...]
        for i in range(nblk):
            r0 = i * rb
            rows = r0 + lax.broadcasted_iota(jnp.int32, (rb, 1), 0)
            cur = p_ref[r0 + 8:r0 + 8 + rb, :]
            prev = jnp.where((rows == 0) | (rows == n_ctx), 0.0, p_ref[r0 + 7:r0 + 7 + rb, :])
            nxt = jnp.where((rows == n_ctx - 1) | (rows == S - 1), 0.0, p_ref[r0 + 9:r0 + 9 + rb, :])
            o_ref[r0:r0 + rb, :] = cur + mu * (0.5 * (prev + nxt) - cur)


def proj_in(xs, mod, g, w, mu, n_ctx):
    B, S, D = xs.shape
    NP = w.shape[1]
    tn = PROJ_TN
    kern = functools.partial(_proj_kernel, n_ctx=n_ctx, rb=384, rw_j0=COL_RW // tn)
    return pl.pallas_call(
        kern,
        grid=(B, NP // tn),
        in_specs=[pl.BlockSpec((None, S, D), lambda b, j: (b, 0, 0)),
                  pl.BlockSpec((None, 12, D), lambda b, j: (b, 0, 0)),
                  pl.BlockSpec((1, D), lambda b, j: (0, 0)),
                  pl.BlockSpec((D, tn), lambda b, j: (0, j)),
                  pl.BlockSpec((1, tn), lambda b, j: (0, j))],
        out_specs=pl.BlockSpec((None, S, tn), lambda b, j: (b, 0, j)),
        out_shape=jax.ShapeDtypeStruct((B, S, NP), F32),
        scratch_shapes=[pltpu.VMEM((S, D), BF16), pltpu.VMEM((S + 16, tn), F32)],
        compiler_params=_cparams(("arbitrary", "arbitrary")),
        name="proj_in",
    )(xs, mod, g, w, mu)


def _gelu(x):
    return 0.5 * x * (1.0 + lax.erf(x * (1.0 / math.sqrt(2.0))))


def _gmlp_kernel(p_ref, vg_ref, ws_ref, bs_ref, o_ref):
    W = vg_ref.shape[1]
    u = _gelu(p_ref[:, :W])
    v = _gelu(p_ref[:, W:])
    v = v * lax.rsqrt(jnp.mean(v * v, axis=-1, keepdims=True) + RMS_EPS) * vg_ref[...]
    vb = v.astype(BF16)
    gd = W // GM_GROUPS
    for g in range(GM_GROUPS):
        sl = slice(g * gd, (g + 1) * gd)
        f = _dot(ws_ref[g], vb[:, sl]) + bs_ref[:, sl]
        o_ref[:, sl] = (u[:, sl] * f).astype(o_ref.dtype)


def gmlp(P, vg, ws, bs_full):
    B, S, _ = P.shape
    W = vg.shape[1]
    return pl.pallas_call(
        _gmlp_kernel,
        grid=(B, S // CHUNK),
        in_specs=[pl.BlockSpec((None, CHUNK, 2 * W), lambda b, i: (b, i, 0)),
                  pl.BlockSpec((1, W), lambda b, i: (0, 0)),
                  pl.BlockSpec((GM_GROUPS, CHUNK, CHUNK), lambda b, i: (0, 0, 0)),
                  pl.BlockSpec((CHUNK, W), lambda b, i: (0, 0))],
        out_specs=pl.BlockSpec((None, CHUNK, W), lambda b, i: (b, i, 0)),
        out_shape=jax.ShapeDtypeStruct((B, S, W), BF16),
        compiler_params=_cparams(("arbitrary", "arbitrary")),
        name="gmlp",
    )(P, vg, ws, bs_full)


def _qk_kernel(q_ref, k_ref, v_ref, cos_ref, sa_ref, sb_ref, qg_ref, kg_ref, bd_ref,
               qo_ref, ko_ref, vo_ref):
    W = q_ref.shape[1]
    reps = W // LANES
    cos = jnp.tile(cos_ref[...], (1, reps))
    sin_a = jnp.tile(sa_ref[...], (1, reps))
    sin_b = jnp.tile(sb_ref[...], (1, reps))
    bd = bd_ref[...]
    half = DA_HEAD_DIM // 2

    def prep(t, g, scale):
        ms = _group_sum(t * t, bd) * (1.0 / DA_HEAD_DIM)
        t = t * lax.rsqrt(ms + RMS_EPS) * g
        t = t * cos + pltpu.roll(t, W - half, 1) * sin_a + pltpu.roll(t, half, 1) * sin_b
        return t * scale if scale != 1.0 else t

    qo_ref[...] = prep(q_ref[...], qg_ref[...], DA_HEAD_DIM ** -0.5).astype(BF16)
    ko_ref[...] = prep(k_ref[...], kg_ref[...], 1.0).astype(BF16)
    vo_ref[...] = v_ref[...].astype(BF16)


def qk_prep(P, cos, sin_a, sin_b, qg, kg, bd, tt=256):
    B, S, _ = P.shape
    W = D_MODEL
    c0 = COL_DA // W
    tok = lambda c: pl.BlockSpec((None, tt, W), lambda b, i, c=c: (b, i, c))
    tab = pl.BlockSpec((tt, LANES), lambda b, i: (i, 0))
    row = pl.BlockSpec((1, W), lambda b, i: (0, 0))
    out = pl.BlockSpec((None, tt, W), lambda b, i: (b, i, 0))
    shp = jax.ShapeDtypeStruct((B, S, W), BF16)
    return pl.pallas_call(
        _qk_kernel,
        grid=(B, S // tt),
        in_specs=[tok(c0), tok(c0 + 1), tok(c0 + 2), tab, tab, tab, row, row,
                  pl.BlockSpec((LANES, LANES), lambda b, i: (0, 0))],
        out_specs=[out, out, out],
        out_shape=[shp, shp, shp],
        compiler_params=_cparams(("arbitrary", "arbitrary")),
        name="qk_prep",
    )(P, P, P, cos, sin_a, sin_b, qg, kg, bd)


def _attn_kernel(q_ref, k_ref, v_ref, lam_ref, sg_ref, o_ref, *, n_ctx, lam_init):
    i = pl.program_id(2)
    tq = q_ref.shape[0]
    lp = lam_ref[...]
    lam = (jnp.exp(jnp.sum(lp[0:1] * lp[1:2], axis=-1, keepdims=True))
           - jnp.exp(jnp.sum(lp[2:3] * lp[3:4], axis=-1, keepdims=True)) + lam_init)

    def attend(nk):
        q = q_ref[...]
        lane = lax.broadcasted_iota(jnp.int32, q.shape, 1)
        zero = jnp.zeros_like(q)
        qs = jnp.concatenate([jnp.where(lane < DA_HEAD_DIM, q, zero),
                              jnp.where(lane >= DA_HEAD_DIM, q, zero)], axis=0)
        s = _dot_nt(qs, k_ref[0:nk, :])
        e = jnp.exp(s - jnp.max(s, axis=-1, keepdims=True))
        inv = 1.0 / jnp.sum(e, axis=-1, keepdims=True)
        pd = e[:tq] * inv[:tq] - e[tq:] * (lam * inv[tq:])
        o = _dot(pd.astype(BF16), v_ref[0:nk, :])
        o = o * lax.rsqrt(jnp.mean(o * o, axis=-1, keepdims=True) + RMS_EPS) * sg_ref[...]
        o_ref[...] = (o * (1.0 - lam_init)).astype(o_ref.dtype)

    @pl.when(i * tq < n_ctx)
    def _():
        attend(n_ctx)

    @pl.when(i * tq >= n_ctx)
    def _():
        attend(k_ref.shape[0])


def attention(qn, kn, vb, lam_p, sg, n_ctx, lam_init, tq=256):
    B, S, W = qn.shape
    H = W // LANES
    kern = functools.partial(_attn_kernel, n_ctx=n_ctx, lam_init=lam_init)
    kv = pl.BlockSpec((None, S, LANES), lambda b, h, i: (b, 0, h))
    return pl.pallas_call(
        kern,
        grid=(B, H, S // tq),
        in_specs=[pl.BlockSpec((None, tq, LANES), lambda b, h, i: (b, i, h)), kv, kv,
                  pl.BlockSpec((4, DA_HEAD_DIM), lambda b, h, i: (0, 0)),
                  pl.BlockSpec((1, LANES), lambda b, h, i: (0, 0))],
        out_specs=pl.BlockSpec((None, tq, LANES), lambda b, h, i: (b, i, h)),
        out_shape=jax.ShapeDtypeStruct((B, S, W), BF16),
        compiler_params=_cparams(("arbitrary", "arbitrary", "arbitrary")),
        name="attention",
    )(qn, kn, vb, lam_p, sg)


def _neg_softplus_neg(x):
    y = -x
    return -(jnp.maximum(y, 0.0) + jnp.log1p(jnp.exp(-jnp.abs(y))))


def _rw_prep_kernel(zk_ref, zt_ref, w0_ref, w2_ref, a0_ref, a2_ref, kk_ref, bd_ref,
                    kkn_ref, lw_ref, as_ref):
    zt = zt_ref[...]
    tw = jnp.tanh(zt[:, 0:LANES]).astype(BF16)
    za = zt[:, LANES:2 * LANES].astype(BF16)
    for d in range(2):
        w_log = _neg_softplus_neg(w0_ref[d:d + 1, :] + _dot(tw, w2_ref[d])) - 0.5
        lw_ref[d] = -jnp.exp(w_log)
        as_ref[d] = jax.nn.sigmoid(a0_ref[d:d + 1, :] + _dot(za, a2_ref[d]))
    kk = zk_ref[...] * kk_ref[...]
    nrm = jnp.sqrt(_group_sum(kk * kk, bd_ref[...]))
    kkn_ref[...] = kk / jnp.maximum(nrm, 1e-12)


def rw_prep(P, w0, w2p, a0, a2p, kkp, bd, tt=256):
    B, S, _ = P.shape
    W = D_MODEL
    tokw = pl.BlockSpec((None, tt, W), lambda b, i: (b, i, COL_RW // W + 1))
    tail = pl.BlockSpec((None, tt, 512), lambda b, i: (b, i, COL_TAIL // 512))
    full = lambda shape: pl.BlockSpec(shape, lambda b, i: (0,) * len(shape))
    out2 = pl.BlockSpec((2, None, tt, W), lambda b, i: (0, b, i, 0))
    return pl.pallas_call(
        _rw_prep_kernel,
        grid=(B, S // tt),
        in_specs=[tokw, tail, full((2, W)), full((2, LANES, W)), full((2, W)), full((2, LANES, W)),
                  full((1, W)), full((LANES, LANES))],
        out_specs=[pl.BlockSpec((None, tt, W), lambda b, i: (b, i, 0)), out2, out2],
        out_shape=[jax.ShapeDtypeStruct((B, S, W), F32),
                   jax.ShapeDtypeStruct((2, B, S, W), F32),
                   jax.ShapeDtypeStruct((2, B, S, W), F32)],
        compiler_params=_cparams(("arbitrary", "arbitrary")),
        name="rw_prep",
    )(P, P, w0, w2p, a0, a2p, kkp, bd)


def _rw_scan_kernel(r_ref, zk_ref, v_ref, kkn_ref, lw_ref, as_ref, ka_ref, y_ref, h_ref):
    c = pl.program_id(1)
    C = RW_CHUNK
    R2 = 2 * C
    n_pairs = ka_ref.shape[1] // LANES

    @pl.when(c == 0)
    def _():
        h_ref[...] = jnp.zeros_like(h_ref)

    ri = lax.broadcasted_iota(jnp.int32, (C, C), 0)
    ci = lax.broadcasted_iota(jnp.int32, (C, C), 1)
    rb = lax.broadcasted_iota(jnp.int32, (R2, R2), 0)
    cb = lax.broadcasted_iota(jnp.int32, (R2, R2), 1)
    levels = int(math.log2(C))
    same_blk = [(rb >> k) == (cb >> k) for k in range(levels + 1)]
    joins = [same_blk[k + 1] & ~same_blk[k] for k in range(levels)]
    same = same_blk[levels]
    eye = jnp.where(rb == cb, 1.0, 0.0)
    first = lax.broadcasted_iota(jnp.int32, (C, LANES), 1) < RW_HEAD
    ka = ka_ref[...]

    def stack(x):
        return jnp.concatenate([jnp.where(first, x, 0.0), jnp.where(first, 0.0, x)], axis=0)

    chains = []
    for d in range(2):
        before_c = (ci <= ri) if d == 0 else (ci >= ri)
        strict = same & ((cb < rb) if d == 0 else (cb > rb))
        incl = same & ((cb <= rb) if d == 0 else (cb >= rb))
        lw = lw_ref[d][...]
        L = _split3_dot_left(jnp.where(before_c, 1.0, 0.0).astype(BF16), lw)
        ltot = jnp.sum(lw, axis=0, keepdims=True)
        lmid = 0.5 * ltot
        e_r = jnp.exp(L - lmid)
        e_a = jnp.exp(L - lw - lmid)
        e_b = jnp.exp(lmid - L)
        e_mid = jnp.exp(lmid)
        asig = as_ref[d][...]
        kkn = kkn_ref[d][...]
        kd = zk_ref[d][...] * (1.0 + (asig - 1.0) * ka)
        a_in = -kkn * e_a
        r_in = r_ref[d][...] * e_r
        b_in = kkn * asig * e_b
        k_in = kd * e_b
        a_h = a_in * e_mid
        r_h = r_in * e_mid
        b_o = b_in * e_mid
        k_o = k_in * e_mid
        wtot = jnp.exp(ltot)
        vv = v_ref[d][...]
        for p in range(n_pairs):
            sl = slice(p * LANES, (p + 1) * LANES)
            st = lambda x: stack(x[:, sl])
            chains.append(dict(
                d=d, p=p, sl=sl, strict=strict, incl=incl, wtot=wtot[:, sl],
                AR=jnp.concatenate([st(a_in), st(r_in)], axis=0).astype(BF16),
                BK=jnp.concatenate([st(b_in), st(k_in)], axis=0).astype(BF16),
                ARh=jnp.concatenate([st(a_h), st(r_h)], axis=0).astype(BF16),
                BKo=jnp.concatenate([st(b_o), st(k_o)], axis=0).astype(BF16),
                V=st(vv)))

    for ch in chains:
        G1 = _dot_nt(ch["AR"], ch["BK"])
        ch["Aab"] = jnp.where(ch["strict"], G1[:R2, :R2], 0.0)
        ch["Aak"] = jnp.where(ch["strict"], G1[:R2, R2:], 0.0).astype(BF16)
        ch["Ar"] = jnp.concatenate([jnp.where(ch["incl"], G1[R2:, :R2], 0.0),
                                    jnp.where(ch["incl"], G1[R2:, R2:], 0.0)], axis=1).astype(BF16)
        ch["H"] = h_ref[ch["d"], ch["p"]]
        ch["T"] = eye + jnp.where(joins[0], ch["Aab"], 0.0)
    for ch in chains:
        ch["G2"] = _dot_nt(ch["ARh"], ch["H"].astype(BF16))
        ch["Vb"] = ch["V"].astype(BF16)
        ch["rhs"] = ch["G2"][:R2] + _dot(ch["Aak"], ch["Vb"])
    for join in joins[1:]:
        for ch in chains:
            ch["Tb"] = ch["T"].astype(BF16)
            ch["TE"] = _dot(ch["Tb"], jnp.where(join, ch["Aab"], 0.0).astype(BF16)).astype(BF16)
        for ch in chains:
            ch["T"] = ch["T"] + _dot(ch["TE"], ch["Tb"])
    for ch in chains:
        U = _dot(ch["T"].astype(BF16), ch["rhs"].astype(BF16))
        ch["UV"] = jnp.concatenate([U, ch["V"]], axis=0)
    for ch in chains:
        Y_st = ch["G2"][R2:] + _dot(ch["Ar"], ch["UV"].astype(BF16))
        y_ref[ch["d"]][:, ch["sl"]] = Y_st[:C] + Y_st[C:]
    for ch in chains:
        h_ref[ch["d"], ch["p"]] = ch["H"] * ch["wtot"] + _dot(ch["UV"].T.astype(BF16), ch["BKo"])


def rw_scan(P, kkn, lw, asig, ka, n_ctx):
    B, S, _ = P.shape
    W = D_MODEL
    C = RW_CHUNK
    nc = S // C
    ncc = n_ctx // C

    def rev(c):
        return jnp.where(c < ncc, ncc - 1 - c, nc - 1 + ncc - c)

    def both(make):
        return [make(lambda c: c), make(rev)]

    pcol = lambda col: both(lambda f: pl.BlockSpec((None, C, W), lambda b, c, f=f: (b, f(c), col)))
    tok = both(lambda f: pl.BlockSpec((None, C, W), lambda b, c, f=f: (b, f(c), 0)))
    dtok = [pl.BlockSpec((None, None, C, W), lambda b, c: (0, b, c, 0)),
            pl.BlockSpec((None, None, C, W), lambda b, c: (1, b, rev(c), 0))]
    c0 = COL_RW // W

    def kern(rf, rb_, kf, kb_, vf, vb_, nf, nb_, lwf, lwb, asf, asb, ka_ref, yf, yb, h_ref):
        _rw_scan_kernel((rf, rb_), (kf, kb_), (vf, vb_), (nf, nb_), (lwf, lwb), (asf, asb), ka_ref,
                        (yf, yb), h_ref)

    return pl.pallas_call(
        kern,
        grid=(B, nc),
        in_specs=[*pcol(c0), *pcol(c0 + 1), *pcol(c0 + 2), *tok, *dtok, *dtok,
                  pl.BlockSpec((1, W), lambda b, c: (0, 0))],
        out_specs=[pl.BlockSpec((None, C, W), lambda b, c: (b, c, 0)),
                   pl.BlockSpec((None, C, W), lambda b, c: (b, rev(c), 0))],
        out_shape=[jax.ShapeDtypeStruct((B, S, W), F32)] * 2,
        scratch_shapes=[pltpu.VMEM((2, W // LANES, LANES, LANES), F32)],
        compiler_params=_cparams(("arbitrary", "arbitrary")),
        name="rw_scan",
    )(P, P, P, P, P, P, kkn, kkn, lw, lw, asig, asig, ka)


def _merge_kernel(x_ref, m_ref, yf_ref, yb_ref, as_ref, r_ref, zk_ref, v_ref, zt_ref, ga_ref, gb_ref, gc_ref,
                  a_ref, b_ref, lnw_ref, lnb_ref, rk_ref, ka_ref, g2_ref, bd_ref,
                  wa_ref, wb_ref, wc_ref, wo_ref, o_ref, *, n_ctx):
    tt = x_ref.shape[0]
    row0 = pl.program_id(1) * tt
    bd = bd_ref[...]
    inv_n = 1.0 / RW_HEAD
    y = yf_ref[...] + yb_ref[...]
    mean = _group_sum(y, bd) * inv_n
    yc = y - mean
    var = _group_sum(yc * yc, bd) * inv_n
    yn = yc * lax.rsqrt(var + GN_EPS) * lnw_ref[...] + lnb_ref[...]
    zk = zk_ref[...]
    ka = ka_ref[...]
    kd_sum = zk * (1.0 + (as_ref[0] - 1.0) * ka) + zk * (1.0 + (as_ref[1] - 1.0) * ka)
    vv = v_ref[...]
    bonus = _group_sum(r_ref[...] * kd_sum * rk_ref[...], bd) * vv
    g = _dot(jax.nn.sigmoid(zt_ref[:, 2 * LANES:]).astype(BF16), g2_ref[...])
    c_br = ((yn + bonus) * g).astype(BF16)
    m = (jax.nn.sigmoid(ga_ref[...]) * _dot(a_ref[...], wa_ref[...])
         + jax.nn.sigmoid(gb_ref[...]) * _dot(b_ref[...], wb_ref[...])
         + jax.nn.sigmoid(gc_ref[...]) * _dot(c_br, wc_ref[...]))
    out = _dot(m.astype(BF16), wo_ref[...])
    o_ref[...] = x_ref[...] + _gate_rows(m_ref, 2, row0, tt, n_ctx) * out


def merge(xs, mod, y, asig, P, a_br, b_br, lnw, lnb, rk, ka, g2p, bd, wa, wb, wc, wo, n_ctx, tt=256):
    B, S, W = xs.shape
    tok = pl.BlockSpec((None, tt, W), lambda b, i: (b, i, 0))
    pcol = lambda col: pl.BlockSpec((None, tt, W), lambda b, i, col=col: (b, i, col))
    dtok = pl.BlockSpec((2, None, tt, W), lambda b, i: (0, b, i, 0))
    row = pl.BlockSpec((1, W), lambda b, i: (0, 0))
    wsq = pl.BlockSpec((W, W), lambda b, i: (0, 0))
    cr, cg = COL_RW // W, COL_GATE // W
    kern = functools.partial(_merge_kernel, n_ctx=n_ctx)
    return pl.pallas_call(
        kern,
        grid=(B, S // tt),
        in_specs=[tok, pl.BlockSpec((None, 12, W), lambda b, i: (b, 0, 0)), tok, tok, dtok,
                  pcol(cr), pcol(cr + 1), pcol(cr + 2),
                  pl.BlockSpec((None, tt, 512), lambda b, i: (b, i, COL_TAIL // 512)),
                  pcol(cg), pcol(cg + 1), pcol(cg + 2), tok, tok,
                  row, row, row, row,
                  pl.BlockSpec((2 * LANES, W), lambda b, i: (0, 0)),
                  pl.BlockSpec((LANES, LANES), lambda b, i: (0, 0)),
                  wsq, wsq, wsq, wsq],
        out_specs=tok,
        out_shape=jax.ShapeDtypeStruct((B, S, W), F32),
        compiler_params=_cparams(("arbitrary", "arbitrary")),
        name="merge",
    )(xs, mod, y[0], y[1], asig, P, P, P, P, P, P, P, a_br, b_br, lnw, lnb, rk, ka, g2p, bd, wa, wb, wc, wo)


def _ffn_kernel(x_ref, m_ref, g_ref, wg_ref, wu_ref, wo_ref, o_ref, h_ref, acc_ref, *, n_ctx, tiles_per_seq):
    j = pl.program_id(1)
    tm = x_ref.shape[0]
    row0 = (pl.program_id(0) % tiles_per_seq) * tm

    @pl.when(j == 0)
    def _():
        h_ref[...] = _modulated_norm(x_ref[...], g_ref[...], m_ref, 3, row0, n_ctx).astype(BF16)
        acc_ref[...] = jnp.zeros_like(acc_ref)

    h = h_ref[...]
    gt = _dot(h, wg_ref[...])
    up = _dot(h, wu_ref[...])
    act = (gt * jax.nn.sigmoid(gt) * up).astype(BF16)
    acc_ref[...] += _dot(act, wo_ref[...])

    @pl.when(j == pl.num_programs(1) - 1)
    def _():
        o_ref[...] = x_ref[...] + _gate_rows(m_ref, 5, row0, tm, n_ctx) * acc_ref[...]


def ffn(xs, mod, g, wi, wo, n_ctx, tm=768, tf=256):
    B, S, W = xs.shape
    FF = wo.shape[0]
    tps = S // tm
    nf = FF // tf
    kern = functools.partial(_ffn_kernel, n_ctx=n_ctx, tiles_per_seq=tps)
    tok = pl.BlockSpec((None, tm, W), lambda t, j: (t // tps, t % tps, 0))
    return pl.pallas_call(
        kern,
        grid=(B * tps, nf),
        in_specs=[tok, pl.BlockSpec((None, 12, W), lambda t, j: (t // tps, 0, 0)),
                  pl.BlockSpec((1, W), lambda t, j: (0, 0)),
                  pl.BlockSpec((W, tf), lambda t, j: (0, j)),
                  pl.BlockSpec((W, tf), lambda t, j: (0, nf + j)),
                  pl.BlockSpec((tf, W), lambda t, j: (j, 0))],
        out_specs=tok,
        out_shape=jax.ShapeDtypeStruct((B, S, W), F32),
        scratch_shapes=[pltpu.VMEM((tm, W), BF16), pltpu.VMEM((tm, W), F32)],
        compiler_params=_cparams(("arbitrary", "arbitrary")),
        name="ffn",
    )(xs, mod, g, wi, wi, wo)


def _rope_tables(n_rows, n_ctx):
    row = jnp.repeat(jnp.arange(n_rows), GRID_W).astype(F32)
    col = jnp.tile(jnp.arange(GRID_W), n_rows).astype(F32)
    axis_dim = DA_HEAD_DIM // 2
    inv_freq = ROPE_BASE ** (-jnp.arange(0, axis_dim, 2, dtype=F32) / axis_dim)
    ang = jnp.concatenate([row[:, None] * inv_freq, col[:, None] * inv_freq], axis=-1)
    ang = jnp.concatenate([ang, ang], axis=-1)
    cos, sin = jnp.cos(ang), jnp.sin(ang)
    cos = jnp.concatenate([jnp.ones((n_ctx, DA_HEAD_DIM), F32), cos], axis=0)
    sin = jnp.concatenate([jnp.zeros((n_ctx, DA_HEAD_DIM), F32), sin], axis=0)
    first = jnp.arange(DA_HEAD_DIM) < DA_HEAD_DIM // 2
    sin_a = jnp.where(first, -sin, 0.0)
    sin_b = jnp.where(first, 0.0, sin)
    dup = lambda t: jnp.concatenate([t, t], axis=-1)
    return dup(cos), dup(sin_a), dup(sin_b)


def kernel(x, c, ctx, c_ctx, ada_w, ada_b, norm1_g, norm2_g, w_in, gm_v_g, gm_ws, gm_bs,
           da_q_g, da_k_g, da_lambda, da_subln_g, rw_mu, rw_w0, rw_w2, rw_a0, rw_a2, rw_g2,
           rw_kk, rw_ka, rw_rk, rw_ln_w, rw_ln_b, w_br_a, w_br_b, w_br_c, w_o, ffn_wi, ffn_wo):
    B, T, D = x.shape
    n_ctx = ctx.shape[1]
    L = ada_w.shape[0]
    cos, sin_a, sin_b = _rope_tables(T // GRID_W, n_ctx)
    xs = jnp.concatenate([ctx, x], axis=1)

    cc = jnp.zeros((16, D), F32).at[:B].set(c).at[B].set(c_ctx)
    mods = ada_mod(cc, ada_w, ada_b)
    mod_l = mods[:, :B].reshape(L, B, 6, 1, D)
    mod_c = jnp.broadcast_to(mods[:, B].reshape(L, 1, 6, 1, D), (L, B, 6, 1, D))
    mod12 = jnp.concatenate([mod_c, mod_l], axis=3).reshape(L, B, 12, D)

    gi = jnp.arange(LANES) // RW_HEAD
    bd = (gi[:, None] == gi[None, :]).astype(BF16)
    row = lambda t: t.reshape(1, -1)
    g0, d0, r0, t0 = 0, 2048, 5120, 5120 + 3072
    k0 = r0 + 3488

    for l in range(L):
        lam_init = 0.8 - 0.6 * math.exp(-0.3 * l)
        wl = w_in[l]
        w_cat = jnp.concatenate([wl[:, g0:d0], wl[:, d0:r0], wl[:, k0:], wl[:, r0:t0],
                                 wl[:, t0:k0], jnp.zeros((D, 96), F32)], axis=1).astype(BF16)
        mu = jnp.concatenate([jnp.zeros((COL_RW,), F32), rw_mu[l], jnp.zeros((96,), F32)]).reshape(1, -1)
        P = proj_in(xs, mod12[l], row(norm1_g[l]), w_cat, mu, n_ctx)

        bs_full = jnp.repeat(gm_bs[l].T, CHUNK, axis=1)
        a_br = gmlp(P, row(gm_v_g[l]), gm_ws[l].astype(BF16), bs_full)

        tile = lambda t, n: jnp.tile(t, n).reshape(1, -1)
        qn, kn, vb = qk_prep(P, cos, sin_a, sin_b, tile(da_q_g[l], 16), tile(da_k_g[l], 16), bd)
        b_br = attention(qn, kn, vb, da_lambda[l], row(da_subln_g[l]), n_ctx, lam_init)

        zpad = jnp.zeros((DECAY_LORA, D), F32)
        w2p = jnp.stack([jnp.concatenate([rw_w2[l, 0], zpad]), jnp.concatenate([zpad, rw_w2[l, 1]])]).astype(BF16)
        a2p = jnp.stack([jnp.concatenate([rw_a2[l, 0], zpad]), jnp.concatenate([zpad, rw_a2[l, 1]])]).astype(BF16)
        kkn, lw, asig = rw_prep(P, rw_w0[l], w2p, rw_a0[l], a2p, row(rw_kk[l]), bd)
        y = rw_scan(P, kkn, lw, asig, row(rw_ka[l]), n_ctx)

        g2p = jnp.concatenate([rw_g2[l], jnp.zeros((2 * LANES - GATE_LORA, D), F32)]).astype(BF16)
        xs = merge(xs, mod12[l], y, asig, P, a_br, b_br, row(rw_ln_w[l]), row(rw_ln_b[l]),
                   row(rw_rk[l]), row(rw_ka[l]), g2p, bd,
                   w_br_a[l].astype(BF16), w_br_b[l].astype(BF16), w_br_c[l].astype(BF16),
                   w_o[l].astype(BF16), n_ctx)
        xs = ffn(xs, mod12[l], row(norm2_g[l]), ffn_wi[l].astype(BF16), ffn_wo[l].astype(BF16), n_ctx)
    return xs[:, n_ctx:]
```

```python
import functools
import math

import jax
import jax.numpy as jnp
from jax import lax
from jax.experimental import pallas as pl
from jax.experimental.pallas import tpu as pltpu

D_MODEL = 1024
GRID_W = 64
CHUNK = 128
GM_GROUPS = 8
DA_HEADS = 8
DA_HEAD_DIM = 64
DA_V_DIM = 128
ROPE_BASE = 10000.0
RW_HEAD = 64
RW_HEADS = 16
DECAY_LORA = 64
AAA_LORA = 64
GATE_LORA = 160
D_FF = 2816
RMS_EPS = 1e-6
GN_EPS = 64e-5

COL_GM = 0
COL_DA = 2048
COL_GATE = 5120
COL_RW = 8192
COL_TAIL = 11264
NP_COLS = 11776
PROJ_TN = 512
RW_CHUNK = 64
ATTN_KB = 256
ATTN_AHEAD = 3
LANES = 128
VMEM_LIMIT = 56 * 1024 * 1024

BF16 = jnp.bfloat16
F32 = jnp.float32


def _dot(a, b):
    return jnp.dot(a, b, preferred_element_type=F32)


def _dot_nt(a, b):
    return lax.dot_general(a, b, (((1,), (1,)), ((), ())), preferred_element_type=F32)


def _split_dot(x, m):
    hi = x.astype(BF16)
    lo = (x - hi.astype(F32)).astype(BF16)
    return _dot(hi, m) + _dot(lo, m)


def _split3_dot_left(m, x):
    hi = x.astype(BF16)
    r1 = x - hi.astype(F32)
    mid = r1.astype(BF16)
    lo = (r1 - mid.astype(F32)).astype(BF16)
    return _dot(m, hi) + _dot(m, mid) + _dot(m, lo)


def _group_sum(x, bd, width=LANES):
    parts = [_split_dot(x[:, i:i + width], bd) for i in range(0, x.shape[1], width)]
    return parts[0] if len(parts) == 1 else jnp.concatenate(parts, axis=1)


def _cparams(sem):
    return pltpu.CompilerParams(dimension_semantics=sem, vmem_limit_bytes=VMEM_LIMIT)


def _ada_kernel(c_ref, w_ref, b_ref, o_ref):
    c = c_ref[...]
    s = c * jax.nn.sigmoid(c)
    o_ref[...] = jnp.dot(s, w_ref[...], precision=lax.Precision.HIGHEST,
                         preferred_element_type=F32) + b_ref[...]


def ada_mod(cc, ada_w, ada_b):
    L, D, N = ada_w.shape
    tn = 1536
    return pl.pallas_call(
        _ada_kernel,
        grid=(L, N // tn),
        in_specs=[pl.BlockSpec((16, D), lambda l, j: (0, 0)),
                  pl.BlockSpec((None, D, tn), lambda l, j: (l, 0, j)),
                  pl.BlockSpec((None, 1, tn), lambda l, j: (l, 0, j))],
        out_specs=pl.BlockSpec((None, 16, tn), lambda l, j: (l, 0, j)),
        out_shape=jax.ShapeDtypeStruct((L, 16, N), F32),
        compiler_params=_cparams(("arbitrary", "arbitrary")),
        name="ada_mod",
    )(cc, ada_w, ada_b.reshape(L, 1, N))


def _modulated_norm(x, g, m_ref, piece, row0, n_ctx):
    rows = row0 + lax.broadcasted_iota(jnp.int32, (x.shape[0], 1), 0)
    is_lat = rows >= n_ctx
    sh = jnp.where(is_lat, m_ref[2 * piece + 1:2 * piece + 2, :], m_ref[2 * piece:2 * piece + 1, :])
    sc = jnp.where(is_lat, m_ref[2 * piece + 3:2 * piece + 4, :], m_ref[2 * piece + 2:2 * piece + 3, :])
    y = x * lax.rsqrt(jnp.mean(x * x, axis=-1, keepdims=True) + RMS_EPS) * g
    return y * (1.0 + sc) + sh


def _gate_rows(m_ref, piece, row0, nrows, n_ctx):
    rows = row0 + lax.broadcasted_iota(jnp.int32, (nrows, 1), 0)
    return jnp.where(rows >= n_ctx, m_ref[2 * piece + 1:2 * piece + 2, :], m_ref[2 * piece:2 * piece + 1, :])


def _proj_kernel(x_ref, m_ref, g_ref, w_ref, mu_ref, o_ref, h_ref, p_ref, *, n_ctx, rb, rw_j0):
    S = x_ref.shape[0]
    j = pl.program_id(1)
    nblk = S // rb

    @pl.when(j == 0)
    def _():
        def body(i, carry):
            r0 = pl.multiple_of(i * rb, rb)
            x = x_ref[pl.ds(r0, rb), :]
            h_ref[pl.ds(r0, rb), :] = _modulated_norm(x, g_ref[...], m_ref, 0, r0, n_ctx).astype(BF16)
            return carry
        lax.fori_loop(0, nblk, body, 0)

    @pl.when(j < rw_j0)
    def _():
        def body(i, carry):
            r0 = pl.multiple_of(i * rb, rb)
            o_ref[pl.ds(r0, rb), :] = _dot(h_ref[pl.ds(r0, rb), :], w_ref[...])
            return carry
        lax.fori_loop(0, nblk, body, 0)

    @pl.when(j >= rw_j0)
    def _():
        zero = jnp.zeros((8, p_ref.shape[1]), F32)
        p_ref[0:8, :] = zero
        p_ref[S + 8:S + 16, :] = zero

        def body(i, carry):
            r0 = pl.multiple_of(i * rb, rb)
            p_ref[pl.ds(r0 + 8, rb), :] = _dot(h_ref[pl.ds(r0, rb), :], w_ref[...])
            return carry
        lax.fori_loop(0, nblk, body, 0)
        mu = mu_ref[...]
        for i in range(nblk):
            r0 = i * rb
            rows = r0 + lax.broadcasted_iota(jnp.int32, (rb, 1), 0)
            cur = p_ref[r0 + 8:r0 + 8 + rb, :]
            prev = jnp.where((rows == 0) | (rows == n_ctx), 0.0, p_ref[r0 + 7:r0 + 7 + rb, :])
            nxt = jnp.where((rows == n_ctx - 1) | (rows == S - 1), 0.0, p_ref[r0 + 9:r0 + 9 + rb, :])
            o_ref[r0:r0 + rb, :] = cur + mu * (0.5 * (prev + nxt) - cur)


def proj_in(xs, mod, g, w, mu, n_ctx):
    B, S, D = xs.shape
    NP = w.shape[1]
    tn = PROJ_TN
    kern = functools.partial(_proj_kernel, n_ctx=n_ctx, rb=384, rw_j0=COL_RW // tn)
    return pl.pallas_call(
        kern,
        grid=(B, NP // tn),
        in_specs=[pl.BlockSpec((None, S, D), lambda b, j: (b, 0, 0)),
                  pl.BlockSpec((None, 12, D), lambda b, j: (b, 0, 0)),
                  pl.BlockSpec((1, D), lambda b, j: (0, 0)),
                  pl.BlockSpec((D, tn), lambda b, j: (0, j)),
                  pl.BlockSpec((1, tn), lambda b, j: (0, j))],
        out_specs=pl.BlockSpec((None, S, tn), lambda b, j: (b, 0, j)),
        out_shape=jax.ShapeDtypeStruct((B, S, NP), F32),
        scratch_shapes=[pltpu.VMEM((S, D), BF16), pltpu.VMEM((S + 16, tn), F32)],
        compiler_params=_cparams(("arbitrary", "arbitrary")),
        name="proj_in",
    )(xs, mod, g, w, mu)


def _gelu(x):
    return 0.5 * x * (1.0 + lax.erf(x * (1.0 / math.sqrt(2.0))))


def _gmlp_kernel(p_ref, vg_ref, ws_ref, bs_ref, o_ref):
    W = vg_ref.shape[1]
    u = _gelu(p_ref[:, :W])
    v = _gelu(p_ref[:, W:])
    v = v * lax.rsqrt(jnp.mean(v * v, axis=-1, keepdims=True) + RMS_EPS) * vg_ref[...]
    vb = v.astype(BF16)
    gd = W // GM_GROUPS
    for g in range(GM_GROUPS):
        sl = slice(g * gd, (g + 1) * gd)
        f = _dot(ws_ref[g], vb[:, sl]) + bs_ref[:, sl]
        o_ref[:, sl] = (u[:, sl] * f).astype(o_ref.dtype)


def gmlp(P, vg, ws, bs_full):
    B, S, _ = P.shape
    W = vg.shape[1]
    return pl.pallas_call(
        _gmlp_kernel,
        grid=(B, S // CHUNK),
        in_specs=[pl.BlockSpec((None, CHUNK, 2 * W), lambda b, i: (b, i, 0)),
                  pl.BlockSpec((1, W), lambda b, i: (0, 0)),
                  pl.BlockSpec((GM_GROUPS, CHUNK, CHUNK), lambda b, i: (0, 0, 0)),
                  pl.BlockSpec((CHUNK, W), lambda b, i: (0, 0))],
        out_specs=pl.BlockSpec((None, CHUNK, W), lambda b, i: (b, i, 0)),
        out_shape=jax.ShapeDtypeStruct((B, S, W), BF16),
        compiler_params=_cparams(("arbitrary", "arbitrary")),
        name="gmlp",
    )(P, vg, ws, bs_full)


def _qk_kernel(q_ref, k_ref, v_ref, cos_ref, sa_ref, sb_ref, qg_ref, kg_ref, bd_ref,
               qo_ref, ko_ref, vo_ref):
    W = q_ref.shape[1]
    reps = W // LANES
    cos = jnp.tile(cos_ref[...], (1, reps))
    sin_a = jnp.tile(sa_ref[...], (1, reps))
    sin_b = jnp.tile(sb_ref[...], (1, reps))
    bd = bd_ref[...]
    half = DA_HEAD_DIM // 2

    def prep(t, g, scale):
        ms = _group_sum(t * t, bd) * (1.0 / DA_HEAD_DIM)
        t = t * lax.rsqrt(ms + RMS_EPS) * g
        t = t * cos + pltpu.roll(t, W - half, 1) * sin_a + pltpu.roll(t, half, 1) * sin_b
        return t * scale if scale != 1.0 else t

    qo_ref[...] = prep(q_ref[...], qg_ref[...], DA_HEAD_DIM ** -0.5 * math.log2(math.e)).astype(BF16)
    ko_ref[...] = prep(k_ref[...], kg_ref[...], 1.0).astype(BF16)
    for h in range(reps):
        vo_ref[h] = v_ref[:, h * LANES:(h + 1) * LANES].T.astype(BF16)


def qk_prep(P, cos, sin_a, sin_b, qg, kg, bd, tt=256):
    B, S, _ = P.shape
    W = D_MODEL
    c0 = COL_DA // W
    tok = lambda c: pl.BlockSpec((None, tt, W), lambda b, i, c=c: (b, i, c))
    tab = pl.BlockSpec((tt, LANES), lambda b, i: (i, 0))
    row = pl.BlockSpec((1, W), lambda b, i: (0, 0))
    out = pl.BlockSpec((None, tt, W), lambda b, i: (b, i, 0))
    shp = jax.ShapeDtypeStruct((B, S, W), BF16)
    return pl.pallas_call(
        _qk_kernel,
        grid=(B, S // tt),
        in_specs=[tok(c0), tok(c0 + 1), tok(c0 + 2), tab, tab, tab, row, row,
                  pl.BlockSpec((LANES, LANES), lambda b, i: (0, 0))],
        out_specs=[out, out, pl.BlockSpec((None, W // LANES, LANES, tt), lambda b, i: (b, 0, 0, i))],
        out_shape=[shp, shp, jax.ShapeDtypeStruct((B, W // LANES, LANES, S), BF16)],
        compiler_params=_cparams(("arbitrary", "arbitrary")),
        name="qk_prep",
    )(P, P, P, cos, sin_a, sin_b, qg, kg, bd)


def _attn_kernel(q_ref, k_ref, v_ref, lam_ref, sg_ref, o_ref, *, n_ctx, lam_init):
    i = pl.program_id(2)
    tq = q_ref.shape[0]
    lp = lam_ref[...]
    lam = (jnp.exp(jnp.sum(lp[0:1] * lp[1:2], axis=-1, keepdims=True))
           - jnp.exp(jnp.sum(lp[2:3] * lp[3:4], axis=-1, keepdims=True)) + lam_init)

    def attend(nk):
        q = q_ref[...]
        lane = lax.broadcasted_iota(jnp.int32, q.shape, 1)
        zero = jnp.zeros_like(q)
        qs = jnp.concatenate([jnp.where(lane < DA_HEAD_DIM, q, zero),
                              jnp.where(lane >= DA_HEAD_DIM, q, zero)], axis=0)
        m = l = acc = None
        nblk = nk // ATTN_KB
        blk = lambda j: slice(j * ATTN_KB, (j + 1) * ATTN_KB)
        scores = lambda j: _dot_nt(k_ref[blk(j), :], qs)
        queue = [scores(j) for j in range(min(ATTN_AHEAD, nblk))]
        for j in range(nblk):
            ks = blk(j)
            if j + ATTN_AHEAD < nblk:
                queue.append(scores(j + ATTN_AHEAD))
            s = queue.pop(0)
            m_blk = jnp.max(s, axis=0, keepdims=True)
            m_new = m_blk if m is None else jnp.maximum(m, m_blk)
            e = jnp.exp2(s - m_new)
            pv = _dot(v_ref[:, ks], e.astype(BF16))
            if m is None:
                l, acc = jnp.sum(e, axis=0, keepdims=True), pv
            else:
                alpha = jnp.exp2(m - m_new)
                l = alpha * l + jnp.sum(e, axis=0, keepdims=True)
                acc = alpha * acc + pv
            m = m_new
        inv = 1.0 / l
        o = acc[:, :tq] * inv[:, :tq] - acc[:, tq:] * (lam * inv[:, tq:])
        o = o * lax.rsqrt(jnp.mean(o * o, axis=0, keepdims=True) + RMS_EPS)
        o_ref[...] = (o.T * sg_ref[...] * (1.0 - lam_init)).astype(o_ref.dtype)

    @pl.when(i * tq < n_ctx)
    def _():
        attend(n_ctx)

    @pl.when(i * tq >= n_ctx)
    def _():
        attend(k_ref.shape[0])


def attention(qn, kn, vb, lam_p, sg, n_ctx, lam_init, tq=256):
    B, S, W = qn.shape
    H = W // LANES
    kern = functools.partial(_attn_kernel, n_ctx=n_ctx, lam_init=lam_init)
    kv = pl.BlockSpec((None, S, LANES), lambda b, h, i: (b, 0, h))
    return pl.pallas_call(
        kern,
        grid=(B, H, S // tq),
        in_specs=[pl.BlockSpec((None, tq, LANES), lambda b, h, i: (b, i, h)), kv,
                  pl.BlockSpec((None, None, LANES, S), lambda b, h, i: (b, h, 0, 0)),
                  pl.BlockSpec((4, DA_HEAD_DIM), lambda b, h, i: (0, 0)),
                  pl.BlockSpec((1, LANES), lambda b, h, i: (0, 0))],
        out_specs=pl.BlockSpec((None, tq, LANES), lambda b, h, i: (b, i, h)),
        out_shape=jax.ShapeDtypeStruct((B, S, W), BF16),
        compiler_params=_cparams(("arbitrary", "arbitrary", "arbitrary")),
        name="attention",
    )(qn, kn, vb, lam_p, sg)


def _neg_softplus_neg(x):
    y = -x
    return -(jnp.maximum(y, 0.0) + jnp.log1p(jnp.exp(-jnp.abs(y))))


def _rw_prep_kernel(zk_ref, zt_ref, w0_ref, w2_ref, a0_ref, a2_ref, kk_ref, bd_ref,
                    kkn_ref, lw_ref, as_ref):
    zt = zt_ref[...]
    tw = jnp.tanh(zt[:, 0:LANES]).astype(BF16)
    za = zt[:, LANES:2 * LANES].astype(BF16)
    for d in range(2):
        w_log = _neg_softplus_neg(w0_ref[d:d + 1, :] + _dot(tw, w2_ref[d])) - 0.5
        lw_ref[d] = -jnp.exp(w_log)
        as_ref[d] = jax.nn.sigmoid(a0_ref[d:d + 1, :] + _dot(za, a2_ref[d]))
    kk = zk_ref[...] * kk_ref[...]
    nrm = jnp.sqrt(_group_sum(kk * kk, bd_ref[...]))
    kkn_ref[...] = kk / jnp.maximum(nrm, 1e-12)


def rw_prep(P, w0, w2p, a0, a2p, kkp, bd, tt=256):
    B, S, _ = P.shape
    W = D_MODEL
    tokw = pl.BlockSpec((None, tt, W), lambda b, i: (b, i, COL_RW // W + 1))
    tail = pl.BlockSpec((None, tt, 512), lambda b, i: (b, i, COL_TAIL // 512))
    full = lambda shape: pl.BlockSpec(shape, lambda b, i: (0,) * len(shape))
    out2 = pl.BlockSpec((2, None, tt, W), lambda b, i: (0, b, i, 0))
    return pl.pallas_call(
        _rw_prep_kernel,
        grid=(B, S // tt),
        in_specs=[tokw, tail, full((2, W)), full((2, LANES, W)), full((2, W)), full((2, LANES, W)),
                  full((1, W)), full((LANES, LANES))],
        out_specs=[pl.BlockSpec((None, tt, W), lambda b, i: (b, i, 0)), out2, out2],
        out_shape=[jax.ShapeDtypeStruct((B, S, W), F32),
                   jax.ShapeDtypeStruct((2, B, S, W), F32),
                   jax.ShapeDtypeStruct((2, B, S, W), F32)],
        compiler_params=_cparams(("arbitrary", "arbitrary")),
        name="rw_prep",
    )(P, P, w0, w2p, a0, a2p, kkp, bd)


def _rw_scan_kernel(r_ref, zk_ref, v_ref, kkn_ref, lw_ref, as_ref, ka_ref, y_ref, h_ref):
    c = pl.program_id(1)
    C = RW_CHUNK
    R2 = 2 * C
    n_pairs = ka_ref.shape[1] // LANES

    @pl.when(c == 0)
    def _():
        h_ref[...] = jnp.zeros_like(h_ref)

    ri = lax.broadcasted_iota(jnp.int32, (C, C), 0)
    ci = lax.broadcasted_iota(jnp.int32, (C, C), 1)
    rt = lax.broadcasted_iota(jnp.int32, (C, LANES), 0)
    lane = lax.broadcasted_iota(jnp.int32, (C, LANES), 1)
    cs = lane & (C - 1)
    levels = int(math.log2(C))
    same_blk = [(rt >> k) == (cs >> k) for k in range(levels + 1)]
    joins = [same_blk[k + 1] & ~same_blk[k] for k in range(levels)]
    eye = jnp.where(rt == cs, 1.0, 0.0)
    first = lane < RW_HEAD
    hr = lax.broadcasted_iota(jnp.int32, (LANES, LANES), 0)
    hc = lax.broadcasted_iota(jnp.int32, (LANES, LANES), 1)
    same_head = (hr >= RW_HEAD) == (hc >= RW_HEAD)
    ka = ka_ref[...]

    def stack(x):
        return jnp.concatenate([jnp.where(first, x, 0.0), jnp.where(first, 0.0, x)], axis=0)

    chains = []
    for d in range(2):
        before_c = (ci <= ri) if d == 0 else (ci >= ri)
        strict = (cs < rt) if d == 0 else (cs > rt)
        incl = (cs <= rt) if d == 0 else (cs >= rt)
        lw = lw_ref[d][...]
        L = _split3_dot_left(jnp.where(before_c, 1.0, 0.0).astype(BF16), lw)
        ltot = jnp.sum(lw, axis=0, keepdims=True)
        lmid = 0.5 * ltot
        e_r = jnp.exp(L - lmid)
        e_a = jnp.exp(L - lw - lmid)
        e_b = jnp.exp(lmid - L)
        e_mid = jnp.exp(lmid)
        asig = as_ref[d][...]
        kkn = kkn_ref[d][...]
        kd = zk_ref[d][...] * (1.0 + (asig - 1.0) * ka)
        a_in = -kkn * e_a
        r_in = r_ref[d][...] * e_r
        b_in = kkn * asig * e_b
        k_in = kd * e_b
        a_h = a_in * e_mid
        r_h = r_in * e_mid
        b_o = b_in * e_mid
        k_o = k_in * e_mid
        wtot = jnp.exp(ltot)
        vv = v_ref[d][...]
        for p in range(n_pairs):
            sl = slice(p * LANES, (p + 1) * LANES)
            chains.append(dict(
                d=d, p=p, sl=sl, strict=strict, incl=incl, wtot=wtot[:, sl],
                AR=jnp.concatenate([a_in[:, sl], r_in[:, sl]], axis=0).astype(BF16),
                BK=jnp.concatenate([stack(b_in[:, sl]), stack(k_in[:, sl])], axis=0).astype(BF16),
                ARh=jnp.concatenate([a_h[:, sl], r_h[:, sl]], axis=0).astype(BF16),
                BKo=jnp.concatenate([b_o[:, sl], k_o[:, sl]], axis=0).astype(BF16),
                V=vv[:, sl]))

    for ch in chains:
        G1 = _dot_nt(ch["AR"], ch["BK"])
        ch["Aab"] = jnp.where(ch["strict"], G1[:C, :R2], 0.0)
        ch["Aak"] = jnp.where(ch["strict"], G1[:C, R2:], 0.0).astype(BF16)
        ch["Ar"] = jnp.concatenate([jnp.where(ch["incl"], G1[C:, :R2], 0.0),
                                    jnp.where(ch["incl"], G1[C:, R2:], 0.0)], axis=1).astype(BF16)
        ch["H"] = h_ref[ch["d"], ch["p"]]
        ch["T"] = eye + jnp.where(joins[0], ch["Aab"], 0.0)
    for ch in chains:
        ch["G2"] = _dot_nt(ch["ARh"], ch["H"].astype(BF16))
        ch["Vs"] = stack(ch["V"]).astype(BF16)
        ch["rhs"] = ch["G2"][:C] + _dot(ch["Aak"], ch["Vs"])
    for join in joins[1:]:
        for ch in chains:
            E = stack(jnp.where(join, ch["Aab"], 0.0)).astype(BF16)
            ch["TE"] = _dot(ch["T"].astype(BF16), E).astype(BF16)
        for ch in chains:
            ch["T"] = ch["T"] + _dot(ch["TE"], stack(ch["T"]).astype(BF16))
    for ch in chains:
        ch["U"] = _dot(ch["T"].astype(BF16), stack(ch["rhs"]).astype(BF16))
    for ch in chains:
        UVs = jnp.concatenate([stack(ch["U"]).astype(BF16), ch["Vs"]], axis=0)
        y_ref[ch["d"]][:, ch["sl"]] = ch["G2"][C:] + _dot(ch["Ar"], UVs)
    for ch in chains:
        UVt = jnp.concatenate([ch["U"], ch["V"]], axis=0).T.astype(BF16)
        upd = jnp.where(same_head, _dot(UVt, ch["BKo"]), 0.0)
        h_ref[ch["d"], ch["p"]] = ch["H"] * ch["wtot"] + upd


def rw_scan(P, kkn, lw, asig, ka, n_ctx):
    B, S, _ = P.shape
    W = D_MODEL
    C = RW_CHUNK
    nc = S // C
    ncc = n_ctx // C

    def rev(c):
        return jnp.where(c < ncc, ncc - 1 - c, nc - 1 + ncc - c)

    def both(make):
        return [make(lambda c: c), make(rev)]

    pcol = lambda col: both(lambda f: pl.BlockSpec((None, C, W), lambda b, c, f=f: (b, f(c), col)))
    tok = both(lambda f: pl.BlockSpec((None, C, W), lambda b, c, f=f: (b, f(c), 0)))
    dtok = [pl.BlockSpec((None, None, C, W), lambda b, c: (0, b, c, 0)),
            pl.BlockSpec((None, None, C, W), lambda b, c: (1, b, rev(c), 0))]
    c0 = COL_RW // W

    def kern(rf, rb_, kf, kb_, vf, vb_, nf, nb_, lwf, lwb, asf, asb, ka_ref, yf, yb, h_ref):
        _rw_scan_kernel((rf, rb_), (kf, kb_), (vf, vb_), (nf, nb_), (lwf, lwb), (asf, asb), ka_ref,
                        (yf, yb), h_ref)

    return pl.pallas_call(
        kern,
        grid=(B, nc),
        in_specs=[*pcol(c0), *pcol(c0 + 1), *pcol(c0 + 2), *tok, *dtok, *dtok,
                  pl.BlockSpec((1, W), lambda b, c: (0, 0))],
        out_specs=[pl.BlockSpec((None, C, W), lambda b, c: (b, c, 0)),
                   pl.BlockSpec((None, C, W), lambda b, c: (b, rev(c), 0))],
        out_shape=[jax.ShapeDtypeStruct((B, S, W), F32)] * 2,
        scratch_shapes=[pltpu.VMEM((2, W // LANES, LANES, LANES), F32)],
        compiler_params=_cparams(("arbitrary", "arbitrary")),
        name="rw_scan",
    )(P, P, P, P, P, P, kkn, kkn, lw, lw, asig, asig, ka)


def _merge_kernel(x_ref, m_ref, yf_ref, yb_ref, as_ref, r_ref, zk_ref, v_ref, zt_ref, ga_ref, gb_ref, gc_ref,
                  a_ref, b_ref, lnw_ref, lnb_ref, rk_ref, ka_ref, g2_ref, bd_ref,
                  wa_ref, wb_ref, wc_ref, wo_ref, o_ref, *, n_ctx):
    tt = x_ref.shape[0]
    row0 = pl.program_id(1) * tt
    bd = bd_ref[...]
    inv_n = 1.0 / RW_HEAD
    y = yf_ref[...] + yb_ref[...]
    mean = _group_sum(y, bd) * inv_n
    yc = y - mean
    var = _group_sum(yc * yc, bd) * inv_n
    yn = yc * lax.rsqrt(var + GN_EPS) * lnw_ref[...] + lnb_ref[...]
    zk = zk_ref[...]
    ka = ka_ref[...]
    kd_sum = zk * (1.0 + (as_ref[0] - 1.0) * ka) + zk * (1.0 + (as_ref[1] - 1.0) * ka)
    vv = v_ref[...]
    bonus = _group_sum(r_ref[...] * kd_sum * rk_ref[...], bd) * vv
    g = _dot(jax.nn.sigmoid(zt_ref[:, 2 * LANES:]).astype(BF16), g2_ref[...])
    c_br = ((yn + bonus) * g).astype(BF16)
    m = (jax.nn.sigmoid(ga_ref[...]) * _dot(a_ref[...], wa_ref[...])
         + jax.nn.sigmoid(gb_ref[...]) * _dot(b_ref[...], wb_ref[...])
         + jax.nn.sigmoid(gc_ref[...]) * _dot(c_br, wc_ref[...]))
    out = _dot(m.astype(BF16), wo_ref[...])
    o_ref[...] = x_ref[...] + _gate_rows(m_ref, 2, row0, tt, n_ctx) * out


def merge(xs, mod, y, asig, P, a_br, b_br, lnw, lnb, rk, ka, g2p, bd, wa, wb, wc, wo, n_ctx, tt=256):
    B, S, W = xs.shape
    tok = pl.BlockSpec((None, tt, W), lambda b, i: (b, i, 0))
    pcol = lambda col: pl.BlockSpec((None, tt, W), lambda b, i, col=col: (b, i, col))
    dtok = pl.BlockSpec((2, None, tt, W), lambda b, i: (0, b, i, 0))
    row = pl.BlockSpec((1, W), lambda b, i: (0, 0))
    wsq = pl.BlockSpec((W, W), lambda b, i: (0, 0))
    cr, cg = COL_RW // W, COL_GATE // W
    kern = functools.partial(_merge_kernel, n_ctx=n_ctx)
    return pl.pallas_call(
        kern,
        grid=(B, S // tt),
        in_specs=[tok, pl.BlockSpec((None, 12, W), lambda b, i: (b, 0, 0)), tok, tok, dtok,
                  pcol(cr), pcol(cr + 1), pcol(cr + 2),
                  pl.BlockSpec((None, tt, 512), lambda b, i: (b, i, COL_TAIL // 512)),
                  pcol(cg), pcol(cg + 1), pcol(cg + 2), tok, tok,
                  row, row, row, row,
                  pl.BlockSpec((2 * LANES, W), lambda b, i: (0, 0)),
                  pl.BlockSpec((LANES, LANES), lambda b, i: (0, 0)),
                  wsq, wsq, wsq, wsq],
        out_specs=tok,
        out_shape=jax.ShapeDtypeStruct((B, S, W), F32),
        compiler_params=_cparams(("arbitrary", "arbitrary")),
        name="merge",
    )(xs, mod, y[0], y[1], asig, P, P, P, P, P, P, P, a_br, b_br, lnw, lnb, rk, ka, g2p, bd, wa, wb, wc, wo)


def _ffn_kernel(x_ref, m_ref, g_ref, wg_ref, wu_ref, wo_ref, o_ref, h_ref, acc_ref, *, n_ctx, tiles_per_seq):
    j = pl.program_id(1)
    tm = x_ref.shape[0]
    row0 = (pl.program_id(0) % tiles_per_seq) * tm

    @pl.when(j == 0)
    def _():
        h_ref[...] = _modulated_norm(x_ref[...], g_ref[...], m_ref, 3, row0, n_ctx).astype(BF16)
        acc_ref[...] = jnp.zeros_like(acc_ref)

    h = h_ref[...]
    gt = _dot(h, wg_ref[...])
    up = _dot(h, wu_ref[...])
    act = (gt * jax.nn.sigmoid(gt) * up).astype(BF16)
    acc_ref[...] += _dot(act, wo_ref[...])

    @pl.when(j == pl.num_programs(1) - 1)
    def _():
        o_ref[...] = x_ref[...] + _gate_rows(m_ref, 5, row0, tm, n_ctx) * acc_ref[...]


def ffn(xs, mod, g, wi, wo, n_ctx, tm=768, tf=256):
    B, S, W = xs.shape
    FF = wo.shape[0]
    tps = S // tm
    nf = FF // tf
    kern = functools.partial(_ffn_kernel, n_ctx=n_ctx, tiles_per_seq=tps)
    tok = pl.BlockSpec((None, tm, W), lambda t, j: (t // tps, t % tps, 0))
    return pl.pallas_call(
        kern,
        grid=(B * tps, nf),
        in_specs=[tok, pl.BlockSpec((None, 12, W), lambda t, j: (t // tps, 0, 0)),
                  pl.BlockSpec((1, W), lambda t, j: (0, 0)),
                  pl.BlockSpec((W, tf), lambda t, j: (0, j)),
                  pl.BlockSpec((W, tf), lambda t, j: (0, nf + j)),
                  pl.BlockSpec((tf, W), lambda t, j: (j, 0))],
        out_specs=tok,
        out_shape=jax.ShapeDtypeStruct((B, S, W), F32),
        scratch_shapes=[pltpu.VMEM((tm, W), BF16), pltpu.VMEM((tm, W), F32)],
        compiler_params=_cparams(("arbitrary", "arbitrary")),
        name="ffn",
    )(xs, mod, g, wi, wi, wo)


def _rope_tables(n_rows, n_ctx):
    row = jnp.repeat(jnp.arange(n_rows), GRID_W).astype(F32)
    col = jnp.tile(jnp.arange(GRID_W), n_rows).astype(F32)
    axis_dim = DA_HEAD_DIM // 2
    inv_freq = ROPE_BASE ** (-jnp.arange(0, axis_dim, 2, dtype=F32) / axis_dim)
    ang = jnp.concatenate([row[:, None] * inv_freq, col[:, None] * inv_freq], axis=-1)
    ang = jnp.concatenate([ang, ang], axis=-1)
    cos, sin = jnp.cos(ang), jnp.sin(ang)
    cos = jnp.concatenate([jnp.ones((n_ctx, DA_HEAD_DIM), F32), cos], axis=0)
    sin = jnp.concatenate([jnp.zeros((n_ctx, DA_HEAD_DIM), F32), sin], axis=0)
    first = jnp.arange(DA_HEAD_DIM) < DA_HEAD_DIM // 2
    sin_a = jnp.where(first, -sin, 0.0)
    sin_b = jnp.where(first, 0.0, sin)
    dup = lambda t: jnp.concatenate([t, t], axis=-1)
    return dup(cos), dup(sin_a), dup(sin_b)


def kernel(x, c, ctx, c_ctx, ada_w, ada_b, norm1_g, norm2_g, w_in, gm_v_g, gm_ws, gm_bs,
           da_q_g, da_k_g, da_lambda, da_subln_g, rw_mu, rw_w0, rw_w2, rw_a0, rw_a2, rw_g2,
           rw_kk, rw_ka, rw_rk, rw_ln_w, rw_ln_b, w_br_a, w_br_b, w_br_c, w_o, ffn_wi, ffn_wo):
    B, T, D = x.shape
    n_ctx = ctx.shape[1]
    L = ada_w.shape[0]
    cos, sin_a, sin_b = _rope_tables(T // GRID_W, n_ctx)
    xs = jnp.concatenate([ctx, x], axis=1)

    cc = jnp.zeros((16, D), F32).at[:B].set(c).at[B].set(c_ctx)
    mods = ada_mod(cc, ada_w, ada_b)
    mod_l = mods[:, :B].reshape(L, B, 6, 1, D)
    mod_c = jnp.broadcast_to(mods[:, B].reshape(L, 1, 6, 1, D), (L, B, 6, 1, D))
    mod12 = jnp.concatenate([mod_c, mod_l], axis=3).reshape(L, B, 12, D)

    gi = jnp.arange(LANES) // RW_HEAD
    bd = (gi[:, None] == gi[None, :]).astype(BF16)
    row = lambda t: t.reshape(1, -1)
    g0, d0, r0, t0 = 0, 2048, 5120, 5120 + 3072
    k0 = r0 + 3488

    for l in range(L):
        lam_init = 0.8 - 0.6 * math.exp(-0.3 * l)
        wl = w_in[l]
        w_cat = jnp.concatenate([wl[:, g0:d0], wl[:, d0:r0], wl[:, k0:], wl[:, r0:t0],
                                 wl[:, t0:k0], jnp.zeros((D, 96), F32)], axis=1).astype(BF16)
        mu = jnp.concatenate([jnp.zeros((COL_RW,), F32), rw_mu[l], jnp.zeros((96,), F32)]).reshape(1, -1)
        P = proj_in(xs, mod12[l], row(norm1_g[l]), w_cat, mu, n_ctx)

        bs_full = jnp.repeat(gm_bs[l].T, CHUNK, axis=1)
        a_br = gmlp(P, row(gm_v_g[l]), gm_ws[l].astype(BF16), bs_full)

        tile = lambda t, n: jnp.tile(t, n).reshape(1, -1)
        qn, kn, vb = qk_prep(P, cos, sin_a, sin_b, tile(da_q_g[l], 16), tile(da_k_g[l], 16), bd)
        b_br = attention(qn, kn, vb, da_lambda[l], row(da_subln_g[l]), n_ctx, lam_init)

        zpad = jnp.zeros((DECAY_LORA, D), F32)
        w2p = jnp.stack([jnp.concatenate([rw_w2[l, 0], zpad]), jnp.concatenate([zpad, rw_w2[l, 1]])]).astype(BF16)
        a2p = jnp.stack([jnp.concatenate([rw_a2[l, 0], zpad]), jnp.concatenate([zpad, rw_a2[l, 1]])]).astype(BF16)
        kkn, lw, asig = rw_prep(P, rw_w0[l], w2p, rw_a0[l], a2p, row(rw_kk[l]), bd)
        y = rw_scan(P, kkn, lw, asig, row(rw_ka[l]), n_ctx)

        g2p = jnp.concatenate([rw_g2[l], jnp.zeros((2 * LANES - GATE_LORA, D), F32)]).astype(BF16)
        xs = merge(xs, mod12[l], y, asig, P, a_br, b_br, row(rw_ln_w[l]), row(rw_ln_b[l]),
                   row(rw_rk[l]), row(rw_ka[l]), g2p, bd,
                   w_br_a[l].astype(BF16), w_br_b[l].astype(BF16), w_br_c[l].astype(BF16),
                   w_o[l].astype(BF16), n_ctx)
        xs = ffn(xs, mod12[l], row(norm2_g[l]), ffn_wi[l].astype(BF16), ffn_wo[l].astype(BF16), n_ctx)
    return xs[:, n_ctx:]
```

```python
import functools
import math

import jax
import jax.numpy as jnp
from jax import lax
from jax.experimental import pallas as pl
from jax.experimental.pallas import tpu as pltpu

D_MODEL = 1024
GRID_W = 64
CHUNK = 128
GM_GROUPS = 8
DA_HEADS = 8
DA_HEAD_DIM = 64
DA_V_DIM = 128
ROPE_BASE = 10000.0
RW_HEAD = 64
RW_HEADS = 16
DECAY_LORA = 64
AAA_LORA = 64
GATE_LORA = 160
D_FF = 2816
RMS_EPS = 1e-6
GN_EPS = 64e-5

COL_GM = 0
COL_DA = 2048
COL_GATE = 5120
COL_RW = 8192
COL_TAIL = 11264
NP_COLS = 11776
PROJ_TN = 512
RW_CHUNK = 64
ATTN_KB = 256
ATTN_AHEAD = 3
LANES = 128
GROUP_W = 256
VMEM_LIMIT = 56 * 1024 * 1024

BF16 = jnp.bfloat16
F32 = jnp.float32


def _dot(a, b):
    return jnp.dot(a, b, preferred_element_type=F32)


def _dot_nt(a, b):
    return lax.dot_general(a, b, (((1,), (1,)), ((), ())), preferred_element_type=F32)


def _split_dot(x, m):
    hi = x.astype(BF16)
    lo = (x - hi.astype(F32)).astype(BF16)
    return _dot(hi, m) + _dot(lo, m)


def _split3_dot_left(m, x):
    hi = x.astype(BF16)
    r1 = x - hi.astype(F32)
    mid = r1.astype(BF16)
    lo = (r1 - mid.astype(F32)).astype(BF16)
    return _dot(m, hi) + _dot(m, mid) + _dot(m, lo)


def _group_sum(x, bd):
    width = bd.shape[0]
    parts = [_split_dot(x[:, i:i + width], bd) for i in range(0, x.shape[1], width)]
    return parts[0] if len(parts) == 1 else jnp.concatenate(parts, axis=1)


def _cparams(sem):
    return pltpu.CompilerParams(dimension_semantics=sem, vmem_limit_bytes=VMEM_LIMIT)


def _ada_kernel(c_ref, w_ref, b_ref, o_ref):
    c = c_ref[...]
    s = c * jax.nn.sigmoid(c)
    o_ref[...] = jnp.dot(s, w_ref[...], precision=lax.Precision.HIGHEST,
                         preferred_element_type=F32) + b_ref[...]


def ada_mod(cc, ada_w, ada_b):
    L, D, N = ada_w.shape
    tn = 1536
    return pl.pallas_call(
        _ada_kernel,
        grid=(L, N // tn),
        in_specs=[pl.BlockSpec((16, D), lambda l, j: (0, 0)),
                  pl.BlockSpec((None, D, tn), lambda l, j: (l, 0, j)),
                  pl.BlockSpec((None, 1, tn), lambda l, j: (l, 0, j))],
        out_specs=pl.BlockSpec((None, 16, tn), lambda l, j: (l, 0, j)),
        out_shape=jax.ShapeDtypeStruct((L, 16, N), F32),
        compiler_params=_cparams(("arbitrary", "arbitrary")),
        name="ada_mod",
    )(cc, ada_w, ada_b.reshape(L, 1, N))


def _modulated_norm(x, g, m_ref, piece, row0, n_ctx):
    rows = row0 + lax.broadcasted_iota(jnp.int32, (x.shape[0], 1), 0)
    is_lat = rows >= n_ctx
    sh = jnp.where(is_lat, m_ref[2 * piece + 1:2 * piece + 2, :], m_ref[2 * piece:2 * piece + 1, :])
    sc = jnp.where(is_lat, m_ref[2 * piece + 3:2 * piece + 4, :], m_ref[2 * piece + 2:2 * piece + 3, :])
    y = x * lax.rsqrt(jnp.mean(x * x, axis=-1, keepdims=True) + RMS_EPS) * g
    return y * (1.0 + sc) + sh


def _gate_rows(m_ref, piece, row0, nrows, n_ctx):
    rows = row0 + lax.broadcasted_iota(jnp.int32, (nrows, 1), 0)
    return jnp.where(rows >= n_ctx, m_ref[2 * piece + 1:2 * piece + 2, :], m_ref[2 * piece:2 * piece + 1, :])


def _proj_kernel(x_ref, m_ref, g_ref, w_ref, mu_ref, o_ref, h_ref, p_ref, *, n_ctx, rb, mb, rw_j0):
    S = x_ref.shape[0]
    j = pl.program_id(1)
    nblk = S // rb

    @pl.when(j == 0)
    def _():
        def body(i, carry):
            r0 = pl.multiple_of(i * rb, rb)
            x = x_ref[pl.ds(r0, rb), :]
            h_ref[pl.ds(r0, rb), :] = _modulated_norm(x, g_ref[...], m_ref, 0, r0, n_ctx).astype(BF16)
            return carry
        lax.fori_loop(0, nblk, body, 0)

    def project(store):
        store(0, n_ctx, _dot(h_ref[0:n_ctx, :], w_ref[...]))

        def body(i, carry):
            r0 = pl.multiple_of(n_ctx + i * mb, math.gcd(n_ctx, mb))
            store(r0, mb, _dot(h_ref[pl.ds(r0, mb), :], w_ref[...]))
            return carry
        lax.fori_loop(0, (S - n_ctx) // mb, body, 0)

    @pl.when(j < rw_j0)
    def _():
        def store(r0, rows, val):
            o_ref[pl.ds(r0, rows), :] = val.astype(o_ref.dtype)
        project(store)

    @pl.when(j >= rw_j0)
    def _():
        zero = jnp.zeros((8, p_ref.shape[1]), F32)
        p_ref[0:8, :] = zero
        p_ref[n_ctx + 8:n_ctx + 16, :] = zero
        p_ref[S + 16:S + 24, :] = zero

        def store(r0, rows, val):
            off = 8 if isinstance(r0, int) and r0 < n_ctx else 16
            p_ref[pl.ds(r0 + off, rows), :] = val
        project(store)
        c_cur =1.0 - mu_ref[...]
        c_nb = 0.5 * mu_ref[...]
        for i in range(nblk):
            r0 = i * rb
            src = r0 + (16 if r0 >= n_ctx else 8)
            nb = p_ref[src - 1:src - 1 + rb, :] + p_ref[src + 1:src + 1 + rb, :]
            o_ref[r0:r0 + rb, :] = (p_ref[src:src + rb, :] * c_cur + nb * c_nb).astype(o_ref.dtype)


def proj_in(xs, mod, g, w, mu, n_ctx):
    B, S, D = xs.shape
    NP = w.shape[1]
    tn = PROJ_TN
    kern = functools.partial(_proj_kernel, n_ctx=n_ctx, rb=n_ctx, mb=min(1024, S - n_ctx), rw_j0=COL_RW // tn)
    return pl.pallas_call(
        kern,
        grid=(B, NP // tn),
        in_specs=[pl.BlockSpec((None, S, D), lambda b, j: (b, 0, 0)),
                  pl.BlockSpec((None, 12, D), lambda b, j: (b, 0, 0)),
                  pl.BlockSpec((1, D), lambda b, j: (0, 0)),
                  pl.BlockSpec((D, tn), lambda b, j: (0, j)),
                  pl.BlockSpec((1, tn), lambda b, j: (0, j))],
        out_specs=pl.BlockSpec((None, S, tn), lambda b, j: (b, 0, j)),
        out_shape=jax.ShapeDtypeStruct((B, S, NP), BF16),
        scratch_shapes=[pltpu.VMEM((S, D), BF16), pltpu.VMEM((S + 24, tn), F32)],
        compiler_params=_cparams(("arbitrary", "arbitrary")),
        name="proj_in",
    )(xs, mod, g, w, mu)


def _gelu(x):
    return 0.5 * x * (1.0 + lax.erf(x * (1.0 / math.sqrt(2.0))))


def _gmlp_kernel(p_ref, vg_ref, ws_ref, bs_ref, o_ref):
    W = vg_ref.shape[1]
    u = _gelu(p_ref[:, :W].astype(F32))
    v = _gelu(p_ref[:, W:].astype(F32))
    v = v * lax.rsqrt(jnp.mean(v * v, axis=-1, keepdims=True) + RMS_EPS) * vg_ref[...]
    vb = v.astype(BF16)
    gd = W // GM_GROUPS
    for g in range(GM_GROUPS):
        sl = slice(g * gd, (g + 1) * gd)
        f = _dot(ws_ref[g], vb[:, sl]) + bs_ref[:, sl]
        o_ref[:, sl] = (u[:, sl] * f).astype(o_ref.dtype)


def gmlp(P, vg, ws, bs_full):
    B, S, _ = P.shape
    W = vg.shape[1]
    return pl.pallas_call(
        _gmlp_kernel,
        grid=(B, S // CHUNK),
        in_specs=[pl.BlockSpec((None, CHUNK, 2 * W), lambda b, i: (b, i, 0)),
                  pl.BlockSpec((1, W), lambda b, i: (0, 0)),
                  pl.BlockSpec((GM_GROUPS, CHUNK, CHUNK), lambda b, i: (0, 0, 0)),
                  pl.BlockSpec((CHUNK, W), lambda b, i: (0, 0))],
        out_specs=pl.BlockSpec((None, CHUNK, W), lambda b, i: (b, i, 0)),
        out_shape=jax.ShapeDtypeStruct((B, S, W), BF16),
        compiler_params=_cparams(("arbitrary", "arbitrary")),
        name="gmlp",
    )(P, vg, ws, bs_full)


def _qk_kernel(q_ref, k_ref, v_ref, cos_ref, sa_ref, sb_ref, qg_ref, kg_ref, bd_ref,
               qo_ref, ko_ref, vo_ref):
    W = q_ref.shape[1]
    reps = W // LANES
    cos = jnp.tile(cos_ref[...], (1, reps))
    sin_a = jnp.tile(sa_ref[...], (1, reps))
    sin_b = jnp.tile(sb_ref[...], (1, reps))
    bd = bd_ref[...]
    half = DA_HEAD_DIM // 2

    def prep(t, g, scale):
        ms = _group_sum(t * t, bd) * (1.0 / DA_HEAD_DIM)
        t = t * lax.rsqrt(ms + RMS_EPS) * g
        t = t * cos + pltpu.roll(t, W - half, 1) * sin_a + pltpu.roll(t, half, 1) * sin_b
        return t * scale if scale != 1.0 else t

    qo_ref[...] = prep(q_ref[...].astype(F32), qg_ref[...], DA_HEAD_DIM ** -0.5 * math.log2(math.e)).astype(BF16)
    ko_ref[...] = prep(k_ref[...].astype(F32), kg_ref[...], 1.0).astype(BF16)
    for h in range(reps):
        vo_ref[h] = v_ref[:, h * LANES:(h + 1) * LANES].astype(F32).T.astype(BF16)


def qk_prep(P, cos, sin_a, sin_b, qg, kg, bd, tt=256):
    B, S, _ = P.shape
    W = D_MODEL
    c0 = COL_DA // W
    tok = lambda c: pl.BlockSpec((None, tt, W), lambda b, i, c=c: (b, i, c))
    tab = pl.BlockSpec((tt, LANES), lambda b, i: (i, 0))
    row = pl.BlockSpec((1, W), lambda b, i: (0, 0))
    out = pl.BlockSpec((None, tt, W), lambda b, i: (b, i, 0))
    shp = jax.ShapeDtypeStruct((B, S, W), BF16)
    return pl.pallas_call(
        _qk_kernel,
        grid=(B, S // tt),
        in_specs=[tok(c0), tok(c0 + 1), tok(c0 + 2), tab, tab, tab, row, row,
                  pl.BlockSpec((GROUP_W, GROUP_W), lambda b, i: (0, 0))],
        out_specs=[out, out, pl.BlockSpec((None, W // LANES, LANES, tt), lambda b, i: (b, 0, 0, i))],
        out_shape=[shp, shp, jax.ShapeDtypeStruct((B, W // LANES, LANES, S), BF16)],
        compiler_params=_cparams(("arbitrary", "arbitrary")),
        name="qk_prep",
    )(P, P, P, cos, sin_a, sin_b, qg, kg, bd)


def _attn_kernel(q_ref, k_ref, v_ref, lam_ref, sg_ref, o_ref, *, n_ctx, lam_init):
    i = pl.program_id(2)
    tq = q_ref.shape[0]
    lp = lam_ref[...]
    lam = (jnp.exp(jnp.sum(lp[0:1] * lp[1:2], axis=-1, keepdims=True))
           - jnp.exp(jnp.sum(lp[2:3] * lp[3:4], axis=-1, keepdims=True)) + lam_init)

    def attend(nk):
        q = q_ref[...]
        lane = lax.broadcasted_iota(jnp.int32, q.shape, 1)
        zero = jnp.zeros_like(q)
        qs = jnp.concatenate([jnp.where(lane < DA_HEAD_DIM, q, zero),
                              jnp.where(lane >= DA_HEAD_DIM, q, zero)], axis=0)
        m = l = acc = None
        nblk = nk // ATTN_KB
        blk = lambda j: slice(j * ATTN_KB, (j + 1) * ATTN_KB)
        scores = lambda j: _dot_nt(k_ref[blk(j), :], qs)
        queue = [scores(j) for j in range(min(ATTN_AHEAD, nblk))]
        for j in range(nblk):
            ks = blk(j)
            if j + ATTN_AHEAD < nblk:
                queue.append(scores(j + ATTN_AHEAD))
            s = queue.pop(0)
            m_blk = jnp.max(s, axis=0, keepdims=True)
            m_new = m_blk if m is None else jnp.maximum(m, m_blk)
            e = jnp.exp2(s - m_new)
            pv = _dot(v_ref[:, ks], e.astype(BF16))
            if m is None:
                l, acc = jnp.sum(e, axis=0, keepdims=True), pv
            else:
                alpha = jnp.exp2(m - m_new)
                l = alpha * l + jnp.sum(e, axis=0, keepdims=True)
                acc = alpha * acc + pv
            m = m_new
        inv = 1.0 / l
        o = acc[:, :tq] * inv[:, :tq] - acc[:, tq:] * (lam * inv[:, tq:])
        o = o * lax.rsqrt(jnp.mean(o * o, axis=0, keepdims=True) + RMS_EPS)
        o_ref[...] = (o.T * sg_ref[...] * (1.0 - lam_init)).astype(o_ref.dtype)

    @pl.when(i * tq < n_ctx)
    def _():
        attend(n_ctx)

    @pl.when(i * tq >= n_ctx)
    def _():
        attend(k_ref.shape[0])


def attention(qn, kn, vb, lam_p, sg, n_ctx, lam_init, tq=256):
    B, S, W = qn.shape
    H = W // LANES
    kern = functools.partial(_attn_kernel, n_ctx=n_ctx, lam_init=lam_init)
    kv = pl.BlockSpec((None, S, LANES), lambda b, h, i: (b, 0, h))
    return pl.pallas_call(
        kern,
        grid=(B, H, S // tq),
        in_specs=[pl.BlockSpec((None, tq, LANES), lambda b, h, i: (b, i, h)), kv,
                  pl.BlockSpec((None, None, LANES, S), lambda b, h, i: (b, h, 0, 0)),
                  pl.BlockSpec((4, DA_HEAD_DIM), lambda b, h, i: (0, 0)),
                  pl.BlockSpec((1, LANES), lambda b, h, i: (0, 0))],
        out_specs=pl.BlockSpec((None, tq, LANES), lambda b, h, i: (b, i, h)),
        out_shape=jax.ShapeDtypeStruct((B, S, W), BF16),
        compiler_params=_cparams(("arbitrary", "arbitrary", "arbitrary")),
        name="attention",
    )(qn, kn, vb, lam_p, sg)


def _neg_softplus_neg(x):
    y = -x
    return -(jnp.maximum(y, 0.0) + jnp.log1p(jnp.exp(-jnp.abs(y))))


def _rw_prep_kernel(zk_ref, zt_ref, w0_ref, w2_ref, a0_ref, a2_ref, kk_ref, bd_ref,
                    kkn_ref, lw_ref, as_ref):
    tw = jnp.tanh(zt_ref[:, 0:LANES].astype(F32)).astype(BF16)
    za = zt_ref[:, LANES:2 * LANES]
    for d in range(2):
        w_log = _neg_softplus_neg(w0_ref[d:d + 1, :] + _dot(tw, w2_ref[d])) - 0.5
        lw_ref[d] = -jnp.exp(w_log)
        as_ref[d] = jax.nn.sigmoid(a0_ref[d:d + 1, :] + _dot(za, a2_ref[d])).astype(as_ref.dtype)
    kk = zk_ref[...].astype(F32) * kk_ref[...]
    nrm = jnp.sqrt(_group_sum(kk * kk, bd_ref[...]))
    kkn_ref[...] = (kk / jnp.maximum(nrm, 1e-12)).astype(kkn_ref.dtype)


def rw_prep(P, w0, w2p, a0, a2p, kkp, bd, tt=256):
    B, S, _ = P.shape
    W = D_MODEL
    tokw = pl.BlockSpec((None, tt, W), lambda b, i: (b, i, COL_RW // W + 1))
    tail = pl.BlockSpec((None, tt, 512), lambda b, i: (b, i, COL_TAIL // 512))
    full = lambda shape: pl.BlockSpec(shape, lambda b, i: (0,) * len(shape))
    out2 = pl.BlockSpec((2, None, tt, W), lambda b, i: (0, b, i, 0))
    return pl.pallas_call(
        _rw_prep_kernel,
        grid=(B, S // tt),
        in_specs=[tokw, tail, full((2, W)), full((2, LANES, W)), full((2, W)), full((2, LANES, W)),
                  full((1, W)), full((GROUP_W, GROUP_W))],
        out_specs=[pl.BlockSpec((None, tt, W), lambda b, i: (b, i, 0)), out2, out2],
        out_shape=[jax.ShapeDtypeStruct((B, S, W), BF16),
                   jax.ShapeDtypeStruct((2, B, S, W), F32),
                   jax.ShapeDtypeStruct((2, B, S, W), BF16)],
        compiler_params=_cparams(("arbitrary", "arbitrary")),
        name="rw_prep",
    )(P, P, w0, w2p, a0, a2p, kkp, bd)


def _rw_scan_kernel(r_ref, zk_ref, v_ref, kkn_ref, lw_ref, as_ref, ka_ref, y_ref, h_ref):
    c = pl.program_id(1)
    C = RW_CHUNK
    R2 = 2 * C
    n_pairs = ka_ref.shape[1] // LANES

    @pl.when(c == 0)
    def _():
        h_ref[...] = jnp.zeros_like(h_ref)

    ri = lax.broadcasted_iota(jnp.int32, (C, C), 0)
    ci = lax.broadcasted_iota(jnp.int32, (C, C), 1)
    rt = lax.broadcasted_iota(jnp.int32, (C, LANES), 0)
    lane = lax.broadcasted_iota(jnp.int32, (C, LANES), 1)
    cs = lane & (C - 1)
    levels = int(math.log2(C))
    same_blk = [(rt >> k) == (cs >> k) for k in range(levels + 1)]
    joins = [same_blk[k + 1] & ~same_blk[k] for k in range(levels)]
    eye = jnp.where(rt == cs, 1.0, 0.0)
    first = lane < RW_HEAD
    hr = lax.broadcasted_iota(jnp.int32, (LANES, LANES), 0)
    hc = lax.broadcasted_iota(jnp.int32, (LANES, LANES), 1)
    same_head = (hr >= RW_HEAD) == (hc >= RW_HEAD)
    ka = ka_ref[...]

    def stack(x):
        return jnp.concatenate([jnp.where(first, x, 0.0), jnp.where(first, 0.0, x)], axis=0)

    chains = []
    for d in range(2):
        before_c = (ci <= ri) if d == 0 else (ci >= ri)
        strict = (cs < rt) if d == 0 else (cs > rt)
        incl = (cs <= rt) if d == 0 else (cs >= rt)
        lw = lw_ref[d][...]
        L = _split3_dot_left(jnp.where(before_c, 1.0, 0.0).astype(BF16), lw)
        ltot = jnp.sum(lw, axis=0, keepdims=True)
        lmid = 0.5 * ltot
        e_r = jnp.exp(L - lmid)
        e_a = jnp.exp(L - lw - lmid)
        e_b = jnp.exp(lmid - L)
        e_mid = jnp.exp(lmid)
        asig = as_ref[d][...].astype(F32)
        kkn = kkn_ref[d][...].astype(F32)
        kd = zk_ref[d][...].astype(F32) * (1.0 + (asig - 1.0) * ka)
        a_in = -kkn * e_a
        r_in = r_ref[d][...].astype(F32) * e_r
        b_in = kkn * asig * e_b
        k_in = kd * e_b
        a_h = a_in * e_mid
        r_h = r_in * e_mid
        b_o = b_in * e_mid
        k_o = k_in * e_mid
        wtot = jnp.exp(ltot)
        vv = v_ref[d][...].astype(F32)
        for p in range(n_pairs):
            sl = slice(p * LANES, (p + 1) * LANES)
            chains.append(dict(
                d=d, p=p, sl=sl, strict=strict, incl=incl, wtot=wtot[:, sl],
                AR=jnp.concatenate([a_in[:, sl], r_in[:, sl]], axis=0).astype(BF16),
                BK=jnp.concatenate([stack(b_in[:, sl]), stack(k_in[:, sl])], axis=0).astype(BF16),
                ARh=jnp.concatenate([a_h[:, sl], r_h[:, sl]], axis=0).astype(BF16),
                BKo=jnp.concatenate([b_o[:, sl], k_o[:, sl]], axis=0).astype(BF16),
                V=vv[:, sl]))

    for ch in chains:
        G1 = _dot_nt(ch["AR"], ch["BK"])
        ch["Aab"] = jnp.where(ch["strict"], G1[:C, :R2], 0.0)
        ch["Aak"] = jnp.where(ch["strict"], G1[:C, R2:], 0.0).astype(BF16)
        ch["Ar"] = jnp.concatenate([jnp.where(ch["incl"], G1[C:, :R2], 0.0),
                                    jnp.where(ch["incl"], G1[C:, R2:], 0.0)], axis=1).astype(BF16)
        ch["H"] = h_ref[ch["d"], ch["p"]]
        ch["T"] = eye + jnp.where(joins[0], ch["Aab"], 0.0)
    for ch in chains:
        ch["G2"] = _dot_nt(ch["ARh"], ch["H"].astype(BF16))
        ch["Vs"] = stack(ch["V"]).astype(BF16)
        ch["rhs"] = ch["G2"][:C] + _dot(ch["Aak"], ch["Vs"])
    for join in joins[1:]:
        for ch in chains:
            E = stack(jnp.where(join, ch["Aab"], 0.0)).astype(BF16)
            ch["TE"] = _dot(ch["T"].astype(BF16), E).astype(BF16)
        for ch in chains:
            ch["T"] = ch["T"] + _dot(ch["TE"], stack(ch["T"]).astype(BF16))
    for ch in chains:
        ch["U"] = _dot(ch["T"].astype(BF16), stack(ch["rhs"]).astype(BF16))
    for ch in chains:
        UVs = jnp.concatenate([stack(ch["U"]).astype(BF16), ch["Vs"]], axis=0)
        y_ref[ch["d"]][:, ch["sl"]] = ch["G2"][C:] + _dot(ch["Ar"], UVs)
    for ch in chains:
        UVt = jnp.concatenate([ch["U"], ch["V"]], axis=0).T.astype(BF16)
        upd = jnp.where(same_head, _dot(UVt, ch["BKo"]), 0.0)
        h_ref[ch["d"], ch["p"]] = ch["H"] * ch["wtot"] + upd


def rw_scan(P, kkn, lw, asig, ka, n_ctx):
    B, S, _ = P.shape
    W = D_MODEL
    C = RW_CHUNK
    nc = S // C
    ncc = n_ctx // C

    def rev(c):
        return jnp.where(c < ncc, ncc - 1 - c, nc - 1 + ncc - c)

    def both(make):
        return [make(lambda c: c), make(rev)]

    pcol = lambda col: both(lambda f: pl.BlockSpec((None, C, W), lambda b, c, f=f: (b, f(c), col)))
    tok = both(lambda f: pl.BlockSpec((None, C, W), lambda b, c, f=f: (b, f(c), 0)))
    dtok = [pl.BlockSpec((None, None, C, W), lambda b, c: (0, b, c, 0)),
            pl.BlockSpec((None, None, C, W), lambda b, c: (1, b, rev(c), 0))]
    c0 = COL_RW // W

    def kern(rf, rb_, kf, kb_, vf, vb_, nf, nb_, lwf, lwb, asf, asb, ka_ref, yf, yb, h_ref):
        _rw_scan_kernel((rf, rb_), (kf, kb_), (vf, vb_), (nf, nb_), (lwf, lwb), (asf, asb), ka_ref,
                        (yf, yb), h_ref)

    return pl.pallas_call(
        kern,
        grid=(B, nc),
        in_specs=[*pcol(c0), *pcol(c0 + 1), *pcol(c0 + 2), *tok, *dtok, *dtok,
                  pl.BlockSpec((1, W), lambda b, c: (0, 0))],
        out_specs=[pl.BlockSpec((None, C, W), lambda b, c: (b, c, 0)),
                   pl.BlockSpec((None, C, W), lambda b, c: (b, rev(c), 0))],
        out_shape=[jax.ShapeDtypeStruct((B, S, W), F32)] * 2,
        scratch_shapes=[pltpu.VMEM((2, W // LANES, LANES, LANES), F32)],
        compiler_params=_cparams(("arbitrary", "arbitrary")),
        name="rw_scan",
    )(P, P, P, P, P, P, kkn, kkn, lw, lw, asig, asig, ka)


def _merge_kernel(x_ref, m_ref, yf_ref, yb_ref, as_ref, r_ref, zk_ref, v_ref, zt_ref, ga_ref, gb_ref, gc_ref,
                  a_ref, b_ref, lnw_ref, lnb_ref, rk_ref, ka_ref, g2_ref, bd_ref,
                  wa_ref, wb_ref, wc_ref, wo_ref, o_ref, *, n_ctx):
    tt = x_ref.shape[0]
    row0 = pl.program_id(1) * tt
    bd = bd_ref[...]
    inv_n = 1.0 / RW_HEAD
    y = yf_ref[...] + yb_ref[...]
    mean = _group_sum(y, bd) * inv_n
    yc = y - mean
    var = _group_sum(yc * yc, bd) * inv_n
    yn = yc * lax.rsqrt(var + GN_EPS) * lnw_ref[...] + lnb_ref[...]
    zk = zk_ref[...].astype(F32)
    ka = ka_ref[...]
    kd_sum = (zk * (1.0 + (as_ref[0].astype(F32) - 1.0) * ka)
              + zk * (1.0 + (as_ref[1].astype(F32) - 1.0) * ka))
    vv = v_ref[...].astype(F32)
    bonus = _group_sum(r_ref[...].astype(F32) * kd_sum * rk_ref[...], bd) * vv
    g = _dot(jax.nn.sigmoid(zt_ref[:, 2 * LANES:].astype(F32)).astype(BF16), g2_ref[...])
    c_br = ((yn + bonus) * g).astype(BF16)
    gate = lambda ref: jax.nn.sigmoid(ref[...].astype(F32))
    m = (gate(ga_ref) * _dot(a_ref[...], wa_ref[...])
         + gate(gb_ref) * _dot(b_ref[...], wb_ref[...])
         + gate(gc_ref) * _dot(c_br, wc_ref[...]))
    out = _dot(m.astype(BF16), wo_ref[...])
    o_ref[...] = x_ref[...] + _gate_rows(m_ref, 2, row0, tt, n_ctx) * out


def merge(xs, mod, y, asig, P, a_br, b_br, lnw, lnb, rk, ka, g2p, bd, wa, wb, wc, wo, n_ctx, tt=256):
    B, S, W = xs.shape
    tok = pl.BlockSpec((None, tt, W), lambda b, i: (b, i, 0))
    pcol = lambda col: pl.BlockSpec((None, tt, W), lambda b, i, col=col: (b, i, col))
    dtok = pl.BlockSpec((2, None, tt, W), lambda b, i: (0, b, i, 0))
    row = pl.BlockSpec((1, W), lambda b, i: (0, 0))
    wsq = pl.BlockSpec((W, W), lambda b, i: (0, 0))
    cr, cg = COL_RW // W, COL_GATE // W
    kern = functools.partial(_merge_kernel, n_ctx=n_ctx)
    return pl.pallas_call(
        kern,
        grid=(B, S // tt),
        in_specs=[tok, pl.BlockSpec((None, 12, W), lambda b, i: (b, 0, 0)), tok, tok, dtok,
                  pcol(cr), pcol(cr + 1), pcol(cr + 2),
                  pl.BlockSpec((None, tt, 512), lambda b, i: (b, i, COL_TAIL // 512)),
                  pcol(cg), pcol(cg + 1), pcol(cg + 2), tok, tok,
                  row, row, row, row,
                  pl.BlockSpec((2 * LANES, W), lambda b, i: (0, 0)),
                  pl.BlockSpec((GROUP_W, GROUP_W), lambda b, i: (0, 0)),
                  wsq, wsq, wsq, wsq],
        out_specs=tok,
        out_shape=jax.ShapeDtypeStruct((B, S, W), F32),
        compiler_params=_cparams(("arbitrary", "arbitrary")),
        name="merge",
    )(xs, mod, y[0], y[1], asig, P, P, P, P, P, P, P, a_br, b_br, lnw, lnb, rk, ka, g2p, bd, wa, wb, wc, wo)


def _ffn_kernel(x_ref, m_ref, g_ref, wi_ref, wo_ref, o_ref, *, n_ctx, fc):
    tm = x_ref.shape[0]
    row0 = pl.program_id(1) * tm
    FF = wo_ref.shape[0]
    x = x_ref[...]
    h = _modulated_norm(x, g_ref[...], m_ref, 3, row0, n_ctx).astype(BF16)

    def gate_up(k):
        return (_dot(h, wi_ref[:, k * fc:(k + 1) * fc]), _dot(h, wi_ref[:, FF + k * fc:FF + (k + 1) * fc]))

    nxt = gate_up(0)
    acc = None
    for k in range(FF // fc):
        gt, up = nxt
        if (k + 1) * fc < FF:
            nxt = gate_up(k + 1)
        act = (gt * jax.nn.sigmoid(gt) * up).astype(BF16)
        part = _dot(act, wo_ref[k * fc:(k + 1) * fc, :])
        acc = part if acc is None else acc + part
    o_ref[...] = x + _gate_rows(m_ref, 5, row0, tm, n_ctx) * acc


def ffn(xs, mod, g, wi, wo, n_ctx, tm=384, fc=256):
    B, S, W = xs.shape
    FF = wo.shape[0]
    kern = functools.partial(_ffn_kernel, n_ctx=n_ctx, fc=fc)
    tok = pl.BlockSpec((None, tm, W), lambda b, i: (b, i, 0))
    return pl.pallas_call(
        kern,
        grid=(B, S // tm),
        in_specs=[tok, pl.BlockSpec((None, 12, W), lambda b, i: (b, 0, 0)),
                  pl.BlockSpec((1, W), lambda b, i: (0, 0)),
                  pl.BlockSpec((W, 2 * FF), lambda b, i: (0, 0)),
                  pl.BlockSpec((FF, W), lambda b, i: (0, 0))],
        out_specs=tok,
        out_shape=jax.ShapeDtypeStruct((B, S, W), F32),
        compiler_params=_cparams(("arbitrary", "arbitrary")),
        name="ffn",
    )(xs, mod, g, wi, wo)


def _rope_tables(n_rows, n_ctx):
    row = jnp.repeat(jnp.arange(n_rows), GRID_W).astype(F32)
    col = jnp.tile(jnp.arange(GRID_W), n_rows).astype(F32)
    axis_dim = DA_HEAD_DIM // 2
    inv_freq = ROPE_BASE ** (-jnp.arange(0, axis_dim, 2, dtype=F32) / axis_dim)
    ang = jnp.concatenate([row[:, None] * inv_freq, col[:, None] * inv_freq], axis=-1)
    ang = jnp.concatenate([ang, ang], axis=-1)
    cos, sin = jnp.cos(ang), jnp.sin(ang)
    cos = jnp.concatenate([jnp.ones((n_ctx, DA_HEAD_DIM), F32), cos], axis=0)
    sin = jnp.concatenate([jnp.zeros((n_ctx, DA_HEAD_DIM), F32), sin], axis=0)
    first = jnp.arange(DA_HEAD_DIM) < DA_HEAD_DIM // 2
    sin_a = jnp.where(first, -sin, 0.0)
    sin_b = jnp.where(first, 0.0, sin)
    dup = lambda t: jnp.concatenate([t, t], axis=-1)
    return dup(cos), dup(sin_a), dup(sin_b)


def kernel(x, c, ctx, c_ctx, ada_w, ada_b, norm1_g, norm2_g, w_in, gm_v_g, gm_ws, gm_bs,
           da_q_g, da_k_g, da_lambda, da_subln_g, rw_mu, rw_w0, rw_w2, rw_a0, rw_a2, rw_g2,
           rw_kk, rw_ka, rw_rk, rw_ln_w, rw_ln_b, w_br_a, w_br_b, w_br_c, w_o, ffn_wi, ffn_wo):
    B, T, D = x.shape
    n_ctx = ctx.shape[1]
    L = ada_w.shape[0]
    cos, sin_a, sin_b = _rope_tables(T // GRID_W, n_ctx)
    xs = jnp.concatenate([ctx, x], axis=1)

    cc = jnp.zeros((16, D), F32).at[:B].set(c).at[B].set(c_ctx)
    mods = ada_mod(cc, ada_w, ada_b)
    mod_l = mods[:, :B].reshape(L, B, 6, 1, D)
    mod_c = jnp.broadcast_to(mods[:, B].reshape(L, 1, 6, 1, D), (L, B, 6, 1, D))
    mod12 = jnp.concatenate([mod_c, mod_l], axis=3).reshape(L, B, 12, D)

    gi = jnp.arange(GROUP_W) // RW_HEAD
    bd = (gi[:, None] == gi[None, :]).astype(BF16)
    row = lambda t: t.reshape(1, -1)
    g0, d0, r0, t0 = 0, 2048, 5120, 5120 + 3072
    k0 = r0 + 3488

    for l in range(L):
        lam_init = 0.8 - 0.6 * math.exp(-0.3 * l)
        wl = w_in[l]
        w_cat = jnp.concatenate([wl[:, g0:d0], wl[:, d0:r0], wl[:, k0:], wl[:, r0:t0],
                                 wl[:, t0:k0], jnp.zeros((D, 96), F32)], axis=1).astype(BF16)
        mu = jnp.concatenate([jnp.zeros((COL_RW,), F32), rw_mu[l], jnp.zeros((96,), F32)]).reshape(1, -1)
        P = proj_in(xs, mod12[l], row(norm1_g[l]), w_cat, mu, n_ctx)

        bs_full = jnp.repeat(gm_bs[l].T, CHUNK, axis=1)
        a_br = gmlp(P, row(gm_v_g[l]), gm_ws[l].astype(BF16), bs_full)

        tile = lambda t, n: jnp.tile(t, n).reshape(1, -1)
        qn, kn, vb = qk_prep(P, cos, sin_a, sin_b, tile(da_q_g[l], 16), tile(da_k_g[l], 16), bd)
        b_br = attention(qn, kn, vb, da_lambda[l], row(da_subln_g[l]), n_ctx, lam_init)

        zpad = jnp.zeros((DECAY_LORA, D), F32)
        w2p = jnp.stack([jnp.concatenate([rw_w2[l, 0], zpad]), jnp.concatenate([zpad, rw_w2[l, 1]])]).astype(BF16)
        a2p = jnp.stack([jnp.concatenate([rw_a2[l, 0], zpad]), jnp.concatenate([zpad, rw_a2[l, 1]])]).astype(BF16)
        kkn, lw, asig = rw_prep(P, rw_w0[l], w2p, rw_a0[l], a2p, row(rw_kk[l]), bd)
        y = rw_scan(P, kkn, lw, asig, row(rw_ka[l]), n_ctx)

        g2p = jnp.concatenate([rw_g2[l], jnp.zeros((2 * LANES - GATE_LORA, D), F32)]).astype(BF16)
        xs = merge(xs, mod12[l], y, asig, P, a_br, b_br, row(rw_ln_w[l]), row(rw_ln_b[l]),
                   row(rw_rk[l]), row(rw_ka[l]), g2p, bd,
                   w_br_a[l].astype(BF16), w_br_b[l].astype(BF16), w_br_c[l].astype(BF16),
                   w_o[l].astype(BF16), n_ctx)
        xs = ffn(xs, mod12[l], row(norm2_g[l]), ffn_wi[l].astype(BF16), ffn_wo[l].astype(BF16), n_ctx)
    return xs[:, n_ctx:]
```

```python
import functools
import math

import jax
import jax.numpy as jnp
from jax import lax
from jax.experimental import pallas as pl
from jax.experimental.pallas import tpu as pltpu

D_MODEL = 1024
GRID_W = 64
CHUNK = 128
GM_GROUPS = 8
DA_HEADS = 8
DA_HEAD_DIM = 64
DA_V_DIM = 128
ROPE_BASE = 10000.0
RW_HEAD = 64
RW_HEADS = 16
DECAY_LORA = 64
AAA_LORA = 64
GATE_LORA = 160
D_FF = 2816
RMS_EPS = 1e-6
GN_EPS = 64e-5

COL_GM = 0
COL_DA = 2048
COL_GATE = 5120
COL_RW = 8192
COL_TAIL = 11264
NP_COLS = 11776
PROJ_TN = 512
RW_CHUNK = 64
ATTN_KB = 256
ATTN_AHEAD = 3
LANES = 128
GROUP_W = 256
VMEM_LIMIT = 56 * 1024 * 1024

BF16 = jnp.bfloat16
F32 = jnp.float32


def _dot(a, b):
    return jnp.dot(a, b, preferred_element_type=F32)


def _dot_nt(a, b):
    return lax.dot_general(a, b, (((1,), (1,)), ((), ())), preferred_element_type=F32)


def _split_dot(x, m):
    hi = x.astype(BF16)
    lo = (x - hi.astype(F32)).astype(BF16)
    return _dot(hi, m) + _dot(lo, m)


def _split3_dot_left(m, x):
    hi = x.astype(BF16)
    r1 = x - hi.astype(F32)
    mid = r1.astype(BF16)
    lo = (r1 - mid.astype(F32)).astype(BF16)
    return _dot(m, hi) + _dot(m, mid) + _dot(m, lo)


def _group_sum(x, bd):
    width = bd.shape[0]
    parts = [_split_dot(x[:, i:i + width], bd) for i in range(0, x.shape[1], width)]
    return parts[0] if len(parts) == 1 else jnp.concatenate(parts, axis=1)


def _cparams(sem):
    return pltpu.CompilerParams(dimension_semantics=sem, vmem_limit_bytes=VMEM_LIMIT)


def _ada_kernel(c_ref, w_ref, b_ref, o_ref):
    c = c_ref[...]
    s = c * jax.nn.sigmoid(c)
    o_ref[...] = jnp.dot(s, w_ref[...], precision=lax.Precision.HIGHEST,
                         preferred_element_type=F32) + b_ref[...]


def ada_mod(cc, ada_w, ada_b):
    L, D, N = ada_w.shape
    tn = 1536
    return pl.pallas_call(
        _ada_kernel,
        grid=(L, N // tn),
        in_specs=[pl.BlockSpec((16, D), lambda l, j: (0, 0)),
                  pl.BlockSpec((None, D, tn), lambda l, j: (l, 0, j)),
                  pl.BlockSpec((None, 1, tn), lambda l, j: (l, 0, j))],
        out_specs=pl.BlockSpec((None, 16, tn), lambda l, j: (l, 0, j)),
        out_shape=jax.ShapeDtypeStruct((L, 16, N), F32),
        compiler_params=_cparams(("arbitrary", "arbitrary")),
        name="ada_mod",
    )(cc, ada_w, ada_b.reshape(L, 1, N))


def _modulated_norm(x, g, m_ref, piece, row0, n_ctx):
    rows = row0 + lax.broadcasted_iota(jnp.int32, (x.shape[0], 1), 0)
    is_lat = rows >= n_ctx
    sh = jnp.where(is_lat, m_ref[2 * piece + 1:2 * piece + 2, :], m_ref[2 * piece:2 * piece + 1, :])
    sc = jnp.where(is_lat, m_ref[2 * piece + 3:2 * piece + 4, :], m_ref[2 * piece + 2:2 * piece + 3, :])
    y = x * lax.rsqrt(jnp.mean(x * x, axis=-1, keepdims=True) + RMS_EPS) * g
    return y * (1.0 + sc) + sh


def _gate_rows(m_ref, piece, row0, nrows, n_ctx):
    rows = row0 + lax.broadcasted_iota(jnp.int32, (nrows, 1), 0)
    return jnp.where(rows >= n_ctx, m_ref[2 * piece + 1:2 * piece + 2, :], m_ref[2 * piece:2 * piece + 1, :])


def _proj_kernel(x_ref, m_ref, g_ref, w_ref, mu_ref, o_ref, h_ref, p_ref, *, n_ctx, rb, mb, rw_j0):
    S = x_ref.shape[0]
    j = pl.program_id(1)
    nblk = S // rb

    @pl.when(j == 0)
    def _():
        def body(i, carry):
            r0 = pl.multiple_of(i * rb, rb)
            x = x_ref[pl.ds(r0, rb), :]
            h_ref[pl.ds(r0, rb), :] = _modulated_norm(x, g_ref[...], m_ref, 0, r0, n_ctx).astype(BF16)
            return carry
        lax.fori_loop(0, nblk, body, 0)

    def project(store):
        store(0, n_ctx, _dot(h_ref[0:n_ctx, :], w_ref[...]))

        def body(i, carry):
            r0 = pl.multiple_of(n_ctx + i * mb, math.gcd(n_ctx, mb))
            store(r0, mb, _dot(h_ref[pl.ds(r0, mb), :], w_ref[...]))
            return carry
        lax.fori_loop(0, (S - n_ctx) // mb, body, 0)

    @pl.when(j < rw_j0)
    def _():
        def store(r0, rows, val):
            o_ref[pl.ds(r0, rows), :] = val.astype(o_ref.dtype)
        project(store)

    @pl.when(j >= rw_j0)
    def _():
        zero = jnp.zeros((8, p_ref.shape[1]), F32)
        p_ref[0:8, :] = zero
        p_ref[n_ctx + 8:n_ctx + 16, :] = zero
        p_ref[S + 16:S + 24, :] = zero

        def store(r0, rows, val):
            off = 8 if isinstance(r0, int) and r0 < n_ctx else 16
            p_ref[pl.ds(r0 + off, rows), :] = val
        project(store)
        c_cur =1.0 - mu_ref[...]
        c_nb = 0.5 * mu_ref[...]
        for i in range(nblk):
            r0 = i * rb
            src = r0 + (16 if r0 >= n_ctx else 8)
            nb = p_ref[src - 1:src - 1 + rb, :] + p_ref[src + 1:src + 1 + rb, :]
            o_ref[r0:r0 + rb, :] = (p_ref[src:src + rb, :] * c_cur + nb * c_nb).astype(o_ref.dtype)


def proj_in(xs, mod, g, w, mu, n_ctx):
    B, S, D = xs.shape
    NP = w.shape[1]
    tn = PROJ_TN
    kern = functools.partial(_proj_kernel, n_ctx=n_ctx, rb=n_ctx, mb=min(1024, S - n_ctx), rw_j0=COL_RW // tn)
    return pl.pallas_call(
        kern,
        grid=(B, NP // tn),
        in_specs=[pl.BlockSpec((None, S, D), lambda b, j: (b, 0, 0)),
                  pl.BlockSpec((None, 12, D), lambda b, j: (b, 0, 0)),
                  pl.BlockSpec((1, D), lambda b, j: (0, 0)),
                  pl.BlockSpec((D, tn), lambda b, j: (0, j)),
                  pl.BlockSpec((1, tn), lambda b, j: (0, j))],
        out_specs=pl.BlockSpec((None, S, tn), lambda b, j: (b, 0, j)),
        out_shape=jax.ShapeDtypeStruct((B, S, NP), BF16),
        scratch_shapes=[pltpu.VMEM((S, D), BF16), pltpu.VMEM((S + 24, tn), F32)],
        compiler_params=_cparams(("arbitrary", "arbitrary")),
        name="proj_in",
    )(xs, mod, g, w, mu)


def _gelu(x):
    return 0.5 * x * (1.0 + lax.erf(x * (1.0 / math.sqrt(2.0))))


def _gmlp_kernel(p_ref, vg_ref, ws_ref, bs_ref, o_ref):
    W = vg_ref.shape[1]
    u = _gelu(p_ref[:, :W].astype(F32))
    v = _gelu(p_ref[:, W:].astype(F32))
    v = v * lax.rsqrt(jnp.mean(v * v, axis=-1, keepdims=True) + RMS_EPS) * vg_ref[...]
    vb = v.astype(BF16)
    gd = W // GM_GROUPS
    for g in range(GM_GROUPS):
        sl = slice(g * gd, (g + 1) * gd)
        f = _dot(ws_ref[g], vb[:, sl]) + bs_ref[:, sl]
        o_ref[:, sl] = (u[:, sl] * f).astype(o_ref.dtype)


def gmlp(P, vg, ws, bs_full):
    B, S, _ = P.shape
    W = vg.shape[1]
    return pl.pallas_call(
        _gmlp_kernel,
        grid=(B, S // CHUNK),
        in_specs=[pl.BlockSpec((None, CHUNK, 2 * W), lambda b, i: (b, i, 0)),
                  pl.BlockSpec((1, W), lambda b, i: (0, 0)),
                  pl.BlockSpec((GM_GROUPS, CHUNK, CHUNK), lambda b, i: (0, 0, 0)),
                  pl.BlockSpec((CHUNK, W), lambda b, i: (0, 0))],
        out_specs=pl.BlockSpec((None, CHUNK, W), lambda b, i: (b, i, 0)),
        out_shape=jax.ShapeDtypeStruct((B, S, W), BF16),
        compiler_params=_cparams(("arbitrary", "arbitrary")),
        name="gmlp",
    )(P, vg, ws, bs_full)


def _qk_kernel(q_ref, k_ref, v_ref, cos_ref, sa_ref, sb_ref, qg_ref, kg_ref, bd_ref,
               qo_ref, ko_ref, vo_ref):
    W = q_ref.shape[1]
    reps = W // LANES
    cos = jnp.tile(cos_ref[...], (1, reps))
    sin_a = jnp.tile(sa_ref[...], (1, reps))
    sin_b = jnp.tile(sb_ref[...], (1, reps))
    bd = bd_ref[...]
    half = DA_HEAD_DIM // 2

    def prep(t, g, scale):
        ms = _group_sum(t * t, bd) * (1.0 / DA_HEAD_DIM)
        t = t * lax.rsqrt(ms + RMS_EPS) * g
        t = t * cos + pltpu.roll(t, W - half, 1) * sin_a + pltpu.roll(t, half, 1) * sin_b
        return t * scale if scale != 1.0 else t

    qo_ref[...] = prep(q_ref[...].astype(F32), qg_ref[...], DA_HEAD_DIM ** -0.5 * math.log2(math.e)).astype(BF16)
    ko_ref[...] = prep(k_ref[...].astype(F32), kg_ref[...], 1.0).astype(BF16)
    for h in range(reps):
        vo_ref[h] = v_ref[:, h * LANES:(h + 1) * LANES].astype(F32).T.astype(BF16)


def qk_prep(P, cos, sin_a, sin_b, qg, kg, bd, tt=256):
    B, S, _ = P.shape
    W = D_MODEL
    c0 = COL_DA // W
    tok = lambda c: pl.BlockSpec((None, tt, W), lambda b, i, c=c: (b, i, c))
    tab = pl.BlockSpec((tt, LANES), lambda b, i: (i, 0))
    row = pl.BlockSpec((1, W), lambda b, i: (0, 0))
    out = pl.BlockSpec((None, tt, W), lambda b, i: (b, i, 0))
    shp = jax.ShapeDtypeStruct((B, S, W), BF16)
    return pl.pallas_call(
        _qk_kernel,
        grid=(B, S // tt),
        in_specs=[tok(c0), tok(c0 + 1), tok(c0 + 2), tab, tab, tab, row, row,
                  pl.BlockSpec((GROUP_W, GROUP_W), lambda b, i: (0, 0))],
        out_specs=[out, out, pl.BlockSpec((None, W // LANES, LANES, tt), lambda b, i: (b, 0, 0, i))],
        out_shape=[shp, shp, jax.ShapeDtypeStruct((B, W // LANES, LANES, S), BF16)],
        compiler_params=_cparams(("arbitrary", "arbitrary")),
        name="qk_prep",
    )(P, P, P, cos, sin_a, sin_b, qg, kg, bd)


def _attn_kernel(q_ref, k_ref, v_ref, lam_ref, sg_ref, o_ref, *, n_ctx, lam_init):
    i = pl.program_id(2)
    tq = q_ref.shape[0]
    lp = lam_ref[...]
    lam = (jnp.exp(jnp.sum(lp[0:1] * lp[1:2], axis=-1, keepdims=True))
           - jnp.exp(jnp.sum(lp[2:3] * lp[3:4], axis=-1, keepdims=True)) + lam_init)

    def attend(nk):
        q = q_ref[...]
        lane = lax.broadcasted_iota(jnp.int32, q.shape, 1)
        zero = jnp.zeros_like(q)
        qs = jnp.concatenate([jnp.where(lane < DA_HEAD_DIM, q, zero),
                              jnp.where(lane >= DA_HEAD_DIM, q, zero)], axis=0)
        m = l = acc = None
        nblk = nk // ATTN_KB
        blk = lambda j: slice(j * ATTN_KB, (j + 1) * ATTN_KB)
        scores = lambda j: _dot_nt(k_ref[blk(j), :], qs)
        queue = [scores(j) for j in range(min(ATTN_AHEAD, nblk))]
        for j in range(nblk):
            ks = blk(j)
            if j + ATTN_AHEAD < nblk:
                queue.append(scores(j + ATTN_AHEAD))
            s = queue.pop(0)
            m_blk = jnp.max(s, axis=0, keepdims=True)
            m_new = m_blk if m is None else jnp.maximum(m, m_blk)
            e = jnp.exp2(s - m_new)
            pv = _dot(v_ref[:, ks], e.astype(BF16))
            if m is None:
                l, acc = jnp.sum(e, axis=0, keepdims=True), pv
            else:
                alpha = jnp.exp2(m - m_new)
                l = alpha * l + jnp.sum(e, axis=0, keepdims=True)
                acc = alpha * acc + pv
            m = m_new
        inv = 1.0 / l
        o = acc[:, :tq] * inv[:, :tq] - acc[:, tq:] * (lam * inv[:, tq:])
        o = o * lax.rsqrt(jnp.mean(o * o, axis=0, keepdims=True) + RMS_EPS)
        o_ref[...] = (o.T * sg_ref[...] * (1.0 - lam_init)).astype(o_ref.dtype)

    attend(k_ref.shape[0])


def attention_ctx(qn, kn, vt, lam_p, sg, n_ctx, lam_init, tq=256):
    B, S, W = qn.shape
    H = W // LANES
    kern = functools.partial(_attn_kernel, n_ctx=n_ctx, lam_init=lam_init)
    return pl.pallas_call(
        kern,
        grid=(B, H, n_ctx // tq),
        in_specs=[pl.BlockSpec((None, tq, LANES), lambda b, h, i: (b, i, h)),
                  pl.BlockSpec((None, n_ctx, LANES), lambda b, h, i: (b, 0, h)),
                  pl.BlockSpec((None, None, LANES, n_ctx), lambda b, h, i: (b, h, 0, 0)),
                  pl.BlockSpec((4, DA_HEAD_DIM), lambda b, h, i: (0, 0)),
                  pl.BlockSpec((1, LANES), lambda b, h, i: (0, 0))],
        out_specs=pl.BlockSpec((None, tq, LANES), lambda b, h, i: (b, i, h)),
        out_shape=jax.ShapeDtypeStruct((B, n_ctx, W), BF16),
        compiler_params=_cparams(("arbitrary", "arbitrary", "arbitrary")),
        name="attention_ctx",
    )(qn, kn, vt, lam_p, sg)


def _attn_lat_kernel(q_ref, k_ref, v_ref, lam_ref, sg_ref, o_ref, s_ref, m_ref, *, lam_init, nq):
    t = pl.program_id(2)
    tq = q_ref.shape[0]
    nblk = k_ref.shape[0] // ATTN_KB
    blk = lambda j: slice(j * ATTN_KB, (j + 1) * ATTN_KB)

    def stacked_q():
        q = q_ref[...]
        lane = lax.broadcasted_iota(jnp.int32, q.shape, 1)
        zero = jnp.zeros_like(q)
        return jnp.concatenate([jnp.where(lane < DA_HEAD_DIM, q, zero),
                                jnp.where(lane >= DA_HEAD_DIM, q, zero)], axis=0)

    def run(scores_slot, softmax_slot):
        qs = stacked_q() if scores_slot is not None else None
        m_run = l = acc = None
        if softmax_slot is not None:
            m_prev = m_ref[softmax_slot]
        for j in range(nblk):
            if scores_slot is not None:
                s = _dot_nt(k_ref[blk(j), :], qs)
                s_ref[scores_slot, blk(j), :] = s
                for r in range(0, ATTN_KB, 8):
                    m_run = s[r:r + 8] if m_run is None else jnp.maximum(m_run, s[r:r + 8])
            if softmax_slot is not None:
                e = jnp.exp2(s_ref[softmax_slot, blk(j), :] - m_prev)
                pv = _dot(v_ref[:, blk(j)], e.astype(BF16))
                lsum = jnp.sum(e, axis=0, keepdims=True)
                l, acc = (lsum, pv) if l is None else (l + lsum, acc + pv)
        if scores_slot is not None:
            m_ref[scores_slot] = jnp.max(m_run, axis=0, keepdims=True)
        if softmax_slot is not None:
            lp = lam_ref[...]
            lam = (jnp.exp(jnp.sum(lp[0:1] * lp[1:2], axis=-1, keepdims=True))
                   - jnp.exp(jnp.sum(lp[2:3] * lp[3:4], axis=-1, keepdims=True)) + lam_init)
            inv = 1.0 / l
            o = acc[:, :tq] * inv[:, :tq] - acc[:, tq:] * (lam * inv[:, tq:])
            o = o * lax.rsqrt(jnp.mean(o * o, axis=0, keepdims=True) + RMS_EPS)
            o_ref[...] = (o.T * sg_ref[...] * (1.0 - lam_init)).astype(o_ref.dtype)

    @pl.when(t == 0)
    def _():
        run(0, None)

    for par in range(2):
        @pl.when((t > 0) & (t < nq) & (t % 2 == par))
        def _():
            run(par, 1 - par)

    @pl.when(t == nq)
    def _():
        run(None, (nq - 1) % 2)


def attention_lat(qn, kn, vt, lam_p, sg, n_ctx, lam_init, tq=256):
    B, S, W = qn.shape
    H = W // LANES
    nq = (S - n_ctx) // tq
    q0 = n_ctx // tq
    kern = functools.partial(_attn_lat_kernel, lam_init=lam_init, nq=nq)
    return pl.pallas_call(
        kern,
        grid=(B, H, nq + 1),
        in_specs=[pl.BlockSpec((None, tq, LANES), lambda b, h, t: (b, q0 + jnp.minimum(t, nq - 1), h)),
                  pl.BlockSpec((None, S, LANES), lambda b, h, t: (b, 0, h)),
                  pl.BlockSpec((None, None, LANES, S), lambda b, h, t: (b, h, 0, 0)),
                  pl.BlockSpec((4, DA_HEAD_DIM), lambda b, h, t: (0, 0)),
                  pl.BlockSpec((1, LANES), lambda b, h, t: (0, 0))],
        out_specs=pl.BlockSpec((None, tq, LANES), lambda b, h, t: (b, jnp.maximum(t - 1, 0), h)),
        out_shape=jax.ShapeDtypeStruct((B, S - n_ctx, W), BF16),
        scratch_shapes=[pltpu.VMEM((2, S, 2 * tq), F32), pltpu.VMEM((2, 1, 2 * tq), F32)],
        compiler_params=_cparams(("arbitrary", "arbitrary", "arbitrary")),
        name="attention_lat",
    )(qn, kn, vt, lam_p, sg)


def _neg_softplus_neg(x):
    y = -x
    return -(jnp.maximum(y, 0.0) + jnp.log1p(jnp.exp(-jnp.abs(y))))


def _rw_prep_kernel(zk_ref, zt_ref, w0_ref, w2_ref, a0_ref, a2_ref, kk_ref, bd_ref,
                    kkn_ref, lw_ref, as_ref):
    tw = jnp.tanh(zt_ref[:, 0:LANES].astype(F32)).astype(BF16)
    za = zt_ref[:, LANES:2 * LANES]
    for d in range(2):
        w_log = _neg_softplus_neg(w0_ref[d:d + 1, :] + _dot(tw, w2_ref[d])) - 0.5
        lw_ref[d] = -jnp.exp(w_log)
        as_ref[d] = jax.nn.sigmoid(a0_ref[d:d + 1, :] + _dot(za, a2_ref[d])).astype(as_ref.dtype)
    kk = zk_ref[...].astype(F32) * kk_ref[...]
    nrm = jnp.sqrt(_group_sum(kk * kk, bd_ref[...]))
    kkn_ref[...] = (kk / jnp.maximum(nrm, 1e-12)).astype(kkn_ref.dtype)


def rw_prep(P, w0, w2p, a0, a2p, kkp, bd, tt=256):
    B, S, _ = P.shape
    W = D_MODEL
    tokw = pl.BlockSpec((None, tt, W), lambda b, i: (b, i, COL_RW // W + 1))
    tail = pl.BlockSpec((None, tt, 512), lambda b, i: (b, i, COL_TAIL // 512))
    full = lambda shape: pl.BlockSpec(shape, lambda b, i: (0,) * len(shape))
    out2 = pl.BlockSpec((2, None, tt, W), lambda b, i: (0, b, i, 0))
    return pl.pallas_call(
        _rw_prep_kernel,
        grid=(B, S // tt),
        in_specs=[tokw, tail, full((2, W)), full((2, LANES, W)), full((2, W)), full((2, LANES, W)),
                  full((1, W)), full((GROUP_W, GROUP_W))],
        out_specs=[pl.BlockSpec((None, tt, W), lambda b, i: (b, i, 0)), out2, out2],
        out_shape=[jax.ShapeDtypeStruct((B, S, W), BF16),
                   jax.ShapeDtypeStruct((2, B, S, W), F32),
                   jax.ShapeDtypeStruct((2, B, S, W), BF16)],
        compiler_params=_cparams(("arbitrary", "arbitrary")),
        name="rw_prep",
    )(P, P, w0, w2p, a0, a2p, kkp, bd)


def _rw_scan_kernel(r_ref, zk_ref, v_ref, kkn_ref, lw_ref, as_ref, ka_ref, y_ref, h_ref):
    c = pl.program_id(1)
    C = RW_CHUNK
    R2 = 2 * C
    n_pairs = ka_ref.shape[1] // LANES

    @pl.when(c == 0)
    def _():
        h_ref[...] = jnp.zeros_like(h_ref)

    ri = lax.broadcasted_iota(jnp.int32, (C, C), 0)
    ci = lax.broadcasted_iota(jnp.int32, (C, C), 1)
    rt = lax.broadcasted_iota(jnp.int32, (C, LANES), 0)
    lane = lax.broadcasted_iota(jnp.int32, (C, LANES), 1)
    cs = lane & (C - 1)
    levels = int(math.log2(C))
    same_blk = [(rt >> k) == (cs >> k) for k in range(levels + 1)]
    joins = [same_blk[k + 1] & ~same_blk[k] for k in range(levels)]
    eye = jnp.where(rt == cs, 1.0, 0.0)
    first = lane < RW_HEAD
    hr = lax.broadcasted_iota(jnp.int32, (LANES, LANES), 0)
    hc = lax.broadcasted_iota(jnp.int32, (LANES, LANES), 1)
    same_head = (hr >= RW_HEAD) == (hc >= RW_HEAD)
    ka = ka_ref[...]

    def stack(x):
        return jnp.concatenate([jnp.where(first, x, 0.0), jnp.where(first, 0.0, x)], axis=0)

    chains = []
    for d in range(2):
        before_c = (ci <= ri) if d == 0 else (ci >= ri)
        strict = (cs < rt) if d == 0 else (cs > rt)
        incl = (cs <= rt) if d == 0 else (cs >= rt)
        lw = lw_ref[d][...]
        L = _split3_dot_left(jnp.where(before_c, 1.0, 0.0).astype(BF16), lw)
        ltot = jnp.sum(lw, axis=0, keepdims=True)
        lmid = 0.5 * ltot
        e_r = jnp.exp(L - lmid)
        e_a = jnp.exp(L - lw - lmid)
        e_b = jnp.exp(lmid - L)
        e_mid = jnp.exp(lmid)
        asig = as_ref[d][...].astype(F32)
        kkn = kkn_ref[d][...].astype(F32)
        kd = zk_ref[d][...].astype(F32) * (1.0 + (asig - 1.0) * ka)
        a_in = -kkn * e_a
        r_in = r_ref[d][...].astype(F32) * e_r
        b_in = kkn * asig * e_b
        k_in = kd * e_b
        a_h = a_in * e_mid
        r_h = r_in * e_mid
        b_o = b_in * e_mid
        k_o = k_in * e_mid
        wtot = jnp.exp(ltot)
        vv = v_ref[d][...].astype(F32)
        for p in range(n_pairs):
            sl = slice(p * LANES, (p + 1) * LANES)
            chains.append(dict(
                d=d, p=p, sl=sl, strict=strict, incl=incl, wtot=wtot[:, sl],
                AR=jnp.concatenate([a_in[:, sl], r_in[:, sl]], axis=0).astype(BF16),
                BK=jnp.concatenate([stack(b_in[:, sl]), stack(k_in[:, sl])], axis=0).astype(BF16),
                ARh=jnp.concatenate([a_h[:, sl], r_h[:, sl]], axis=0).astype(BF16),
                BKo=jnp.concatenate([b_o[:, sl], k_o[:, sl]], axis=0).astype(BF16),
                V=vv[:, sl]))

    for ch in chains:
        G1 = _dot_nt(ch["AR"], ch["BK"])
        ch["Aab"] = jnp.where(ch["strict"], G1[:C, :R2], 0.0)
        ch["Aak"] = jnp.where(ch["strict"], G1[:C, R2:], 0.0).astype(BF16)
        ch["Ar"] = jnp.concatenate([jnp.where(ch["incl"], G1[C:, :R2], 0.0),
                                    jnp.where(ch["incl"], G1[C:, R2:], 0.0)], axis=1).astype(BF16)
        ch["H"] = h_ref[ch["d"], ch["p"]]
        ch["T"] = eye + jnp.where(joins[0], ch["Aab"], 0.0)
    for ch in chains:
        ch["G2"] = _dot_nt(ch["ARh"], ch["H"].astype(BF16))
        ch["Vs"] = stack(ch["V"]).astype(BF16)
        ch["rhs"] = ch["G2"][:C] + _dot(ch["Aak"], ch["Vs"])
    for join in joins[1:]:
        for ch in chains:
            E = stack(jnp.where(join, ch["Aab"], 0.0)).astype(BF16)
            ch["TE"] = _dot(ch["T"].astype(BF16), E).astype(BF16)
        for ch in chains:
            ch["T"] = ch["T"] + _dot(ch["TE"], stack(ch["T"]).astype(BF16))
    for ch in chains:
        ch["U"] = _dot(ch["T"].astype(BF16), stack(ch["rhs"]).astype(BF16))
    for ch in chains:
        UVs = jnp.concatenate([stack(ch["U"]).astype(BF16), ch["Vs"]], axis=0)
        y_ref[ch["d"]][:, ch["sl"]] = ch["G2"][C:] + _dot(ch["Ar"], UVs)
    for ch in chains:
        UVt = jnp.concatenate([ch["U"], ch["V"]], axis=0).T.astype(BF16)
        upd = jnp.where(same_head, _dot(UVt, ch["BKo"]), 0.0)
        h_ref[ch["d"], ch["p"]] = ch["H"] * ch["wtot"] + upd


def rw_scan(P, kkn, lw, asig, ka, n_ctx):
    B, S, _ = P.shape
    W = D_MODEL
    C = RW_CHUNK
    nc = S // C
    ncc = n_ctx // C

    def rev(c):
        return jnp.where(c < ncc, ncc - 1 - c, nc - 1 + ncc - c)

    def both(make):
        return [make(lambda c: c), make(rev)]

    pcol = lambda col: both(lambda f: pl.BlockSpec((None, C, W), lambda b, c, f=f: (b, f(c), col)))
    tok = both(lambda f: pl.BlockSpec((None, C, W), lambda b, c, f=f: (b, f(c), 0)))
    dtok = [pl.BlockSpec((None, None, C, W), lambda b, c: (0, b, c, 0)),
            pl.BlockSpec((None, None, C, W), lambda b, c: (1, b, rev(c), 0))]
    c0 = COL_RW // W

    def kern(rf, rb_, kf, kb_, vf, vb_, nf, nb_, lwf, lwb, asf, asb, ka_ref, yf, yb, h_ref):
        _rw_scan_kernel((rf, rb_), (kf, kb_), (vf, vb_), (nf, nb_), (lwf, lwb), (asf, asb), ka_ref,
                        (yf, yb), h_ref)

    return pl.pallas_call(
        kern,
        grid=(B, nc),
        in_specs=[*pcol(c0), *pcol(c0 + 1), *pcol(c0 + 2), *tok, *dtok, *dtok,
                  pl.BlockSpec((1, W), lambda b, c: (0, 0))],
        out_specs=[pl.BlockSpec((None, C, W), lambda b, c: (b, c, 0)),
                   pl.BlockSpec((None, C, W), lambda b, c: (b, rev(c), 0))],
        out_shape=[jax.ShapeDtypeStruct((B, S, W), F32)] * 2,
        scratch_shapes=[pltpu.VMEM((2, W // LANES, LANES, LANES), F32)],
        compiler_params=_cparams(("arbitrary", "arbitrary")),
        name="rw_scan",
    )(P, P, P, P, P, P, kkn, kkn, lw, lw, asig, asig, ka)


def _merge_kernel(x_ref, m_ref, yf_ref, yb_ref, as_ref, r_ref, zk_ref, v_ref, zt_ref, ga_ref, gb_ref, gc_ref,
                  a_ref, b_ref, lnw_ref, lnb_ref, rk_ref, ka_ref, g2_ref, bd_ref,
                  wa_ref, wb_ref, wc_ref, wo_ref, o_ref, *, n_ctx):
    tt = x_ref.shape[0]
    row0 = pl.program_id(1) * tt
    bd = bd_ref[...]
    inv_n = 1.0 / RW_HEAD
    y = yf_ref[...] + yb_ref[...]
    mean = _group_sum(y, bd) * inv_n
    yc = y - mean
    var = _group_sum(yc * yc, bd) * inv_n
    yn = yc * lax.rsqrt(var + GN_EPS) * lnw_ref[...] + lnb_ref[...]
    zk = zk_ref[...].astype(F32)
    ka = ka_ref[...]
    kd_sum = (zk * (1.0 + (as_ref[0].astype(F32) - 1.0) * ka)
              + zk * (1.0 + (as_ref[1].astype(F32) - 1.0) * ka))
    vv = v_ref[...].astype(F32)
    bonus = _group_sum(r_ref[...].astype(F32) * kd_sum * rk_ref[...], bd) * vv
    g = _dot(jax.nn.sigmoid(zt_ref[:, 2 * LANES:].astype(F32)).astype(BF16), g2_ref[...])
    c_br = ((yn + bonus) * g).astype(BF16)
    gate = lambda ref: jax.nn.sigmoid(ref[...].astype(F32))
    m = (gate(ga_ref) * _dot(a_ref[...], wa_ref[...])
         + gate(gb_ref) * _dot(b_ref[...], wb_ref[...])
         + gate(gc_ref) * _dot(c_br, wc_ref[...]))
    out = _dot(m.astype(BF16), wo_ref[...])
    o_ref[...] = x_ref[...] + _gate_rows(m_ref, 2, row0, tt, n_ctx) * out


def merge(xs, mod, y, asig, P, a_br, b_br, lnw, lnb, rk, ka, g2p, bd, wa, wb, wc, wo, n_ctx, tt=256):
    B, S, W = xs.shape
    tok = pl.BlockSpec((None, tt, W), lambda b, i: (b, i, 0))
    pcol = lambda col: pl.BlockSpec((None, tt, W), lambda b, i, col=col: (b, i, col))
    dtok = pl.BlockSpec((2, None, tt, W), lambda b, i: (0, b, i, 0))
    row = pl.BlockSpec((1, W), lambda b, i: (0, 0))
    wsq = pl.BlockSpec((W, W), lambda b, i: (0, 0))
    cr, cg = COL_RW // W, COL_GATE // W
    kern = functools.partial(_merge_kernel, n_ctx=n_ctx)
    return pl.pallas_call(
        kern,
        grid=(B, S // tt),
        in_specs=[tok, pl.BlockSpec((None, 12, W), lambda b, i: (b, 0, 0)), tok, tok, dtok,
                  pcol(cr), pcol(cr + 1), pcol(cr + 2),
                  pl.BlockSpec((None, tt, 512), lambda b, i: (b, i, COL_TAIL // 512)),
                  pcol(cg), pcol(cg + 1), pcol(cg + 2), tok, tok,
                  row, row, row, row,
                  pl.BlockSpec((2 * LANES, W), lambda b, i: (0, 0)),
                  pl.BlockSpec((GROUP_W, GROUP_W), lambda b, i: (0, 0)),
                  wsq, wsq, wsq, wsq],
        out_specs=tok,
        out_shape=jax.ShapeDtypeStruct((B, S, W), F32),
        compiler_params=_cparams(("arbitrary", "arbitrary")),
        name="merge",
    )(xs, mod, y[0], y[1], asig, P, P, P, P, P, P, P, a_br, b_br, lnw, lnb, rk, ka, g2p, bd, wa, wb, wc, wo)


def _ffn_kernel(x_ref, m_ref, g_ref, wi_ref, wo_ref, o_ref, *, n_ctx, fc):
    tm = x_ref.shape[0]
    row0 = pl.program_id(1) * tm
    FF = wo_ref.shape[0]
    x = x_ref[...]
    h = _modulated_norm(x, g_ref[...], m_ref, 3, row0, n_ctx).astype(BF16)

    def gate_up(k):
        return (_dot(h, wi_ref[:, k * fc:(k + 1) * fc]), _dot(h, wi_ref[:, FF + k * fc:FF + (k + 1) * fc]))

    nxt = gate_up(0)
    acc = None
    for k in range(FF // fc):
        gt, up = nxt
        if (k + 1) * fc < FF:
            nxt = gate_up(k + 1)
        act = (gt * jax.nn.sigmoid(gt) * up).astype(BF16)
        part = _dot(act, wo_ref[k * fc:(k + 1) * fc, :])
        acc = part if acc is None else acc + part
    o_ref[...] = x + _gate_rows(m_ref, 5, row0, tm, n_ctx) * acc


def ffn(xs, mod, g, wi, wo, n_ctx, tm=384, fc=256):
    B, S, W = xs.shape
    FF = wo.shape[0]
    kern = functools.partial(_ffn_kernel, n_ctx=n_ctx, fc=fc)
    tok = pl.BlockSpec((None, tm, W), lambda b, i: (b, i, 0))
    return pl.pallas_call(
        kern,
        grid=(B, S // tm),
        in_specs=[tok, pl.BlockSpec((None, 12, W), lambda b, i: (b, 0, 0)),
                  pl.BlockSpec((1, W), lambda b, i: (0, 0)),
                  pl.BlockSpec((W, 2 * FF), lambda b, i: (0, 0)),
                  pl.BlockSpec((FF, W), lambda b, i: (0, 0))],
        out_specs=tok,
        out_shape=jax.ShapeDtypeStruct((B, S, W), F32),
        compiler_params=_cparams(("arbitrary", "arbitrary")),
        name="ffn",
    )(xs, mod, g, wi, wo)


def _rope_tables(n_rows, n_ctx):
    row = jnp.repeat(jnp.arange(n_rows), GRID_W).astype(F32)
    col = jnp.tile(jnp.arange(GRID_W), n_rows).astype(F32)
    axis_dim = DA_HEAD_DIM // 2
    inv_freq = ROPE_BASE ** (-jnp.arange(0, axis_dim, 2, dtype=F32) / axis_dim)
    ang = jnp.concatenate([row[:, None] * inv_freq, col[:, None] * inv_freq], axis=-1)
    ang = jnp.concatenate([ang, ang], axis=-1)
    cos, sin = jnp.cos(ang), jnp.sin(ang)
    cos = jnp.concatenate([jnp.ones((n_ctx, DA_HEAD_DIM), F32), cos], axis=0)
    sin = jnp.concatenate([jnp.zeros((n_ctx, DA_HEAD_DIM), F32), sin], axis=0)
    first = jnp.arange(DA_HEAD_DIM) < DA_HEAD_DIM // 2
    sin_a = jnp.where(first, -sin, 0.0)
    sin_b = jnp.where(first, 0.0, sin)
    dup = lambda t: jnp.concatenate([t, t], axis=-1)
    return dup(cos), dup(sin_a), dup(sin_b)


def kernel(x, c, ctx, c_ctx, ada_w, ada_b, norm1_g, norm2_g, w_in, gm_v_g, gm_ws, gm_bs,
           da_q_g, da_k_g, da_lambda, da_subln_g, rw_mu, rw_w0, rw_w2, rw_a0, rw_a2, rw_g2,
           rw_kk, rw_ka, rw_rk, rw_ln_w, rw_ln_b, w_br_a, w_br_b, w_br_c, w_o, ffn_wi, ffn_wo):
    B, T, D = x.shape
    n_ctx = ctx.shape[1]
    L = ada_w.shape[0]
    cos, sin_a, sin_b = _rope_tables(T // GRID_W, n_ctx)
    xs = jnp.concatenate([ctx, x], axis=1)

    cc = jnp.zeros((16, D), F32).at[:B].set(c).at[B].set(c_ctx)
    mods = ada_mod(cc, ada_w, ada_b)
    mod_l = mods[:, :B].reshape(L, B, 6, 1, D)
    mod_c = jnp.broadcast_to(mods[:, B].reshape(L, 1, 6, 1, D), (L, B, 6, 1, D))
    mod12 = jnp.concatenate([mod_c, mod_l], axis=3).reshape(L, B, 12, D)

    gi = jnp.arange(GROUP_W) // RW_HEAD
    bd = (gi[:, None] == gi[None, :]).astype(BF16)
    row = lambda t: t.reshape(1, -1)
    g0, d0, r0, t0 = 0, 2048, 5120, 5120 + 3072
    k0 = r0 + 3488

    for l in range(L):
        lam_init = 0.8 - 0.6 * math.exp(-0.3 * l)
        wl = w_in[l]
        w_cat = jnp.concatenate([wl[:, g0:d0], wl[:, d0:r0], wl[:, k0:], wl[:, r0:t0],
                                 wl[:, t0:k0], jnp.zeros((D, 96), F32)], axis=1).astype(BF16)
        mu = jnp.concatenate([jnp.zeros((COL_RW,), F32), rw_mu[l], jnp.zeros((96,), F32)]).reshape(1, -1)
        P = proj_in(xs, mod12[l], row(norm1_g[l]), w_cat, mu, n_ctx)

        bs_full = jnp.repeat(gm_bs[l].T, CHUNK, axis=1)
        a_br = gmlp(P, row(gm_v_g[l]), gm_ws[l].astype(BF16), bs_full)

        tile = lambda t, n: jnp.tile(t, n).reshape(1, -1)
        qn, kn, vb = qk_prep(P, cos, sin_a, sin_b, tile(da_q_g[l], 16), tile(da_k_g[l], 16), bd)
        b_br = jnp.concatenate(
            [attention_ctx(qn, kn, vb, da_lambda[l], row(da_subln_g[l]), n_ctx, lam_init),
             attention_lat(qn, kn, vb, da_lambda[l], row(da_subln_g[l]), n_ctx, lam_init)], axis=1)

        zpad = jnp.zeros((DECAY_LORA, D), F32)
        w2p = jnp.stack([jnp.concatenate([rw_w2[l, 0], zpad]), jnp.concatenate([zpad, rw_w2[l, 1]])]).astype(BF16)
        a2p = jnp.stack([jnp.concatenate([rw_a2[l, 0], zpad]), jnp.concatenate([zpad, rw_a2[l, 1]])]).astype(BF16)
        kkn, lw, asig = rw_prep(P, rw_w0[l], w2p, rw_a0[l], a2p, row(rw_kk[l]), bd)
        y = rw_scan(P, kkn, lw, asig, row(rw_ka[l]), n_ctx)

        g2p = jnp.concatenate([rw_g2[l], jnp.zeros((2 * LANES - GATE_LORA, D), F32)]).astype(BF16)
        xs = merge(xs, mod12[l], y, asig, P, a_br, b_br, row(rw_ln_w[l]), row(rw_ln_b[l]),
                   row(rw_rk[l]), row(rw_ka[l]), g2p, bd,
                   w_br_a[l].astype(BF16), w_br_b[l].astype(BF16), w_br_c[l].astype(BF16),
                   w_o[l].astype(BF16), n_ctx)
        xs = ffn(xs, mod12[l], row(norm2_g[l]), ffn_wi[l].astype(BF16), ffn_wo[l].astype(BF16), n_ctx)
    return xs[:, n_ctx:]
```

```python
import functools
import math

import jax
import jax.numpy as jnp
from jax import lax
from jax.experimental import pallas as pl
from jax.experimental.pallas import tpu as pltpu

D_MODEL = 1024
GRID_W = 64
CHUNK = 128
GM_GROUPS = 8
DA_HEADS = 8
DA_HEAD_DIM = 64
DA_V_DIM = 128
ROPE_BASE = 10000.0
RW_HEAD = 64
RW_HEADS = 16
DECAY_LORA = 64
AAA_LORA = 64
GATE_LORA = 160
D_FF = 2816
RMS_EPS = 1e-6
GN_EPS = 64e-5

COL_GM = 0
COL_DA = 2048
COL_GATE = 5120
COL_RW = 8192
COL_TAIL = 11264
NP_COLS = 11776
PROJ_TN = 512
RW_CHUNK = 64
ATTN_KB = 256
ATTN_AHEAD = 3
LANES = 128
GROUP_W = 256
VMEM_LIMIT = 56 * 1024 * 1024

BF16 = jnp.bfloat16
F32 = jnp.float32


def _dot(a, b):
    return jnp.dot(a, b, preferred_element_type=F32)


def _dot_nt(a, b):
    return lax.dot_general(a, b, (((1,), (1,)), ((), ())), preferred_element_type=F32)


def _split_dot(x, m):
    hi = x.astype(BF16)
    lo = (x - hi.astype(F32)).astype(BF16)
    return _dot(hi, m) + _dot(lo, m)


def _split3_dot_left(m, x):
    hi = x.astype(BF16)
    r1 = x - hi.astype(F32)
    mid = r1.astype(BF16)
    lo = (r1 - mid.astype(F32)).astype(BF16)
    return _dot(m, hi) + _dot(m, mid) + _dot(m, lo)


def _group_sum(x, bd):
    width = bd.shape[0]
    parts = [_split_dot(x[:, i:i + width], bd) for i in range(0, x.shape[1], width)]
    return parts[0] if len(parts) == 1 else jnp.concatenate(parts, axis=1)


def _cparams(sem):
    return pltpu.CompilerParams(dimension_semantics=sem, vmem_limit_bytes=VMEM_LIMIT)


def _ada_kernel(c_ref, w_ref, b_ref, o_ref):
    c = c_ref[...]
    s = c * jax.nn.sigmoid(c)
    o_ref[...] = jnp.dot(s, w_ref[...], precision=lax.Precision.HIGHEST,
                         preferred_element_type=F32) + b_ref[...]


def ada_mod(cc, ada_w, ada_b):
    L, D, N = ada_w.shape
    tn = 1536
    return pl.pallas_call(
        _ada_kernel,
        grid=(L, N // tn),
        in_specs=[pl.BlockSpec((16, D), lambda l, j: (0, 0)),
                  pl.BlockSpec((None, D, tn), lambda l, j: (l, 0, j)),
                  pl.BlockSpec((None, 1, tn), lambda l, j: (l, 0, j))],
        out_specs=pl.BlockSpec((None, 16, tn), lambda l, j: (l, 0, j)),
        out_shape=jax.ShapeDtypeStruct((L, 16, N), F32),
        compiler_params=_cparams(("arbitrary", "arbitrary")),
        name="ada_mod",
    )(cc, ada_w, ada_b.reshape(L, 1, N))


def _modulated_norm(x, g, m_ref, piece, row0, n_ctx):
    rows = row0 + lax.broadcasted_iota(jnp.int32, (x.shape[0], 1), 0)
    is_lat = rows >= n_ctx
    sh = jnp.where(is_lat, m_ref[2 * piece + 1:2 * piece + 2, :], m_ref[2 * piece:2 * piece + 1, :])
    sc = jnp.where(is_lat, m_ref[2 * piece + 3:2 * piece + 4, :], m_ref[2 * piece + 2:2 * piece + 3, :])
    y = x * lax.rsqrt(jnp.mean(x * x, axis=-1, keepdims=True) + RMS_EPS) * g
    return y * (1.0 + sc) + sh


def _gate_rows(m_ref, piece, row0, nrows, n_ctx):
    rows = row0 + lax.broadcasted_iota(jnp.int32, (nrows, 1), 0)
    return jnp.where(rows >= n_ctx, m_ref[2 * piece + 1:2 * piece + 2, :], m_ref[2 * piece:2 * piece + 1, :])


def _proj_kernel(x_ref, m_ref, g_ref, w_ref, mu_ref, o_ref, h_ref, p_ref, *, n_ctx, rb, mb, rw_j0):
    S = x_ref.shape[0]
    j = pl.program_id(1)
    nblk = S // rb

    @pl.when(j == 0)
    def _():
        def body(i, carry):
            r0 = pl.multiple_of(i * rb, rb)
            x = x_ref[pl.ds(r0, rb), :]
            h_ref[pl.ds(r0, rb), :] = _modulated_norm(x, g_ref[...], m_ref, 0, r0, n_ctx).astype(BF16)
            return carry
        lax.fori_loop(0, nblk, body, 0)

    def project(store):
        store(0, n_ctx, _dot(h_ref[0:n_ctx, :], w_ref[...]))

        def body(i, carry):
            r0 = pl.multiple_of(n_ctx + i * mb, math.gcd(n_ctx, mb))
            store(r0, mb, _dot(h_ref[pl.ds(r0, mb), :], w_ref[...]))
            return carry
        lax.fori_loop(0, (S - n_ctx) // mb, body, 0)

    @pl.when(j < rw_j0)
    def _():
        def store(r0, rows, val):
            o_ref[pl.ds(r0, rows), :] = val.astype(o_ref.dtype)
        project(store)

    @pl.when(j >= rw_j0)
    def _():
        zero = jnp.zeros((8, p_ref.shape[1]), F32)
        p_ref[0:8, :] = zero
        p_ref[n_ctx + 8:n_ctx + 16, :] = zero
        p_ref[S + 16:S + 24, :] = zero

        def store(r0, rows, val):
            off = 8 if isinstance(r0, int) and r0 < n_ctx else 16
            p_ref[pl.ds(r0 + off, rows), :] = val
        project(store)
        c_cur =1.0 - mu_ref[...]
        c_nb = 0.5 * mu_ref[...]
        for i in range(nblk):
            r0 = i * rb
            src = r0 + (16 if r0 >= n_ctx else 8)
            nb = p_ref[src - 1:src - 1 + rb, :] + p_ref[src + 1:src + 1 + rb, :]
            o_ref[r0:r0 + rb, :] = (p_ref[src:src + rb, :] * c_cur + nb * c_nb).astype(o_ref.dtype)


def proj_in(xs, mod, g, w, mu, n_ctx):
    B, S, D = xs.shape
    NP = w.shape[1]
    tn = PROJ_TN
    kern = functools.partial(_proj_kernel, n_ctx=n_ctx, rb=n_ctx, mb=min(1024, S - n_ctx), rw_j0=COL_RW // tn)
    return pl.pallas_call(
        kern,
        grid=(B, NP // tn),
        in_specs=[pl.BlockSpec((None, S, D), lambda b, j: (b, 0, 0)),
                  pl.BlockSpec((None, 12, D), lambda b, j: (b, 0, 0)),
                  pl.BlockSpec((1, D), lambda b, j: (0, 0)),
                  pl.BlockSpec((D, tn), lambda b, j: (0, j)),
                  pl.BlockSpec((1, tn), lambda b, j: (0, j))],
        out_specs=pl.BlockSpec((None, S, tn), lambda b, j: (b, 0, j)),
        out_shape=jax.ShapeDtypeStruct((B, S, NP), BF16),
        scratch_shapes=[pltpu.VMEM((S, D), BF16), pltpu.VMEM((S + 24, tn), F32)],
        compiler_params=_cparams(("arbitrary", "arbitrary")),
        name="proj_in",
    )(xs, mod, g, w, mu)


def _gelu(x):
    return 0.5 * x * (1.0 + lax.erf(x * (1.0 / math.sqrt(2.0))))


def _gmlp_kernel(p_ref, vg_ref, ws_ref, bs_ref, o_ref):
    W = vg_ref.shape[1]
    gd = W // GM_GROUPS
    for c in range(p_ref.shape[0] // CHUNK):
        rows = slice(c * CHUNK, (c + 1) * CHUNK)
        u = _gelu(p_ref[rows, :W].astype(F32))
        v = _gelu(p_ref[rows, W:].astype(F32))
        v = v * lax.rsqrt(jnp.mean(v * v, axis=-1, keepdims=True) + RMS_EPS) * vg_ref[...]
        vb = v.astype(BF16)
        for g in range(GM_GROUPS):
            sl = slice(g * gd, (g + 1) * gd)
            f = _dot(ws_ref[g], vb[:, sl]) + bs_ref[:, sl]
            o_ref[rows, sl] = (u[:, sl] * f).astype(o_ref.dtype)


def gmlp(P, vg, ws, bs_full, tt=256):
    B, S, _ = P.shape
    W = vg.shape[1]
    return pl.pallas_call(
        _gmlp_kernel,
        grid=(B, S // tt),
        in_specs=[pl.BlockSpec((None, tt, 2 * W), lambda b, i: (b, i, 0)),
                  pl.BlockSpec((1, W), lambda b, i: (0, 0)),
                  pl.BlockSpec((GM_GROUPS, CHUNK, CHUNK), lambda b, i: (0, 0, 0)),
                  pl.BlockSpec((CHUNK, W), lambda b, i: (0, 0))],
        out_specs=pl.BlockSpec((None, tt, W), lambda b, i: (b, i, 0)),
        out_shape=jax.ShapeDtypeStruct((B, S, W), BF16),
        compiler_params=_cparams(("arbitrary", "arbitrary")),
        name="gmlp",
    )(P, vg, ws, bs_full)


def _qk_kernel(q_ref, k_ref, v_ref, cos_ref, sa_ref, sb_ref, qg_ref, kg_ref, bd_ref,
               qo_ref, ko_ref, vo_ref):
    W = q_ref.shape[1]
    reps = W // LANES
    cos = jnp.tile(cos_ref[...], (1, reps))
    sin_a = jnp.tile(sa_ref[...], (1, reps))
    sin_b = jnp.tile(sb_ref[...], (1, reps))
    bd = bd_ref[...]
    half = DA_HEAD_DIM // 2

    def prep(t, g, scale):
        ms = _group_sum(t * t, bd) * (1.0 / DA_HEAD_DIM)
        t = t * lax.rsqrt(ms + RMS_EPS) * g
        t = t * cos + pltpu.roll(t, W - half, 1) * sin_a + pltpu.roll(t, half, 1) * sin_b
        return t * scale if scale != 1.0 else t

    qo_ref[...] = prep(q_ref[...].astype(F32), qg_ref[...], DA_HEAD_DIM ** -0.5 * math.log2(math.e)).astype(BF16)
    ko_ref[...] = prep(k_ref[...].astype(F32), kg_ref[...], 1.0).astype(BF16)
    for h in range(reps):
        vo_ref[h] = v_ref[:, h * LANES:(h + 1) * LANES].astype(F32).T.astype(BF16)


def qk_prep(P, cos, sin_a, sin_b, qg, kg, bd, tt=256):
    B, S, _ = P.shape
    W = D_MODEL
    c0 = COL_DA // W
    tok = lambda c: pl.BlockSpec((None, tt, W), lambda b, i, c=c: (b, i, c))
    tab = pl.BlockSpec((tt, LANES), lambda b, i: (i, 0))
    row = pl.BlockSpec((1, W), lambda b, i: (0, 0))
    out = pl.BlockSpec((None, tt, W), lambda b, i: (b, i, 0))
    shp = jax.ShapeDtypeStruct((B, S, W), BF16)
    return pl.pallas_call(
        _qk_kernel,
        grid=(B, S // tt),
        in_specs=[tok(c0), tok(c0 + 1), tok(c0 + 2), tab, tab, tab, row, row,
                  pl.BlockSpec((GROUP_W, GROUP_W), lambda b, i: (0, 0))],
        out_specs=[out, out, pl.BlockSpec((None, W // LANES, LANES, tt), lambda b, i: (b, 0, 0, i))],
        out_shape=[shp, shp, jax.ShapeDtypeStruct((B, W // LANES, LANES, S), BF16)],
        compiler_params=_cparams(("arbitrary", "arbitrary")),
        name="qk_prep",
    )(P, P, P, cos, sin_a, sin_b, qg, kg, bd)


def _attn_kernel(q_ref, k_ref, v_ref, lam_ref, sg_ref, o_ref, *, n_ctx, lam_init):
    i = pl.program_id(2)
    tq = q_ref.shape[0]
    lp = lam_ref[...]
    lam = (jnp.exp(jnp.sum(lp[0:1] * lp[1:2], axis=-1, keepdims=True))
           - jnp.exp(jnp.sum(lp[2:3] * lp[3:4], axis=-1, keepdims=True)) + lam_init)

    def attend(nk):
        q = q_ref[...]
        lane = lax.broadcasted_iota(jnp.int32, q.shape, 1)
        zero = jnp.zeros_like(q)
        qs = jnp.concatenate([jnp.where(lane < DA_HEAD_DIM, q, zero),
                              jnp.where(lane >= DA_HEAD_DIM, q, zero)], axis=0)
        m = l = acc = None
        nblk = nk // ATTN_KB
        blk = lambda j: slice(j * ATTN_KB, (j + 1) * ATTN_KB)
        scores = lambda j: _dot_nt(k_ref[blk(j), :], qs)
        queue = [scores(j) for j in range(min(ATTN_AHEAD, nblk))]
        for j in range(nblk):
            ks = blk(j)
            if j + ATTN_AHEAD < nblk:
                queue.append(scores(j + ATTN_AHEAD))
            s = queue.pop(0)
            m_blk = jnp.max(s, axis=0, keepdims=True)
            m_new = m_blk if m is None else jnp.maximum(m, m_blk)
            e = jnp.exp2(s - m_new)
            pv = _dot(v_ref[:, ks], e.astype(BF16))
            if m is None:
                l, acc = jnp.sum(e, axis=0, keepdims=True), pv
            else:
                alpha = jnp.exp2(m - m_new)
                l = alpha * l + jnp.sum(e, axis=0, keepdims=True)
                acc = alpha * acc + pv
            m = m_new
        inv = 1.0 / l
        o = acc[:, :tq] * inv[:, :tq] - acc[:, tq:] * (lam * inv[:, tq:])
        o = o * lax.rsqrt(jnp.mean(o * o, axis=0, keepdims=True) + RMS_EPS)
        o_ref[...] = (o.T * sg_ref[...] * (1.0 - lam_init)).astype(o_ref.dtype)

    attend(k_ref.shape[0])


def attention_ctx(qn, kn, vt, lam_p, sg, n_ctx, lam_init, tq=256):
    B, S, W = qn.shape
    H = W // LANES
    kern = functools.partial(_attn_kernel, n_ctx=n_ctx, lam_init=lam_init)
    return pl.pallas_call(
        kern,
        grid=(B, H, n_ctx // tq),
        in_specs=[pl.BlockSpec((None, tq, LANES), lambda b, h, i: (b, i, h)),
                  pl.BlockSpec((None, n_ctx, LANES), lambda b, h, i: (b, 0, h)),
                  pl.BlockSpec((None, None, LANES, n_ctx), lambda b, h, i: (b, h, 0, 0)),
                  pl.BlockSpec((4, DA_HEAD_DIM), lambda b, h, i: (0, 0)),
                  pl.BlockSpec((1, LANES), lambda b, h, i: (0, 0))],
        out_specs=pl.BlockSpec((None, tq, LANES), lambda b, h, i: (b, i, h)),
        out_shape=jax.ShapeDtypeStruct((B, n_ctx, W), BF16),
        compiler_params=_cparams(("arbitrary", "arbitrary", "arbitrary")),
        name="attention_ctx",
    )(qn, kn, vt, lam_p, sg)


def _attn_lat_kernel(q_ref, k_ref, v_ref, lam_ref, sg_ref, o_ref, s_ref, m_ref, *, lam_init, nq):
    t = pl.program_id(2)
    tq = q_ref.shape[0]
    nblk = k_ref.shape[0] // ATTN_KB
    blk = lambda j: slice(j * ATTN_KB, (j + 1) * ATTN_KB)

    def stacked_q():
        q = q_ref[...]
        lane = lax.broadcasted_iota(jnp.int32, q.shape, 1)
        zero = jnp.zeros_like(q)
        return jnp.concatenate([jnp.where(lane < DA_HEAD_DIM, q, zero),
                                jnp.where(lane >= DA_HEAD_DIM, q, zero)], axis=0)

    def run(scores_slot, softmax_slot):
        qs = stacked_q() if scores_slot is not None else None
        m_run = l = acc = None
        if softmax_slot is not None:
            m_prev = m_ref[softmax_slot]
        for j in range(nblk):
            if scores_slot is not None:
                s = _dot_nt(k_ref[blk(j), :], qs)
                s_ref[scores_slot, blk(j), :] = s
                for r in range(0, ATTN_KB, 8):
                    m_run = s[r:r + 8] if m_run is None else jnp.maximum(m_run, s[r:r + 8])
            if softmax_slot is not None:
                e = jnp.exp2(s_ref[softmax_slot, blk(j), :] - m_prev)
                pv = _dot(v_ref[:, blk(j)], e.astype(BF16))
                lsum = jnp.sum(e, axis=0, keepdims=True)
                l, acc = (lsum, pv) if l is None else (l + lsum, acc + pv)
        if scores_slot is not None:
            m_ref[scores_slot] = jnp.max(m_run, axis=0, keepdims=True)
        if softmax_slot is not None:
            lp = lam_ref[...]
            lam = (jnp.exp(jnp.sum(lp[0:1] * lp[1:2], axis=-1, keepdims=True))
                   - jnp.exp(jnp.sum(lp[2:3] * lp[3:4], axis=-1, keepdims=True)) + lam_init)
            inv = 1.0 / l
            o = acc[:, :tq] * inv[:, :tq] - acc[:, tq:] * (lam * inv[:, tq:])
            o = o * lax.rsqrt(jnp.mean(o * o, axis=0, keepdims=True) + RMS_EPS)
            o_ref[...] = (o.T * sg_ref[...] * (1.0 - lam_init)).astype(o_ref.dtype)

    @pl.when(t == 0)
    def _():
        run(0, None)

    for par in range(2):
        @pl.when((t > 0) & (t < nq) & (t % 2 == par))
        def _():
            run(par, 1 - par)

    @pl.when(t == nq)
    def _():
        run(None, (nq - 1) % 2)


def attention_lat(qn, kn, vt, lam_p, sg, n_ctx, lam_init, tq=512):
    B, S, W = qn.shape
    H = W // LANES
    nq = (S - n_ctx) // tq
    q_lat = qn[:, n_ctx:]
    kern = functools.partial(_attn_lat_kernel, lam_init=lam_init, nq=nq)
    return pl.pallas_call(
        kern,
        grid=(B, H, nq + 1),
        in_specs=[pl.BlockSpec((None, tq, LANES), lambda b, h, t: (b, jnp.minimum(t, nq - 1), h)),
                  pl.BlockSpec((None, S, LANES), lambda b, h, t: (b, 0, h)),
                  pl.BlockSpec((None, None, LANES, S), lambda b, h, t: (b, h, 0, 0)),
                  pl.BlockSpec((4, DA_HEAD_DIM), lambda b, h, t: (0, 0)),
                  pl.BlockSpec((1, LANES), lambda b, h, t: (0, 0))],
        out_specs=pl.BlockSpec((None, tq, LANES), lambda b, h, t: (b, jnp.maximum(t - 1, 0), h)),
        out_shape=jax.ShapeDtypeStruct((B, S - n_ctx, W), BF16),
        scratch_shapes=[pltpu.VMEM((2, S, 2 * tq), F32), pltpu.VMEM((2, 1, 2 * tq), F32)],
        compiler_params=_cparams(("arbitrary", "arbitrary", "arbitrary")),
        name="attention_lat",
    )(q_lat, kn, vt, lam_p, sg)


def _neg_softplus_neg(x):
    y = -x
    return -(jnp.maximum(y, 0.0) + jnp.log1p(jnp.exp(-jnp.abs(y))))


def _rw_prep_kernel(zk_ref, zt_ref, w0_ref, w2_ref, a0_ref, a2_ref, kk_ref, bd_ref,
                    kkn_ref, lw_ref, as_ref):
    tw = jnp.tanh(zt_ref[:, 0:LANES].astype(F32)).astype(BF16)
    za = zt_ref[:, LANES:2 * LANES]
    for d in range(2):
        w_log = _neg_softplus_neg(w0_ref[d:d + 1, :] + _dot(tw, w2_ref[d])) - 0.5
        lw_ref[d] = -jnp.exp(w_log)
        as_ref[d] = jax.nn.sigmoid(a0_ref[d:d + 1, :] + _dot(za, a2_ref[d])).astype(as_ref.dtype)
    kk = zk_ref[...].astype(F32) * kk_ref[...]
    nrm = jnp.sqrt(_group_sum(kk * kk, bd_ref[...]))
    kkn_ref[...] = (kk / jnp.maximum(nrm, 1e-12)).astype(kkn_ref.dtype)


def rw_prep(P, w0, w2p, a0, a2p, kkp, bd, tt=256):
    B, S, _ = P.shape
    W = D_MODEL
    tokw = pl.BlockSpec((None, tt, W), lambda b, i: (b, i, COL_RW // W + 1))
    tail = pl.BlockSpec((None, tt, 512), lambda b, i: (b, i, COL_TAIL // 512))
    full = lambda shape: pl.BlockSpec(shape, lambda b, i: (0,) * len(shape))
    out2 = pl.BlockSpec((2, None, tt, W), lambda b, i: (0, b, i, 0))
    return pl.pallas_call(
        _rw_prep_kernel,
        grid=(B, S // tt),
        in_specs=[tokw, tail, full((2, W)), full((2, LANES, W)), full((2, W)), full((2, LANES, W)),
                  full((1, W)), full((GROUP_W, GROUP_W))],
        out_specs=[pl.BlockSpec((None, tt, W), lambda b, i: (b, i, 0)), out2, out2],
        out_shape=[jax.ShapeDtypeStruct((B, S, W), BF16),
                   jax.ShapeDtypeStruct((2, B, S, W), F32),
                   jax.ShapeDtypeStruct((2, B, S, W), BF16)],
        compiler_params=_cparams(("arbitrary", "arbitrary")),
        name="rw_prep",
    )(P, P, w0, w2p, a0, a2p, kkp, bd)


def _rw_scan_kernel(r_ref, zk_ref, v_ref, kkn_ref, lw_ref, as_ref, ka_ref, y_ref, h_ref):
    c = pl.program_id(1)
    C = RW_CHUNK
    R2 = 2 * C
    n_pairs = ka_ref.shape[1] // LANES

    @pl.when(c == 0)
    def _():
        h_ref[...] = jnp.zeros_like(h_ref)

    ri = lax.broadcasted_iota(jnp.int32, (C, C), 0)
    ci = lax.broadcasted_iota(jnp.int32, (C, C), 1)
    rt = lax.broadcasted_iota(jnp.int32, (C, LANES), 0)
    lane = lax.broadcasted_iota(jnp.int32, (C, LANES), 1)
    cs = lane & (C - 1)
    levels = int(math.log2(C))
    same_blk = [(rt >> k) == (cs >> k) for k in range(levels + 1)]
    joins = [same_blk[k + 1] & ~same_blk[k] for k in range(levels)]
    eye = jnp.where(rt == cs, 1.0, 0.0)
    first = lane < RW_HEAD
    hr = lax.broadcasted_iota(jnp.int32, (LANES, LANES), 0)
    hc = lax.broadcasted_iota(jnp.int32, (LANES, LANES), 1)
    same_head = (hr >= RW_HEAD) == (hc >= RW_HEAD)
    ka = ka_ref[...]

    def stack(x):
        return jnp.concatenate([jnp.where(first, x, 0.0), jnp.where(first, 0.0, x)], axis=0)

    chains = []
    for d in range(2):
        before_c = (ci <= ri) if d == 0 else (ci >= ri)
        strict = (cs < rt) if d == 0 else (cs > rt)
        incl = (cs <= rt) if d == 0 else (cs >= rt)
        lw = lw_ref[d][...]
        L = _split3_dot_left(jnp.where(before_c, 1.0, 0.0).astype(BF16), lw)
        ltot = jnp.sum(lw, axis=0, keepdims=True)
        lmid = 0.5 * ltot
        e_r = jnp.exp(L - lmid)
        e_a = jnp.exp(L - lw - lmid)
        e_b = jnp.exp(lmid - L)
        e_mid = jnp.exp(lmid)
        asig = as_ref[d][...].astype(F32)
        kkn = kkn_ref[d][...].astype(F32)
        kd = zk_ref[d][...].astype(F32) * (1.0 + (asig - 1.0) * ka)
        a_in = -kkn * e_a
        r_in = r_ref[d][...].astype(F32) * e_r
        b_in = kkn * asig * e_b
        k_in = kd * e_b
        a_h = a_in * e_mid
        r_h = r_in * e_mid
        b_o = b_in * e_mid
        k_o = k_in * e_mid
        wtot = jnp.exp(ltot)
        vv = v_ref[d][...].astype(F32)
        for p in range(n_pairs):
            sl = slice(p * LANES, (p + 1) * LANES)
            chains.append(dict(
                d=d, p=p, sl=sl, strict=strict, incl=incl, wtot=wtot[:, sl],
                AR=jnp.concatenate([a_in[:, sl], r_in[:, sl]], axis=0).astype(BF16),
                BK=jnp.concatenate([stack(b_in[:, sl]), stack(k_in[:, sl])], axis=0).astype(BF16),
                ARh=jnp.concatenate([a_h[:, sl], r_h[:, sl]], axis=0).astype(BF16),
                BKo=jnp.concatenate([b_o[:, sl], k_o[:, sl]], axis=0).astype(BF16),
                V=vv[:, sl]))

    for ch in chains:
        G1 = _dot_nt(ch["AR"], ch["BK"])
        ch["Aab"] = jnp.where(ch["strict"], G1[:C, :R2], 0.0)
        ch["Aak"] = jnp.where(ch["strict"], G1[:C, R2:], 0.0).astype(BF16)
        ch["Ar"] = jnp.concatenate([jnp.where(ch["incl"], G1[C:, :R2], 0.0),
                                    jnp.where(ch["incl"], G1[C:, R2:], 0.0)], axis=1).astype(BF16)
        ch["H"] = h_ref[ch["d"], ch["p"]]
        ch["T"] = eye + jnp.where(joins[0], ch["Aab"], 0.0)
    for ch in chains:
        ch["G2"] = _dot_nt(ch["ARh"], ch["H"].astype(BF16))
        ch["Vs"] = stack(ch["V"]).astype(BF16)
        ch["rhs"] = ch["G2"][:C] + _dot(ch["Aak"], ch["Vs"])
    for join in joins[1:]:
        for ch in chains:
            E = stack(jnp.where(join, ch["Aab"], 0.0)).astype(BF16)
            ch["TE"] = _dot(ch["T"].astype(BF16), E).astype(BF16)
        for ch in chains:
            ch["T"] = ch["T"] + _dot(ch["TE"], stack(ch["T"]).astype(BF16))
    for ch in chains:
        ch["U"] = _dot(ch["T"].astype(BF16), stack(ch["rhs"]).astype(BF16))
    for ch in chains:
        UVs = jnp.concatenate([stack(ch["U"]).astype(BF16), ch["Vs"]], axis=0)
        y_ref[ch["d"]][:, ch["sl"]] = ch["G2"][C:] + _dot(ch["Ar"], UVs)
    for ch in chains:
        UVt = jnp.concatenate([ch["U"], ch["V"]], axis=0).T.astype(BF16)
        upd = jnp.where(same_head, _dot(UVt, ch["BKo"]), 0.0)
        h_ref[ch["d"], ch["p"]] = ch["H"] * ch["wtot"] + upd


def rw_scan(P, kkn, lw, asig, ka, n_ctx):
    B, S, _ = P.shape
    W = D_MODEL
    C = RW_CHUNK
    nc = S // C
    ncc = n_ctx // C

    def rev(c):
        return jnp.where(c < ncc, ncc - 1 - c, nc - 1 + ncc - c)

    def both(make):
        return [make(lambda c: c), make(rev)]

    pcol = lambda col: both(lambda f: pl.BlockSpec((None, C, W), lambda b, c, f=f: (b, f(c), col)))
    tok = both(lambda f: pl.BlockSpec((None, C, W), lambda b, c, f=f: (b, f(c), 0)))
    dtok = [pl.BlockSpec((None, None, C, W), lambda b, c: (0, b, c, 0)),
            pl.BlockSpec((None, None, C, W), lambda b, c: (1, b, rev(c), 0))]
    c0 = COL_RW // W

    def kern(rf, rb_, kf, kb_, vf, vb_, nf, nb_, lwf, lwb, asf, asb, ka_ref, yf, yb, h_ref):
        _rw_scan_kernel((rf, rb_), (kf, kb_), (vf, vb_), (nf, nb_), (lwf, lwb), (asf, asb), ka_ref,
                        (yf, yb), h_ref)

    return pl.pallas_call(
        kern,
        grid=(B, nc),
        in_specs=[*pcol(c0), *pcol(c0 + 1), *pcol(c0 + 2), *tok, *dtok, *dtok,
                  pl.BlockSpec((1, W), lambda b, c: (0, 0))],
        out_specs=[pl.BlockSpec((None, C, W), lambda b, c: (b, c, 0)),
                   pl.BlockSpec((None, C, W), lambda b, c: (b, rev(c), 0))],
        out_shape=[jax.ShapeDtypeStruct((B, S, W), F32)] * 2,
        scratch_shapes=[pltpu.VMEM((2, W // LANES, LANES, LANES), F32)],
        compiler_params=_cparams(("arbitrary", "arbitrary")),
        name="rw_scan",
    )(P, P, P, P, P, P, kkn, kkn, lw, lw, asig, asig, ka)


def _merge_kernel(x_ref, m_ref, yf_ref, yb_ref, as_ref, r_ref, zk_ref, v_ref, zt_ref, ga_ref, gb_ref, gc_ref,
                  a_ref, b_ref, lnw_ref, lnb_ref, rk_ref, ka_ref, g2_ref, bd_ref,
                  wa_ref, wb_ref, wc_ref, wo_ref, o_ref, *, n_ctx):
    tt = x_ref.shape[0]
    row0 = pl.program_id(1) * tt
    bd = bd_ref[...]
    inv_n = 1.0 / RW_HEAD
    y = yf_ref[...] + yb_ref[...]
    mean = _group_sum(y, bd) * inv_n
    yc = y - mean
    var = _group_sum(yc * yc, bd) * inv_n
    yn = yc * lax.rsqrt(var + GN_EPS) * lnw_ref[...] + lnb_ref[...]
    zk = zk_ref[...].astype(F32)
    ka = ka_ref[...]
    kd_sum = (zk * (1.0 + (as_ref[0].astype(F32) - 1.0) * ka)
              + zk * (1.0 + (as_ref[1].astype(F32) - 1.0) * ka))
    vv = v_ref[...].astype(F32)
    bonus = _group_sum(r_ref[...].astype(F32) * kd_sum * rk_ref[...], bd) * vv
    g = _dot(jax.nn.sigmoid(zt_ref[:, 2 * LANES:].astype(F32)).astype(BF16), g2_ref[...])
    c_br = ((yn + bonus) * g).astype(BF16)
    gate = lambda ref: jax.nn.sigmoid(ref[...].astype(F32))
    m = (gate(ga_ref) * _dot(a_ref[...], wa_ref[...])
         + gate(gb_ref) * _dot(b_ref[...], wb_ref[...])
         + gate(gc_ref) * _dot(c_br, wc_ref[...]))
    out = _dot(m.astype(BF16), wo_ref[...])
    o_ref[...] = x_ref[...] + _gate_rows(m_ref, 2, row0, tt, n_ctx) * out


def merge(xs, mod, y, asig, P, a_br, b_br, lnw, lnb, rk, ka, g2p, bd, wa, wb, wc, wo, n_ctx, tt=256):
    B, S, W = xs.shape
    tok = pl.BlockSpec((None, tt, W), lambda b, i: (b, i, 0))
    pcol = lambda col: pl.BlockSpec((None, tt, W), lambda b, i, col=col: (b, i, col))
    dtok = pl.BlockSpec((2, None, tt, W), lambda b, i: (0, b, i, 0))
    row = pl.BlockSpec((1, W), lambda b, i: (0, 0))
    wsq = pl.BlockSpec((W, W), lambda b, i: (0, 0))
    cr, cg = COL_RW // W, COL_GATE // W
    kern = functools.partial(_merge_kernel, n_ctx=n_ctx)
    return pl.pallas_call(
        kern,
        grid=(B, S // tt),
        in_specs=[tok, pl.BlockSpec((None, 12, W), lambda b, i: (b, 0, 0)), tok, tok, dtok,
                  pcol(cr), pcol(cr + 1), pcol(cr + 2),
                  pl.BlockSpec((None, tt, 512), lambda b, i: (b, i, COL_TAIL // 512)),
                  pcol(cg), pcol(cg + 1), pcol(cg + 2), tok, tok,
                  row, row, row, row,
                  pl.BlockSpec((2 * LANES, W), lambda b, i: (0, 0)),
                  pl.BlockSpec((GROUP_W, GROUP_W), lambda b, i: (0, 0)),
                  wsq, wsq, wsq, wsq],
        out_specs=tok,
        out_shape=jax.ShapeDtypeStruct((B, S, W), F32),
        compiler_params=_cparams(("arbitrary", "arbitrary")),
        name="merge",
    )(xs, mod, y[0], y[1], asig, P, P, P, P, P, P, P, a_br, b_br, lnw, lnb, rk, ka, g2p, bd, wa, wb, wc, wo)


def _ffn_kernel(x_ref, m_ref, g_ref, wi_ref, wo_ref, o_ref, *, n_ctx, fc):
    tm = x_ref.shape[0]
    row0 = pl.program_id(1) * tm
    FF = wo_ref.shape[0]
    x = x_ref[...]
    h = _modulated_norm(x, g_ref[...], m_ref, 3, row0, n_ctx).astype(BF16)

    def gate_up(k):
        return (_dot(h, wi_ref[:, k * fc:(k + 1) * fc]), _dot(h, wi_ref[:, FF + k * fc:FF + (k + 1) * fc]))

    nxt = gate_up(0)
    acc = None
    for k in range(FF // fc):
        gt, up = nxt
        if (k + 1) * fc < FF:
            nxt = gate_up(k + 1)
        act = (gt * jax.nn.sigmoid(gt) * up).astype(BF16)
        part = _dot(act, wo_ref[k * fc:(k + 1) * fc, :])
        acc = part if acc is None else acc + part
    o_ref[...] = x + _gate_rows(m_ref, 5, row0, tm, n_ctx) * acc


def ffn(xs, mod, g, wi, wo, n_ctx, tm=384, fc=256):
    B, S, W = xs.shape
    FF = wo.shape[0]
    kern = functools.partial(_ffn_kernel, n_ctx=n_ctx, fc=fc)
    tok = pl.BlockSpec((None, tm, W), lambda b, i: (b, i, 0))
    return pl.pallas_call(
        kern,
        grid=(B, S // tm),
        in_specs=[tok, pl.BlockSpec((None, 12, W), lambda b, i: (b, 0, 0)),
                  pl.BlockSpec((1, W), lambda b, i: (0, 0)),
                  pl.BlockSpec((W, 2 * FF), lambda b, i: (0, 0)),
                  pl.BlockSpec((FF, W), lambda b, i: (0, 0))],
        out_specs=tok,
        out_shape=jax.ShapeDtypeStruct((B, S, W), F32),
        compiler_params=_cparams(("arbitrary", "arbitrary")),
        name="ffn",
    )(xs, mod, g, wi, wo)


def _rope_tables(n_rows, n_ctx):
    row = jnp.repeat(jnp.arange(n_rows), GRID_W).astype(F32)
    col = jnp.tile(jnp.arange(GRID_W), n_rows).astype(F32)
    axis_dim = DA_HEAD_DIM // 2
    inv_freq = ROPE_BASE ** (-jnp.arange(0, axis_dim, 2, dtype=F32) / axis_dim)
    ang = jnp.concatenate([row[:, None] * inv_freq, col[:, None] * inv_freq], axis=-1)
    ang = jnp.concatenate([ang, ang], axis=-1)
    cos, sin = jnp.cos(ang), jnp.sin(ang)
    cos = jnp.concatenate([jnp.ones((n_ctx, DA_HEAD_DIM), F32), cos], axis=0)
    sin = jnp.concatenate([jnp.zeros((n_ctx, DA_HEAD_DIM), F32), sin], axis=0)
    first = jnp.arange(DA_HEAD_DIM) < DA_HEAD_DIM // 2
    sin_a = jnp.where(first, -sin, 0.0)
    sin_b = jnp.where(first, 0.0, sin)
    dup = lambda t: jnp.concatenate([t, t], axis=-1)
    return dup(cos), dup(sin_a), dup(sin_b)


def kernel(x, c, ctx, c_ctx, ada_w, ada_b, norm1_g, norm2_g, w_in, gm_v_g, gm_ws, gm_bs,
           da_q_g, da_k_g, da_lambda, da_subln_g, rw_mu, rw_w0, rw_w2, rw_a0, rw_a2, rw_g2,
           rw_kk, rw_ka, rw_rk, rw_ln_w, rw_ln_b, w_br_a, w_br_b, w_br_c, w_o, ffn_wi, ffn_wo):
    B, T, D = x.shape
    n_ctx = ctx.shape[1]
    L = ada_w.shape[0]
    cos, sin_a, sin_b = _rope_tables(T // GRID_W, n_ctx)
    xs = jnp.concatenate([ctx, x], axis=1)

    cc = jnp.zeros((16, D), F32).at[:B].set(c).at[B].set(c_ctx)
    mods = ada_mod(cc, ada_w, ada_b)
    mod_l = mods[:, :B].reshape(L, B, 6, 1, D)
    mod_c = jnp.broadcast_to(mods[:, B].reshape(L, 1, 6, 1, D), (L, B, 6, 1, D))
    mod12 = jnp.concatenate([mod_c, mod_l], axis=3).reshape(L, B, 12, D)

    gi = jnp.arange(GROUP_W) // RW_HEAD
    bd = (gi[:, None] == gi[None, :]).astype(BF16)
    row = lambda t: t.reshape(1, -1)
    g0, d0, r0, t0 = 0, 2048, 5120, 5120 + 3072
    k0 = r0 + 3488

    for l in range(L):
        lam_init = 0.8 - 0.6 * math.exp(-0.3 * l)
        wl = w_in[l]
        w_cat = jnp.concatenate([wl[:, a:b].astype(BF16) for a, b in
                                 ((g0, d0), (d0, r0), (k0, wl.shape[1]), (r0, t0), (t0, k0))]
                                + [jnp.zeros((D, 96), BF16)], axis=1)
        mu = jnp.concatenate([jnp.zeros((COL_RW,), F32), rw_mu[l], jnp.zeros((96,), F32)]).reshape(1, -1)
        P = proj_in(xs, mod12[l], row(norm1_g[l]), w_cat, mu, n_ctx)

        bs_full = jnp.repeat(gm_bs[l].T, CHUNK, axis=1)
        a_br = gmlp(P, row(gm_v_g[l]), gm_ws[l].astype(BF16), bs_full)

        tile = lambda t, n: jnp.tile(t, n).reshape(1, -1)
        qn, kn, vb = qk_prep(P, cos, sin_a, sin_b, tile(da_q_g[l], 16), tile(da_k_g[l], 16), bd)
        b_br = jnp.concatenate(
            [attention_ctx(qn, kn, vb, da_lambda[l], row(da_subln_g[l]), n_ctx, lam_init),
             attention_lat(qn, kn, vb, da_lambda[l], row(da_subln_g[l]), n_ctx, lam_init)], axis=1)

        zpad = jnp.zeros((DECAY_LORA, D), F32)
        w2p = jnp.stack([jnp.concatenate([rw_w2[l, 0], zpad]), jnp.concatenate([zpad, rw_w2[l, 1]])]).astype(BF16)
        a2p = jnp.stack([jnp.concatenate([rw_a2[l, 0], zpad]), jnp.concatenate([zpad, rw_a2[l, 1]])]).astype(BF16)
        kkn, lw, asig = rw_prep(P, rw_w0[l], w2p, rw_a0[l], a2p, row(rw_kk[l]), bd)
        y = rw_scan(P, kkn, lw, asig, row(rw_ka[l]), n_ctx)

        g2p = jnp.concatenate([rw_g2[l], jnp.zeros((2 * LANES - GATE_LORA, D), F32)]).astype(BF16)
        xs = merge(xs, mod12[l], y, asig, P, a_br, b_br, row(rw_ln_w[l]), row(rw_ln_b[l]),
                   row(rw_rk[l]), row(rw_ka[l]), g2p, bd,
                   w_br_a[l].astype(BF16), w_br_b[l].astype(BF16), w_br_c[l].astype(BF16),
                   w_o[l].astype(BF16), n_ctx)
        xs = ffn(xs, mod12[l], row(norm2_g[l]), ffn_wi[l].astype(BF16), ffn_wo[l].astype(BF16), n_ctx)
    return xs[:, n_ctx:]
```

```python
import functools
import math

import jax
import jax.numpy as jnp
from jax import lax
from jax.experimental import pallas as pl
from jax.experimental.pallas import tpu as pltpu

D_MODEL = 1024
GRID_W = 64
CHUNK = 128
GM_GROUPS = 8
DA_HEADS = 8
DA_HEAD_DIM = 64
DA_V_DIM = 128
ROPE_BASE = 10000.0
RW_HEAD = 64
RW_HEADS = 16
DECAY_LORA = 64
AAA_LORA = 64
GATE_LORA = 160
D_FF = 2816
RMS_EPS = 1e-6
GN_EPS = 64e-5

COL_GM = 0
COL_DA = 2048
COL_GATE = 5120
COL_RW = 8192
COL_TAIL = 11264
NP_COLS = 11776
PROJ_TN = 512
RW_CHUNK = 64
ATTN_KB = 256
ATTN_AHEAD = 3
LANES = 128
GROUP_W = 256
VMEM_LIMIT = 56 * 1024 * 1024

BF16 = jnp.bfloat16
F32 = jnp.float32


def _dot(a, b):
    return jnp.dot(a, b, preferred_element_type=F32)


def _dot_nt(a, b):
    return lax.dot_general(a, b, (((1,), (1,)), ((), ())), preferred_element_type=F32)


def _split_dot(x, m):
    hi = x.astype(BF16)
    lo = (x - hi.astype(F32)).astype(BF16)
    return _dot(hi, m) + _dot(lo, m)


def _split3_dot_left(m, x):
    hi = x.astype(BF16)
    r1 = x - hi.astype(F32)
    mid = r1.astype(BF16)
    lo = (r1 - mid.astype(F32)).astype(BF16)
    return _dot(m, hi) + _dot(m, mid) + _dot(m, lo)


def _group_sum(x, bd):
    width = bd.shape[0]
    parts = [_split_dot(x[:, i:i + width], bd) for i in range(0, x.shape[1], width)]
    return parts[0] if len(parts) == 1 else jnp.concatenate(parts, axis=1)


def _cparams(sem):
    return pltpu.CompilerParams(dimension_semantics=sem, vmem_limit_bytes=VMEM_LIMIT)


def _ada_kernel(c_ref, w_ref, b_ref, o_ref):
    c = c_ref[...]
    s = c * jax.nn.sigmoid(c)
    o_ref[...] = jnp.dot(s, w_ref[...], precision=lax.Precision.HIGHEST,
                         preferred_element_type=F32) + b_ref[...]


def ada_mod(cc, ada_w, ada_b):
    L, D, N = ada_w.shape
    tn = 1536
    return pl.pallas_call(
        _ada_kernel,
        grid=(L, N // tn),
        in_specs=[pl.BlockSpec((16, D), lambda l, j: (0, 0)),
                  pl.BlockSpec((None, D, tn), lambda l, j: (l, 0, j)),
                  pl.BlockSpec((None, 1, tn), lambda l, j: (l, 0, j))],
        out_specs=pl.BlockSpec((None, 16, tn), lambda l, j: (l, 0, j)),
        out_shape=jax.ShapeDtypeStruct((L, 16, N), F32),
        compiler_params=_cparams(("arbitrary", "arbitrary")),
        name="ada_mod",
    )(cc, ada_w, ada_b.reshape(L, 1, N))


def _modulated_norm(x, g, m_ref, piece, row0, n_ctx):
    rows = row0 + lax.broadcasted_iota(jnp.int32, (x.shape[0], 1), 0)
    is_lat = rows >= n_ctx
    sh = jnp.where(is_lat, m_ref[2 * piece + 1:2 * piece + 2, :], m_ref[2 * piece:2 * piece + 1, :])
    sc = jnp.where(is_lat, m_ref[2 * piece + 3:2 * piece + 4, :], m_ref[2 * piece + 2:2 * piece + 3, :])
    y = x * lax.rsqrt(jnp.mean(x * x, axis=-1, keepdims=True) + RMS_EPS) * g
    return y * (1.0 + sc) + sh


def _gate_rows(m_ref, piece, row0, nrows, n_ctx):
    rows = row0 + lax.broadcasted_iota(jnp.int32, (nrows, 1), 0)
    return jnp.where(rows >= n_ctx, m_ref[2 * piece + 1:2 * piece + 2, :], m_ref[2 * piece:2 * piece + 1, :])


def _proj_kernel(x_ref, m_ref, g_ref, w_ref, mu_ref, o_ref, h_ref, p_ref, *, n_ctx, rb, mb, rw_j0):
    S = x_ref.shape[0]
    j = pl.program_id(1)
    nblk = S // rb

    @pl.when(j == 0)
    def _():
        def body(i, carry):
            r0 = pl.multiple_of(i * rb, rb)
            x = x_ref[pl.ds(r0, rb), :]
            h_ref[pl.ds(r0, rb), :] = _modulated_norm(x, g_ref[...], m_ref, 0, r0, n_ctx).astype(BF16)
            return carry
        lax.fori_loop(0, nblk, body, 0)

    def project(store):
        store(0, n_ctx, _dot(h_ref[0:n_ctx, :], w_ref[...]))

        def body(i, carry):
            r0 = pl.multiple_of(n_ctx + i * mb, math.gcd(n_ctx, mb))
            store(r0, mb, _dot(h_ref[pl.ds(r0, mb), :], w_ref[...]))
            return carry
        lax.fori_loop(0, (S - n_ctx) // mb, body, 0)

    @pl.when(j < rw_j0)
    def _():
        def store(r0, rows, val):
            o_ref[pl.ds(r0, rows), :] = val.astype(o_ref.dtype)
        project(store)

    @pl.when(j >= rw_j0)
    def _():
        zero = jnp.zeros((8, p_ref.shape[1]), F32)
        p_ref[0:8, :] = zero
        p_ref[n_ctx + 8:n_ctx + 16, :] = zero
        p_ref[S + 16:S + 24, :] = zero

        def store(r0, rows, val):
            off = 8 if isinstance(r0, int) and r0 < n_ctx else 16
            p_ref[pl.ds(r0 + off, rows), :] = val
        project(store)
        c_cur =1.0 - mu_ref[...]
        c_nb = 0.5 * mu_ref[...]
        for i in range(nblk):
            r0 = i * rb
            src = r0 + (16 if r0 >= n_ctx else 8)
            nb = p_ref[src - 1:src - 1 + rb, :] + p_ref[src + 1:src + 1 + rb, :]
            o_ref[r0:r0 + rb, :] = (p_ref[src:src + rb, :] * c_cur + nb * c_nb).astype(o_ref.dtype)


def proj_in(xs, mod, g, w, mu, n_ctx):
    B, S, D = xs.shape
    NP = w.shape[1]
    tn = PROJ_TN
    kern = functools.partial(_proj_kernel, n_ctx=n_ctx, rb=n_ctx, mb=min(1024, S - n_ctx), rw_j0=COL_RW // tn)
    return pl.pallas_call(
        kern,
        grid=(B, NP // tn),
        in_specs=[pl.BlockSpec((None, S, D), lambda b, j: (b, 0, 0)),
                  pl.BlockSpec((None, 12, D), lambda b, j: (b, 0, 0)),
                  pl.BlockSpec((1, D), lambda b, j: (0, 0)),
                  pl.BlockSpec((D, tn), lambda b, j: (0, j)),
                  pl.BlockSpec((1, tn), lambda b, j: (0, j))],
        out_specs=pl.BlockSpec((None, S, tn), lambda b, j: (b, 0, j)),
        out_shape=jax.ShapeDtypeStruct((B, S, NP), BF16),
        scratch_shapes=[pltpu.VMEM((S, D), BF16), pltpu.VMEM((S + 24, tn), F32)],
        compiler_params=_cparams(("arbitrary", "arbitrary")),
        name="proj_in",
    )(xs, mod, g, w, mu)


def _gelu(x):
    return 0.5 * x * (1.0 + lax.erf(x * (1.0 / math.sqrt(2.0))))


def _gmlp_kernel(p_ref, vg_ref, ws_ref, bs_ref, o_ref):
    W = vg_ref.shape[1]
    gd = W // GM_GROUPS
    for c in range(p_ref.shape[0] // CHUNK):
        rows = slice(c * CHUNK, (c + 1) * CHUNK)
        u = _gelu(p_ref[rows, :W].astype(F32))
        v = _gelu(p_ref[rows, W:].astype(F32))
        v = v * lax.rsqrt(jnp.mean(v * v, axis=-1, keepdims=True) + RMS_EPS) * vg_ref[...]
        vb = v.astype(BF16)
        for g in range(GM_GROUPS):
            sl = slice(g * gd, (g + 1) * gd)
            f = _dot(ws_ref[g], vb[:, sl]) + bs_ref[:, sl]
            o_ref[rows, sl] = (u[:, sl] * f).astype(o_ref.dtype)


def gmlp(P, vg, ws, bs_full, tt=256):
    B, S, _ = P.shape
    W = vg.shape[1]
    return pl.pallas_call(
        _gmlp_kernel,
        grid=(B, S // tt),
        in_specs=[pl.BlockSpec((None, tt, 2 * W), lambda b, i: (b, i, 0)),
                  pl.BlockSpec((1, W), lambda b, i: (0, 0)),
                  pl.BlockSpec((GM_GROUPS, CHUNK, CHUNK), lambda b, i: (0, 0, 0)),
                  pl.BlockSpec((CHUNK, W), lambda b, i: (0, 0))],
        out_specs=pl.BlockSpec((None, tt, W), lambda b, i: (b, i, 0)),
        out_shape=jax.ShapeDtypeStruct((B, S, W), BF16),
        compiler_params=_cparams(("arbitrary", "arbitrary")),
        name="gmlp",
    )(P, vg, ws, bs_full)


def _qk_kernel(q_ref, k_ref, v_ref, cos_ref, sa_ref, sb_ref, qg_ref, kg_ref, bd_ref,
               qo_ref, ko_ref, vo_ref):
    W = q_ref.shape[1]
    reps = W // LANES
    bd = bd_ref[...]
    sw = bd.shape[0]
    cos = jnp.tile(cos_ref[...], (1, sw // LANES))
    sin_a = jnp.tile(sa_ref[...], (1, sw // LANES))
    sin_b = jnp.tile(sb_ref[...], (1, sw // LANES))
    half = DA_HEAD_DIM // 2

    def prep(src, g_ref, dst, scale):
        for c0 in range(0, W, sw):
            t = src[:, c0:c0 + sw].astype(F32)
            ms = _group_sum(t * t, bd) * (1.0 / DA_HEAD_DIM)
            t = t * lax.rsqrt(ms + RMS_EPS) * g_ref[:, c0:c0 + sw]
            t = t * cos + pltpu.roll(t, sw - half, 1) * sin_a + pltpu.roll(t, half, 1) * sin_b
            dst[:, c0:c0 + sw] = (t * scale if scale != 1.0 else t).astype(dst.dtype)

    prep(q_ref, qg_ref, qo_ref, DA_HEAD_DIM ** -0.5 * math.log2(math.e))
    prep(k_ref, kg_ref, ko_ref, 1.0)
    for h in range(reps):
        vo_ref[h] = v_ref[:, h * LANES:(h + 1) * LANES].astype(F32).T.astype(BF16)


def qk_prep(P, cos, sin_a, sin_b, qg, kg, bd, tt=256):
    B, S, _ = P.shape
    W = D_MODEL
    c0 = COL_DA // W
    tok = lambda c: pl.BlockSpec((None, tt, W), lambda b, i, c=c: (b, i, c))
    tab = pl.BlockSpec((tt, LANES), lambda b, i: (i, 0))
    row = pl.BlockSpec((1, W), lambda b, i: (0, 0))
    out = pl.BlockSpec((None, tt, W), lambda b, i: (b, i, 0))
    shp = jax.ShapeDtypeStruct((B, S, W), BF16)
    return pl.pallas_call(
        _qk_kernel,
        grid=(B, S // tt),
        in_specs=[tok(c0), tok(c0 + 1), tok(c0 + 2), tab, tab, tab, row, row,
                  pl.BlockSpec((GROUP_W, GROUP_W), lambda b, i: (0, 0))],
        out_specs=[out, out, pl.BlockSpec((None, W // LANES, LANES, tt), lambda b, i: (b, 0, 0, i))],
        out_shape=[shp, shp, jax.ShapeDtypeStruct((B, W // LANES, LANES, S), BF16)],
        compiler_params=_cparams(("arbitrary", "arbitrary")),
        name="qk_prep",
    )(P, P, P, cos, sin_a, sin_b, qg, kg, bd)


def _attn_kernel(q_ref, k_ref, v_ref, lam_ref, sg_ref, o_ref, *, n_ctx, lam_init):
    i = pl.program_id(2)
    tq = q_ref.shape[0]
    lp = lam_ref[...]
    lam = (jnp.exp(jnp.sum(lp[0:1] * lp[1:2], axis=-1, keepdims=True))
           - jnp.exp(jnp.sum(lp[2:3] * lp[3:4], axis=-1, keepdims=True)) + lam_init)

    def attend(nk):
        q = q_ref[...]
        lane = lax.broadcasted_iota(jnp.int32, q.shape, 1)
        zero = jnp.zeros_like(q)
        qs = jnp.concatenate([jnp.where(lane < DA_HEAD_DIM, q, zero),
                              jnp.where(lane >= DA_HEAD_DIM, q, zero)], axis=0)
        m = l = acc = None
        nblk = nk // ATTN_KB
        blk = lambda j: slice(j * ATTN_KB, (j + 1) * ATTN_KB)
        scores = lambda j: _dot_nt(k_ref[blk(j), :], qs)
        queue = [scores(j) for j in range(min(ATTN_AHEAD, nblk))]
        for j in range(nblk):
            ks = blk(j)
            if j + ATTN_AHEAD < nblk:
                queue.append(scores(j + ATTN_AHEAD))
            s = queue.pop(0)
            m_blk = jnp.max(s, axis=0, keepdims=True)
            m_new = m_blk if m is None else jnp.maximum(m, m_blk)
            e = jnp.exp2(s - m_new)
            pv = _dot(v_ref[:, ks], e.astype(BF16))
            if m is None:
                l, acc = jnp.sum(e, axis=0, keepdims=True), pv
            else:
                alpha = jnp.exp2(m - m_new)
                l = alpha * l + jnp.sum(e, axis=0, keepdims=True)
                acc = alpha * acc + pv
            m = m_new
        inv = 1.0 / l
        o = acc[:, :tq] * inv[:, :tq] - acc[:, tq:] * (lam * inv[:, tq:])
        o = o * lax.rsqrt(jnp.mean(o * o, axis=0, keepdims=True) + RMS_EPS)
        o_ref[...] = (o.T * sg_ref[...] * (1.0 - lam_init)).astype(o_ref.dtype)

    attend(k_ref.shape[0])


def attention_ctx(qn, kn, vt, lam_p, sg, n_ctx, lam_init, tq=256):
    B, S, W = qn.shape
    H = W // LANES
    kern = functools.partial(_attn_kernel, n_ctx=n_ctx, lam_init=lam_init)
    return pl.pallas_call(
        kern,
        grid=(B, H, n_ctx // tq),
        in_specs=[pl.BlockSpec((None, tq, LANES), lambda b, h, i: (b, i, h)),
                  pl.BlockSpec((None, n_ctx, LANES), lambda b, h, i: (b, 0, h)),
                  pl.BlockSpec((None, None, LANES, n_ctx), lambda b, h, i: (b, h, 0, 0)),
                  pl.BlockSpec((4, DA_HEAD_DIM), lambda b, h, i: (0, 0)),
                  pl.BlockSpec((1, LANES), lambda b, h, i: (0, 0))],
        out_specs=pl.BlockSpec((None, tq, LANES), lambda b, h, i: (b, i, h)),
        out_shape=jax.ShapeDtypeStruct((B, n_ctx, W), BF16),
        compiler_params=_cparams(("arbitrary", "arbitrary", "arbitrary")),
        name="attention_ctx",
    )(qn, kn, vt, lam_p, sg)


def _attn_lat_kernel(q_ref, k_ref, v_ref, lam_ref, sg_ref, o_ref, s_ref, m_ref, *, lam_init, nq):
    t = pl.program_id(2)
    tq = q_ref.shape[0]
    nblk = k_ref.shape[0] // ATTN_KB
    blk = lambda j: slice(j * ATTN_KB, (j + 1) * ATTN_KB)

    def stacked_q():
        q = q_ref[...]
        lane = lax.broadcasted_iota(jnp.int32, q.shape, 1)
        zero = jnp.zeros_like(q)
        return jnp.concatenate([jnp.where(lane < DA_HEAD_DIM, q, zero),
                                jnp.where(lane >= DA_HEAD_DIM, q, zero)], axis=0)

    def run(scores_slot, softmax_slot):
        qs = stacked_q() if scores_slot is not None else None
        m_run = l = acc = None
        if softmax_slot is not None:
            m_prev = m_ref[softmax_slot]
        for j in range(nblk):
            if scores_slot is not None:
                s = _dot_nt(k_ref[blk(j), :], qs)
                s_ref[scores_slot, blk(j), :] = s
                for r in range(0, ATTN_KB, 8):
                    m_run = s[r:r + 8] if m_run is None else jnp.maximum(m_run, s[r:r + 8])
            if softmax_slot is not None:
                e = jnp.exp2(s_ref[softmax_slot, blk(j), :] - m_prev)
                pv = _dot(v_ref[:, blk(j)], e.astype(BF16))
                lsum = jnp.sum(e, axis=0, keepdims=True)
                l, acc = (lsum, pv) if l is None else (l + lsum, acc + pv)
        if scores_slot is not None:
            m_ref[scores_slot] = jnp.max(m_run, axis=0, keepdims=True)
        if softmax_slot is not None:
            lp = lam_ref[...]
            lam = (jnp.exp(jnp.sum(lp[0:1] * lp[1:2], axis=-1, keepdims=True))
                   - jnp.exp(jnp.sum(lp[2:3] * lp[3:4], axis=-1, keepdims=True)) + lam_init)
            inv = 1.0 / l
            o = acc[:, :tq] * inv[:, :tq] - acc[:, tq:] * (lam * inv[:, tq:])
            o = o * lax.rsqrt(jnp.mean(o * o, axis=0, keepdims=True) + RMS_EPS)
            o_ref[...] = (o.T * sg_ref[...] * (1.0 - lam_init)).astype(o_ref.dtype)

    @pl.when(t == 0)
    def _():
        run(0, None)

    for par in range(2):
        @pl.when((t > 0) & (t < nq) & (t % 2 == par))
        def _():
            run(par, 1 - par)

    @pl.when(t == nq)
    def _():
        run(None, (nq - 1) % 2)


def attention_lat(qn, kn, vt, lam_p, sg, n_ctx, lam_init, tq=512):
    B, S, W = qn.shape
    H = W // LANES
    nq = (S - n_ctx) // tq
    q_lat = qn[:, n_ctx:]
    kern = functools.partial(_attn_lat_kernel, lam_init=lam_init, nq=nq)
    return pl.pallas_call(
        kern,
        grid=(B, H, nq + 1),
        in_specs=[pl.BlockSpec((None, tq, LANES), lambda b, h, t: (b, jnp.minimum(t, nq - 1), h)),
                  pl.BlockSpec((None, S, LANES), lambda b, h, t: (b, 0, h)),
                  pl.BlockSpec((None, None, LANES, S), lambda b, h, t: (b, h, 0, 0)),
                  pl.BlockSpec((4, DA_HEAD_DIM), lambda b, h, t: (0, 0)),
                  pl.BlockSpec((1, LANES), lambda b, h, t: (0, 0))],
        out_specs=pl.BlockSpec((None, tq, LANES), lambda b, h, t: (b, jnp.maximum(t - 1, 0), h)),
        out_shape=jax.ShapeDtypeStruct((B, S - n_ctx, W), BF16),
        scratch_shapes=[pltpu.VMEM((2, S, 2 * tq), F32), pltpu.VMEM((2, 1, 2 * tq), F32)],
        compiler_params=_cparams(("arbitrary", "arbitrary", "arbitrary")),
        name="attention_lat",
    )(q_lat, kn, vt, lam_p, sg)


def _rw_prep_kernel(zk_ref, zt_ref, w0_ref, w2_ref, a0_ref, a2_ref, kk_ref, bd_ref,
                    kkn_ref, lw_ref, as_ref):
    tw = jnp.tanh(zt_ref[:, 0:LANES].astype(F32)).astype(BF16)
    za = zt_ref[:, LANES:2 * LANES]
    for d in range(2):
        x = w0_ref[d:d + 1, :] + _dot(tw, w2_ref[d])
        lw_ref[d] = -math.exp(-0.5) * jax.nn.sigmoid(x)
        as_ref[d] = jax.nn.sigmoid(a0_ref[d:d + 1, :] + _dot(za, a2_ref[d])).astype(as_ref.dtype)
    kk = zk_ref[...].astype(F32) * kk_ref[...]
    nrm = jnp.sqrt(_group_sum(kk * kk, bd_ref[...]))
    kkn_ref[...] = (kk / jnp.maximum(nrm, 1e-12)).astype(kkn_ref.dtype)


def rw_prep(P, w0, w2p, a0, a2p, kkp, bd, tt=256):
    B, S, _ = P.shape
    W = D_MODEL
    tokw = pl.BlockSpec((None, tt, W), lambda b, i: (b, i, COL_RW // W + 1))
    tail = pl.BlockSpec((None, tt, 512), lambda b, i: (b, i, COL_TAIL // 512))
    full = lambda shape: pl.BlockSpec(shape, lambda b, i: (0,) * len(shape))
    out2 = pl.BlockSpec((2, None, tt, W), lambda b, i: (0, b, i, 0))
    return pl.pallas_call(
        _rw_prep_kernel,
        grid=(B, S // tt),
        in_specs=[tokw, tail, full((2, W)), full((2, LANES, W)), full((2, W)), full((2, LANES, W)),
                  full((1, W)), full((GROUP_W, GROUP_W))],
        out_specs=[pl.BlockSpec((None, tt, W), lambda b, i: (b, i, 0)), out2, out2],
        out_shape=[jax.ShapeDtypeStruct((B, S, W), BF16),
                   jax.ShapeDtypeStruct((2, B, S, W), F32),
                   jax.ShapeDtypeStruct((2, B, S, W), BF16)],
        compiler_params=_cparams(("arbitrary", "arbitrary")),
        name="rw_prep",
    )(P, P, w0, w2p, a0, a2p, kkp, bd)


def _rw_scan_kernel(r_ref, zk_ref, v_ref, kkn_ref, lw_ref, as_ref, ka_ref, y_ref, h_ref):
    c = pl.program_id(1)
    C = RW_CHUNK
    R2 = 2 * C
    n_pairs = ka_ref.shape[1] // LANES

    @pl.when(c == 0)
    def _():
        h_ref[...] = jnp.zeros_like(h_ref)

    ri = lax.broadcasted_iota(jnp.int32, (C, C), 0)
    ci = lax.broadcasted_iota(jnp.int32, (C, C), 1)
    rt = lax.broadcasted_iota(jnp.int32, (C, LANES), 0)
    lane = lax.broadcasted_iota(jnp.int32, (C, LANES), 1)
    cs = lane & (C - 1)
    levels = int(math.log2(C))
    same_blk = [(rt >> k) == (cs >> k) for k in range(levels + 1)]
    joins = [same_blk[k + 1] & ~same_blk[k] for k in range(levels)]
    eye = jnp.where(rt == cs, 1.0, 0.0)
    first = lane < RW_HEAD
    hr = lax.broadcasted_iota(jnp.int32, (LANES, LANES), 0)
    hc = lax.broadcasted_iota(jnp.int32, (LANES, LANES), 1)
    same_head = (hr >= RW_HEAD) == (hc >= RW_HEAD)
    ka = ka_ref[...]

    def stack(x):
        return jnp.concatenate([jnp.where(first, x, 0.0), jnp.where(first, 0.0, x)], axis=0)

    chains = []
    for d in range(2):
        before_c = (ci <= ri) if d == 0 else (ci >= ri)
        strict = (cs < rt) if d == 0 else (cs > rt)
        incl = (cs <= rt) if d == 0 else (cs >= rt)
        lw = lw_ref[d][...]
        L = _split3_dot_left(jnp.where(before_c, 1.0, 0.0).astype(BF16), lw)
        ltot = jnp.sum(lw, axis=0, keepdims=True)
        lmid = 0.5 * ltot
        e_r = jnp.exp(L - lmid)
        e_a = jnp.exp(L - lw - lmid)
        e_b = jnp.exp(lmid - L)
        e_mid = jnp.exp(lmid)
        asig = as_ref[d][...].astype(F32)
        kkn = kkn_ref[d][...].astype(F32)
        kd = zk_ref[d][...].astype(F32) * (1.0 + (asig - 1.0) * ka)
        a_in = -kkn * e_a
        r_in = r_ref[d][...].astype(F32) * e_r
        b_in = kkn * asig * e_b
        k_in = kd * e_b
        a_h = a_in * e_mid
        r_h = r_in * e_mid
        b_o = b_in * e_mid
        k_o = k_in * e_mid
        wtot = jnp.exp(ltot)
        vv = v_ref[d][...].astype(F32)
        for p in range(n_pairs):
            sl = slice(p * LANES, (p + 1) * LANES)
            chains.append(dict(
                d=d, p=p, sl=sl, strict=strict, incl=incl, wtot=wtot[:, sl],
                AR=jnp.concatenate([a_in[:, sl], r_in[:, sl]], axis=0).astype(BF16),
                BK=jnp.concatenate([stack(b_in[:, sl]), stack(k_in[:, sl])], axis=0).astype(BF16),
                ARh=jnp.concatenate([a_h[:, sl], r_h[:, sl]], axis=0).astype(BF16),
                BKo=jnp.concatenate([b_o[:, sl], k_o[:, sl]], axis=0).astype(BF16),
                V=vv[:, sl]))

    for ch in chains:
        G1 = _dot_nt(ch["AR"], ch["BK"])
        ch["Aab"] = jnp.where(ch["strict"], G1[:C, :R2], 0.0)
        ch["Aak"] = jnp.where(ch["strict"], G1[:C, R2:], 0.0).astype(BF16)
        ch["Ar"] = jnp.concatenate([jnp.where(ch["incl"], G1[C:, :R2], 0.0),
                                    jnp.where(ch["incl"], G1[C:, R2:], 0.0)], axis=1).astype(BF16)
        ch["H"] = h_ref[ch["d"], ch["p"]]
        ch["T"] = eye + jnp.where(joins[0], ch["Aab"], 0.0)
        ch["As"] = stack(ch["Aab"]).astype(BF16)
    for ch in chains:
        ch["G2"] = _dot_nt(ch["ARh"], ch["H"].astype(BF16))
        ch["Vs"] = stack(ch["V"]).astype(BF16)
        ch["rhs"] = ch["G2"][:C] + _dot(ch["Aak"], ch["Vs"])
    for join in joins[1:]:
        for ch in chains:
            ch["TA"] = _dot(ch["T"].astype(BF16), ch["As"]).astype(BF16)
        for ch in chains:
            ch["T"] = ch["T"] + jnp.where(join, _dot(ch["TA"], stack(ch["T"]).astype(BF16)), 0.0)
    for ch in chains:
        ch["U"] = _dot(ch["T"].astype(BF16), stack(ch["rhs"]).astype(BF16))
    for ch in chains:
        UVs = jnp.concatenate([stack(ch["U"]).astype(BF16), ch["Vs"]], axis=0)
        y_ref[ch["d"]][:, ch["sl"]] = ch["G2"][C:] + _dot(ch["Ar"], UVs)
    for ch in chains:
        UVt = jnp.concatenate([ch["U"], ch["V"]], axis=0).T.astype(BF16)
        upd = jnp.where(same_head, _dot(UVt, ch["BKo"]), 0.0)
        h_ref[ch["d"], ch["p"]] = ch["H"] * ch["wtot"] + upd


def rw_scan(P, kkn, lw, asig, ka, n_ctx):
    B, S, _ = P.shape
    W = D_MODEL
    C = RW_CHUNK
    nc = S // C
    ncc = n_ctx // C

    def rev(c):
        return jnp.where(c < ncc, ncc - 1 - c, nc - 1 + ncc - c)

    def both(make):
        return [make(lambda c: c), make(rev)]

    pcol = lambda col: both(lambda f: pl.BlockSpec((None, C, W), lambda b, c, f=f: (b, f(c), col)))
    tok = both(lambda f: pl.BlockSpec((None, C, W), lambda b, c, f=f: (b, f(c), 0)))
    dtok = [pl.BlockSpec((None, None, C, W), lambda b, c: (0, b, c, 0)),
            pl.BlockSpec((None, None, C, W), lambda b, c: (1, b, rev(c), 0))]
    c0 = COL_RW // W

    def kern(rf, rb_, kf, kb_, vf, vb_, nf, nb_, lwf, lwb, asf, asb, ka_ref, yf, yb, h_ref):
        _rw_scan_kernel((rf, rb_), (kf, kb_), (vf, vb_), (nf, nb_), (lwf, lwb), (asf, asb), ka_ref,
                        (yf, yb), h_ref)

    return pl.pallas_call(
        kern,
        grid=(B, nc),
        in_specs=[*pcol(c0), *pcol(c0 + 1), *pcol(c0 + 2), *tok, *dtok, *dtok,
                  pl.BlockSpec((1, W), lambda b, c: (0, 0))],
        out_specs=[pl.BlockSpec((None, C, W), lambda b, c: (b, c, 0)),
                   pl.BlockSpec((None, C, W), lambda b, c: (b, rev(c), 0))],
        out_shape=[jax.ShapeDtypeStruct((B, S, W), F32)] * 2,
        scratch_shapes=[pltpu.VMEM((2, W // LANES, LANES, LANES), F32)],
        compiler_params=_cparams(("arbitrary", "arbitrary")),
        name="rw_scan",
    )(P, P, P, P, P, P, kkn, kkn, lw, lw, asig, asig, ka)


def _merge_kernel(x_ref, m_ref, yf_ref, yb_ref, as_ref, r_ref, zk_ref, v_ref, zt_ref, ga_ref, gb_ref, gc_ref,
                  a_ref, bc_ref, bl_ref, lnw_ref, lnb_ref, rk_ref, ka_ref, g2_ref, bd_ref,
                  wa_ref, wb_ref, wc_ref, wo_ref, o_ref, *, n_ctx):
    tt = x_ref.shape[0]
    row0 = pl.program_id(1) * tt
    bd = bd_ref[...]
    inv_n = 1.0 / RW_HEAD
    y = yf_ref[...] + yb_ref[...]
    mean = _group_sum(y, bd) * inv_n
    yc = y - mean
    var = _group_sum(yc * yc, bd) * inv_n
    yn = yc * lax.rsqrt(var + GN_EPS) * lnw_ref[...] + lnb_ref[...]
    zk = zk_ref[...].astype(F32)
    ka = ka_ref[...]
    kd_sum = (zk * (1.0 + (as_ref[0].astype(F32) - 1.0) * ka)
              + zk * (1.0 + (as_ref[1].astype(F32) - 1.0) * ka))
    vv = v_ref[...].astype(F32)
    bonus = _group_sum(r_ref[...].astype(F32) * kd_sum * rk_ref[...], bd) * vv
    g = _dot(jax.nn.sigmoid(zt_ref[:, 2 * LANES:].astype(F32)).astype(BF16), g2_ref[...])
    c_br = ((yn + bonus) * g).astype(BF16)
    gate = lambda ref: jax.nn.sigmoid(ref[...].astype(F32))
    m = (gate(ga_ref) * _dot(a_ref[...], wa_ref[...])
         + gate(gb_ref) * _dot(jnp.where(row0 < n_ctx, bc_ref[...], bl_ref[...]), wb_ref[...])
         + gate(gc_ref) * _dot(c_br, wc_ref[...]))
    out = _dot(m.astype(BF16), wo_ref[...])
    o_ref[...] = x_ref[...] + _gate_rows(m_ref, 2, row0, tt, n_ctx) * out


def merge(xs, mod, y, asig, P, a_br, b_ctx, b_lat, lnw, lnb, rk, ka, g2p, bd, wa, wb, wc, wo, n_ctx, tt=256):
    B, S, W = xs.shape
    tok = pl.BlockSpec((None, tt, W), lambda b, i: (b, i, 0))
    pcol = lambda col: pl.BlockSpec((None, tt, W), lambda b, i, col=col: (b, i, col))
    dtok = pl.BlockSpec((2, None, tt, W), lambda b, i: (0, b, i, 0))
    row = pl.BlockSpec((1, W), lambda b, i: (0, 0))
    wsq = pl.BlockSpec((W, W), lambda b, i: (0, 0))
    cr, cg = COL_RW // W, COL_GATE // W
    nct = n_ctx // tt
    kern = functools.partial(_merge_kernel, n_ctx=n_ctx)
    return pl.pallas_call(
        kern,
        grid=(B, S // tt),
        in_specs=[tok, pl.BlockSpec((None, 12, W), lambda b, i: (b, 0, 0)), tok, tok, dtok,
                  pcol(cr), pcol(cr + 1), pcol(cr + 2),
                  pl.BlockSpec((None, tt, 512), lambda b, i: (b, i, COL_TAIL // 512)),
                  pcol(cg), pcol(cg + 1), pcol(cg + 2), tok,
                  pl.BlockSpec((None, tt, W), lambda b, i: (b, jnp.minimum(i, nct - 1), 0)),
                  pl.BlockSpec((None, tt, W), lambda b, i: (b, jnp.maximum(i - nct, 0), 0)),
                  row, row, row, row,
                  pl.BlockSpec((2 * LANES, W), lambda b, i: (0, 0)),
                  pl.BlockSpec((GROUP_W, GROUP_W), lambda b, i: (0, 0)),
                  wsq, wsq, wsq, wsq],
        out_specs=tok,
        out_shape=jax.ShapeDtypeStruct((B, S, W), F32),
        compiler_params=_cparams(("arbitrary", "arbitrary")),
        name="merge",
    )(xs, mod, y[0], y[1], asig, P, P, P, P, P, P, P, a_br, b_ctx, b_lat, lnw, lnb, rk, ka, g2p, bd, wa, wb, wc, wo)


def _ffn_kernel(x_ref, m_ref, g_ref, wi_ref, wo_ref, o_ref, *, n_ctx, fc):
    tm = x_ref.shape[0]
    row0 = pl.program_id(1) * tm
    FF = wo_ref.shape[0]
    x = x_ref[...]
    h = _modulated_norm(x, g_ref[...], m_ref, 3, row0, n_ctx).astype(BF16)

    def gate_up(k):
        return (_dot(h, wi_ref[:, k * fc:(k + 1) * fc]), _dot(h, wi_ref[:, FF + k * fc:FF + (k + 1) * fc]))

    nxt = gate_up(0)
    acc = None
    for k in range(FF // fc):
        gt, up = nxt
        if (k + 1) * fc < FF:
            nxt = gate_up(k + 1)
        act = (gt * jax.nn.sigmoid(gt) * up).astype(BF16)
        part = _dot(act, wo_ref[k * fc:(k + 1) * fc, :])
        acc = part if acc is None else acc + part
    o_ref[...] = x + _gate_rows(m_ref, 5, row0, tm, n_ctx) * acc


def ffn(xs, mod, g, wi, wo, n_ctx, tm=384, fc=256):
    B, S, W = xs.shape
    FF = wo.shape[0]
    kern = functools.partial(_ffn_kernel, n_ctx=n_ctx, fc=fc)
    tok = pl.BlockSpec((None, tm, W), lambda b, i: (b, i, 0))
    return pl.pallas_call(
        kern,
        grid=(B, S // tm),
        in_specs=[tok, pl.BlockSpec((None, 12, W), lambda b, i: (b, 0, 0)),
                  pl.BlockSpec((1, W), lambda b, i: (0, 0)),
                  pl.BlockSpec((W, 2 * FF), lambda b, i: (0, 0)),
                  pl.BlockSpec((FF, W), lambda b, i: (0, 0))],
        out_specs=tok,
        out_shape=jax.ShapeDtypeStruct((B, S, W), F32),
        compiler_params=_cparams(("arbitrary", "arbitrary")),
        name="ffn",
    )(xs, mod, g, wi, wo)


def _rope_tables(n_rows, n_ctx):
    row = jnp.repeat(jnp.arange(n_rows), GRID_W).astype(F32)
    col = jnp.tile(jnp.arange(GRID_W), n_rows).astype(F32)
    axis_dim = DA_HEAD_DIM // 2
    inv_freq = ROPE_BASE ** (-jnp.arange(0, axis_dim, 2, dtype=F32) / axis_dim)
    ang = jnp.concatenate([row[:, None] * inv_freq, col[:, None] * inv_freq], axis=-1)
    ang = jnp.concatenate([ang, ang], axis=-1)
    cos, sin = jnp.cos(ang), jnp.sin(ang)
    cos = jnp.concatenate([jnp.ones((n_ctx, DA_HEAD_DIM), F32), cos], axis=0)
    sin = jnp.concatenate([jnp.zeros((n_ctx, DA_HEAD_DIM), F32), sin], axis=0)
    first = jnp.arange(DA_HEAD_DIM) < DA_HEAD_DIM // 2
    sin_a = jnp.where(first, -sin, 0.0)
    sin_b = jnp.where(first, 0.0, sin)
    dup = lambda t: jnp.concatenate([t, t], axis=-1)
    return dup(cos), dup(sin_a), dup(sin_b)


def kernel(x, c, ctx, c_ctx, ada_w, ada_b, norm1_g, norm2_g, w_in, gm_v_g, gm_ws, gm_bs,
           da_q_g, da_k_g, da_lambda, da_subln_g, rw_mu, rw_w0, rw_w2, rw_a0, rw_a2, rw_g2,
           rw_kk, rw_ka, rw_rk, rw_ln_w, rw_ln_b, w_br_a, w_br_b, w_br_c, w_o, ffn_wi, ffn_wo):
    B, T, D = x.shape
    n_ctx = ctx.shape[1]
    L = ada_w.shape[0]
    cos, sin_a, sin_b = _rope_tables(T // GRID_W, n_ctx)
    xs = jnp.concatenate([ctx, x], axis=1)

    cc = jnp.zeros((16, D), F32).at[:B].set(c).at[B].set(c_ctx)
    mods = ada_mod(cc, ada_w, ada_b)
    mod_l = mods[:, :B].reshape(L, B, 6, 1, D)
    mod_c = jnp.broadcast_to(mods[:, B].reshape(L, 1, 6, 1, D), (L, B, 6, 1, D))
    mod12 = jnp.concatenate([mod_c, mod_l], axis=3).reshape(L, B, 12, D)

    gi = jnp.arange(GROUP_W) // RW_HEAD
    bd = (gi[:, None] == gi[None, :]).astype(BF16)
    row = lambda t: t.reshape(1, -1)
    g0, d0, r0, t0 = 0, 2048, 5120, 5120 + 3072
    k0 = r0 + 3488

    for l in range(L):
        lam_init = 0.8 - 0.6 * math.exp(-0.3 * l)
        wl = w_in[l]
        w_cat = jnp.concatenate([wl[:, a:b].astype(BF16) for a, b in
                                 ((g0, d0), (d0, r0), (k0, wl.shape[1]), (r0, t0), (t0, k0))]
                                + [jnp.zeros((D, 96), BF16)], axis=1)
        mu = jnp.concatenate([jnp.zeros((COL_RW,), F32), rw_mu[l], jnp.zeros((96,), F32)]).reshape(1, -1)
        P = proj_in(xs, mod12[l], row(norm1_g[l]), w_cat, mu, n_ctx)

        bs_full = jnp.repeat(gm_bs[l].T, CHUNK, axis=1)
        a_br = gmlp(P, row(gm_v_g[l]), gm_ws[l].astype(BF16), bs_full)

        tile = lambda t, n: jnp.tile(t, n).reshape(1, -1)
        qn, kn, vb = qk_prep(P, cos, sin_a, sin_b, tile(da_q_g[l], 16), tile(da_k_g[l], 16), bd)
        b_ctx = attention_ctx(qn, kn, vb, da_lambda[l], row(da_subln_g[l]), n_ctx, lam_init)
        b_lat = attention_lat(qn, kn, vb, da_lambda[l], row(da_subln_g[l]), n_ctx, lam_init)

        zpad = jnp.zeros((DECAY_LORA, D), F32)
        w2p = jnp.stack([jnp.concatenate([rw_w2[l, 0], zpad]), jnp.concatenate([zpad, rw_w2[l, 1]])]).astype(BF16)
        a2p = jnp.stack([jnp.concatenate([rw_a2[l, 0], zpad]), jnp.concatenate([zpad, rw_a2[l, 1]])]).astype(BF16)
        kkn, lw, asig = rw_prep(P, rw_w0[l], w2p, rw_a0[l], a2p, row(rw_kk[l]), bd)
        y = rw_scan(P, kkn, lw, asig, row(rw_ka[l]), n_ctx)

        g2p = jnp.concatenate([rw_g2[l], jnp.zeros((2 * LANES - GATE_LORA, D), F32)]).astype(BF16)
        xs = merge(xs, mod12[l], y, asig, P, a_br, b_ctx, b_lat, row(rw_ln_w[l]), row(rw_ln_b[l]),
                   row(rw_rk[l]), row(rw_ka[l]), g2p, bd,
                   w_br_a[l].astype(BF16), w_br_b[l].astype(BF16), w_br_c[l].astype(BF16),
                   w_o[l].astype(BF16), n_ctx)
        xs = ffn(xs, mod12[l], row(norm2_g[l]), ffn_wi[l].astype(BF16), ffn_wo[l].astype(BF16), n_ctx)
    return xs[:, n_ctx:]
```

```python
import functools
import math

import jax
import jax.numpy as jnp
from jax import lax
from jax.experimental import pallas as pl
from jax.experimental.pallas import tpu as pltpu

D_MODEL = 1024
GRID_W = 64
CHUNK = 128
GM_GROUPS = 8
DA_HEADS = 8
DA_HEAD_DIM = 64
DA_V_DIM = 128
ROPE_BASE = 10000.0
RW_HEAD = 64
RW_HEADS = 16
DECAY_LORA = 64
AAA_LORA = 64
GATE_LORA = 160
D_FF = 2816
RMS_EPS = 1e-6
GN_EPS = 64e-5

COL_GM = 0
COL_DA = 2048
COL_GATE = 5120
COL_RW = 8192
COL_TAIL = 11264
NP_COLS = 11776
PROJ_TN = 512
RW_CHUNK = 64
RW_SUB = 2
ATTN_KB = 256
ATTN_AHEAD = 3
LANES = 128
GROUP_W = 256
VMEM_LIMIT = 56 * 1024 * 1024

BF16 = jnp.bfloat16
F32 = jnp.float32


def _dot(a, b):
    return jnp.dot(a, b, preferred_element_type=F32)


def _dot_nt(a, b):
    return lax.dot_general(a, b, (((1,), (1,)), ((), ())), preferred_element_type=F32)


def _split_dot(x, m):
    hi = x.astype(BF16)
    lo = (x - hi.astype(F32)).astype(BF16)
    return _dot(hi, m) + _dot(lo, m)


def _split3_dot_left(m, x):
    hi = x.astype(BF16)
    r1 = x - hi.astype(F32)
    mid = r1.astype(BF16)
    lo = (r1 - mid.astype(F32)).astype(BF16)
    return _dot(m, hi) + _dot(m, mid) + _dot(m, lo)


def _group_sum(x, bd):
    width = bd.shape[0]
    parts = [_split_dot(x[:, i:i + width], bd) for i in range(0, x.shape[1], width)]
    return parts[0] if len(parts) == 1 else jnp.concatenate(parts, axis=1)


def _cparams(sem):
    return pltpu.CompilerParams(dimension_semantics=sem, vmem_limit_bytes=VMEM_LIMIT)


def _ada_kernel(c_ref, w_ref, b_ref, o_ref):
    c = c_ref[...]
    s = c * jax.nn.sigmoid(c)
    o_ref[...] = jnp.dot(s, w_ref[...], precision=lax.Precision.HIGHEST,
                         preferred_element_type=F32) + b_ref[...]


def ada_mod(cc, ada_w, ada_b):
    L, D, N = ada_w.shape
    tn = 1536
    return pl.pallas_call(
        _ada_kernel,
        grid=(L, N // tn),
        in_specs=[pl.BlockSpec((16, D), lambda l, j: (0, 0)),
                  pl.BlockSpec((None, D, tn), lambda l, j: (l, 0, j)),
                  pl.BlockSpec((None, 1, tn), lambda l, j: (l, 0, j))],
        out_specs=pl.BlockSpec((None, 16, tn), lambda l, j: (l, 0, j)),
        out_shape=jax.ShapeDtypeStruct((L, 16, N), F32),
        compiler_params=_cparams(("arbitrary", "arbitrary")),
        name="ada_mod",
    )(cc, ada_w, ada_b.reshape(L, 1, N))


def _modulated_norm(x, g, m_ref, piece, row0, n_ctx):
    rows = row0 + lax.broadcasted_iota(jnp.int32, (x.shape[0], 1), 0)
    is_lat = rows >= n_ctx
    sh = jnp.where(is_lat, m_ref[2 * piece + 1:2 * piece + 2, :], m_ref[2 * piece:2 * piece + 1, :])
    sc = jnp.where(is_lat, m_ref[2 * piece + 3:2 * piece + 4, :], m_ref[2 * piece + 2:2 * piece + 3, :])
    y = x * lax.rsqrt(jnp.mean(x * x, axis=-1, keepdims=True) + RMS_EPS) * g
    return y * (1.0 + sc) + sh


def _gate_rows(m_ref, piece, row0, nrows, n_ctx):
    rows = row0 + lax.broadcasted_iota(jnp.int32, (nrows, 1), 0)
    return jnp.where(rows >= n_ctx, m_ref[2 * piece + 1:2 * piece + 2, :], m_ref[2 * piece:2 * piece + 1, :])


def _proj_kernel(x_ref, m_ref, g_ref, w_ref, mu_ref, o_ref, h_ref, p_ref, *, n_ctx, rb, mb, rw_j0):
    S = x_ref.shape[0]
    j = pl.program_id(1)
    nblk = S // rb

    @pl.when(j == 0)
    def _():
        def body(i, carry):
            r0 = pl.multiple_of(i * rb, rb)
            x = x_ref[pl.ds(r0, rb), :]
            h_ref[pl.ds(r0, rb), :] = _modulated_norm(x, g_ref[...], m_ref, 0, r0, n_ctx).astype(BF16)
            return carry
        lax.fori_loop(0, nblk, body, 0)

    def project(store):
        store(0, n_ctx, _dot(h_ref[0:n_ctx, :], w_ref[...]))

        def body(i, carry):
            r0 = pl.multiple_of(n_ctx + i * mb, math.gcd(n_ctx, mb))
            store(r0, mb, _dot(h_ref[pl.ds(r0, mb), :], w_ref[...]))
            return carry
        lax.fori_loop(0, (S - n_ctx) // mb, body, 0)

    @pl.when(j < rw_j0)
    def _():
        def store(r0, rows, val):
            o_ref[pl.ds(r0, rows), :] = val.astype(o_ref.dtype)
        project(store)

    @pl.when(j >= rw_j0)
    def _():
        zero = jnp.zeros((8, p_ref.shape[1]), F32)
        p_ref[0:8, :] = zero
        p_ref[n_ctx + 8:n_ctx + 16, :] = zero
        p_ref[S + 16:S + 24, :] = zero

        def store(r0, rows, val):
            off = 8 if isinstance(r0, int) and r0 < n_ctx else 16
            p_ref[pl.ds(r0 + off, rows), :] = val
        project(store)
        c_cur =1.0 - mu_ref[...]
        c_nb = 0.5 * mu_ref[...]
        for i in range(nblk):
            r0 = i * rb
            src = r0 + (16 if r0 >= n_ctx else 8)
            nb = p_ref[src - 1:src - 1 + rb, :] + p_ref[src + 1:src + 1 + rb, :]
            o_ref[r0:r0 + rb, :] = (p_ref[src:src + rb, :] * c_cur + nb * c_nb).astype(o_ref.dtype)


def proj_in(xs, mod, g, w, mu, n_ctx):
    B, S, D = xs.shape
    NP = w.shape[1]
    tn = PROJ_TN
    kern = functools.partial(_proj_kernel, n_ctx=n_ctx, rb=n_ctx, mb=min(1024, S - n_ctx), rw_j0=COL_RW // tn)
    return pl.pallas_call(
        kern,
        grid=(B, NP // tn),
        in_specs=[pl.BlockSpec((None, S, D), lambda b, j: (b, 0, 0)),
                  pl.BlockSpec((None, 12, D), lambda b, j: (b, 0, 0)),
                  pl.BlockSpec((1, D), lambda b, j: (0, 0)),
                  pl.BlockSpec((D, tn), lambda b, j: (0, j)),
                  pl.BlockSpec((1, tn), lambda b, j: (0, j))],
        out_specs=pl.BlockSpec((None, S, tn), lambda b, j: (b, 0, j)),
        out_shape=jax.ShapeDtypeStruct((B, S, NP), BF16),
        scratch_shapes=[pltpu.VMEM((S, D), BF16), pltpu.VMEM((S + 24, tn), F32)],
        compiler_params=_cparams(("arbitrary", "arbitrary")),
        name="proj_in",
    )(xs, mod, g, w, mu)


def _gelu(x):
    return 0.5 * x * (1.0 + lax.erf(x * (1.0 / math.sqrt(2.0))))


def _gmlp_kernel(p_ref, vg_ref, ws_ref, bs_ref, o_ref):
    W = vg_ref.shape[1]
    gd = W // GM_GROUPS
    for c in range(p_ref.shape[0] // CHUNK):
        rows = slice(c * CHUNK, (c + 1) * CHUNK)
        u = _gelu(p_ref[rows, :W].astype(F32))
        v = _gelu(p_ref[rows, W:].astype(F32))
        v = v * lax.rsqrt(jnp.mean(v * v, axis=-1, keepdims=True) + RMS_EPS) * vg_ref[...]
        vb = v.astype(BF16)
        for g in range(GM_GROUPS):
            sl = slice(g * gd, (g + 1) * gd)
            f = _dot(ws_ref[g], vb[:, sl]) + bs_ref[:, sl]
            o_ref[rows, sl] = (u[:, sl] * f).astype(o_ref.dtype)


def gmlp(P, vg, ws, bs_full, tt=768):
    B, S, _ = P.shape
    W = vg.shape[1]
    return pl.pallas_call(
        _gmlp_kernel,
        grid=(B, S // tt),
        in_specs=[pl.BlockSpec((None, tt, 2 * W), lambda b, i: (b, i, 0)),
                  pl.BlockSpec((1, W), lambda b, i: (0, 0)),
                  pl.BlockSpec((GM_GROUPS, CHUNK, CHUNK), lambda b, i: (0, 0, 0)),
                  pl.BlockSpec((CHUNK, W), lambda b, i: (0, 0))],
        out_specs=pl.BlockSpec((None, tt, W), lambda b, i: (b, i, 0)),
        out_shape=jax.ShapeDtypeStruct((B, S, W), BF16),
        compiler_params=_cparams(("arbitrary", "arbitrary")),
        name="gmlp",
    )(P, vg, ws, bs_full)


def _qk_kernel(q_ref, k_ref, v_ref, cos_ref, sa_ref, sb_ref, qg_ref, kg_ref, bd_ref,
               qo_ref, ko_ref, vo_ref):
    W = q_ref.shape[1]
    reps = W // LANES
    bd = bd_ref[...]
    sw = bd.shape[0]
    cos = jnp.tile(cos_ref[...], (1, sw // LANES))
    sin_a = jnp.tile(sa_ref[...], (1, sw // LANES))
    sin_b = jnp.tile(sb_ref[...], (1, sw // LANES))
    half = DA_HEAD_DIM // 2

    def prep(src, g_ref, dst, scale):
        for c0 in range(0, W, sw):
            t = src[:, c0:c0 + sw].astype(F32)
            ms = _group_sum(t * t, bd) * (1.0 / DA_HEAD_DIM)
            t = t * lax.rsqrt(ms + RMS_EPS) * g_ref[:, c0:c0 + sw]
            t = t * cos + pltpu.roll(t, sw - half, 1) * sin_a + pltpu.roll(t, half, 1) * sin_b
            dst[:, c0:c0 + sw] = (t * scale if scale != 1.0 else t).astype(dst.dtype)

    prep(q_ref, qg_ref, qo_ref, DA_HEAD_DIM ** -0.5 * math.log2(math.e))
    prep(k_ref, kg_ref, ko_ref, 1.0)
    for h in range(reps):
        vo_ref[h] = v_ref[:, h * LANES:(h + 1) * LANES].astype(F32).T.astype(BF16)


def qk_prep(P, cos, sin_a, sin_b, qg, kg, bd, tt=768):
    B, S, _ = P.shape
    W = D_MODEL
    c0 = COL_DA // W
    tok = lambda c: pl.BlockSpec((None, tt, W), lambda b, i, c=c: (b, i, c))
    tab = pl.BlockSpec((tt, LANES), lambda b, i: (i, 0))
    row = pl.BlockSpec((1, W), lambda b, i: (0, 0))
    out = pl.BlockSpec((None, tt, W), lambda b, i: (b, i, 0))
    shp = jax.ShapeDtypeStruct((B, S, W), BF16)
    return pl.pallas_call(
        _qk_kernel,
        grid=(B, S // tt),
        in_specs=[tok(c0), tok(c0 + 1), tok(c0 + 2), tab, tab, tab, row, row,
                  pl.BlockSpec((GROUP_W, GROUP_W), lambda b, i: (0, 0))],
        out_specs=[out, out, pl.BlockSpec((None, W // LANES, LANES, tt), lambda b, i: (b, 0, 0, i))],
        out_shape=[shp, shp, jax.ShapeDtypeStruct((B, W // LANES, LANES, S), BF16)],
        compiler_params=_cparams(("arbitrary", "arbitrary")),
        name="qk_prep",
    )(P, P, P, cos, sin_a, sin_b, qg, kg, bd)


def _attn_kernel(q_ref, k_ref, v_ref, lam_ref, sg_ref, o_ref, *, n_ctx, lam_init):
    i = pl.program_id(2)
    tq = q_ref.shape[0]
    lp = lam_ref[...]
    lam = (jnp.exp(jnp.sum(lp[0:1] * lp[1:2], axis=-1, keepdims=True))
           - jnp.exp(jnp.sum(lp[2:3] * lp[3:4], axis=-1, keepdims=True)) + lam_init)

    def attend(nk):
        q = q_ref[...]
        lane = lax.broadcasted_iota(jnp.int32, q.shape, 1)
        zero = jnp.zeros_like(q)
        qs = jnp.concatenate([jnp.where(lane < DA_HEAD_DIM, q, zero),
                              jnp.where(lane >= DA_HEAD_DIM, q, zero)], axis=0)
        m = l = acc = None
        nblk = nk // ATTN_KB
        blk = lambda j: slice(j * ATTN_KB, (j + 1) * ATTN_KB)
        scores = lambda j: _dot_nt(k_ref[blk(j), :], qs)
        queue = [scores(j) for j in range(min(ATTN_AHEAD, nblk))]
        for j in range(nblk):
            ks = blk(j)
            if j + ATTN_AHEAD < nblk:
                queue.append(scores(j + ATTN_AHEAD))
            s = queue.pop(0)
            m_blk = jnp.max(s, axis=0, keepdims=True)
            m_new = m_blk if m is None else jnp.maximum(m, m_blk)
            e = jnp.exp2(s - m_new)
            pv = _dot(v_ref[:, ks], e.astype(BF16))
            if m is None:
                l, acc = jnp.sum(e, axis=0, keepdims=True), pv
            else:
                alpha = jnp.exp2(m - m_new)
                l = alpha * l + jnp.sum(e, axis=0, keepdims=True)
                acc = alpha * acc + pv
            m = m_new
        inv = 1.0 / l
        o = acc[:, :tq] * inv[:, :tq] - acc[:, tq:] * (lam * inv[:, tq:])
        o = o * lax.rsqrt(jnp.mean(o * o, axis=0, keepdims=True) + RMS_EPS)
        o_ref[...] = (o.T * sg_ref[...] * (1.0 - lam_init)).astype(o_ref.dtype)

    attend(k_ref.shape[0])


def attention_ctx(qn, kn, vt, lam_p, sg, n_ctx, lam_init, tq=256):
    B, S, W = qn.shape
    H = W // LANES
    kern = functools.partial(_attn_kernel, n_ctx=n_ctx, lam_init=lam_init)
    return pl.pallas_call(
        kern,
        grid=(B, H, n_ctx // tq),
        in_specs=[pl.BlockSpec((None, tq, LANES), lambda b, h, i: (b, i, h)),
                  pl.BlockSpec((None, n_ctx, LANES), lambda b, h, i: (b, 0, h)),
                  pl.BlockSpec((None, None, LANES, n_ctx), lambda b, h, i: (b, h, 0, 0)),
                  pl.BlockSpec((4, DA_HEAD_DIM), lambda b, h, i: (0, 0)),
                  pl.BlockSpec((1, LANES), lambda b, h, i: (0, 0))],
        out_specs=pl.BlockSpec((None, tq, LANES), lambda b, h, i: (b, i, h)),
        out_shape=jax.ShapeDtypeStruct((B, n_ctx, W), BF16),
        compiler_params=_cparams(("arbitrary", "arbitrary", "arbitrary")),
        name="attention_ctx",
    )(qn, kn, vt, lam_p, sg)


def _attn_lat_kernel(q_ref, k_ref, v_ref, lam_ref, sg_ref, o_ref, s_ref, m_ref, *, lam_init, nq):
    t = pl.program_id(2)
    tq = q_ref.shape[0]
    nblk = k_ref.shape[0] // ATTN_KB
    blk = lambda j: slice(j * ATTN_KB, (j + 1) * ATTN_KB)

    def stacked_q():
        q = q_ref[...]
        lane = lax.broadcasted_iota(jnp.int32, q.shape, 1)
        zero = jnp.zeros_like(q)
        return jnp.concatenate([jnp.where(lane < DA_HEAD_DIM, q, zero),
                                jnp.where(lane >= DA_HEAD_DIM, q, zero)], axis=0)

    def run(scores_slot, softmax_slot):
        qs = stacked_q() if scores_slot is not None else None
        m_run = l = acc = None
        if softmax_slot is not None:
            m_prev = m_ref[softmax_slot]
        for j in range(nblk):
            if scores_slot is not None:
                s = _dot_nt(k_ref[blk(j), :], qs)
                s_ref[scores_slot, blk(j), :] = s
                for r in range(0, ATTN_KB, 8):
                    m_run = s[r:r + 8] if m_run is None else jnp.maximum(m_run, s[r:r + 8])
            if softmax_slot is not None:
                e = jnp.exp2(s_ref[softmax_slot, blk(j), :] - m_prev)
                pv = _dot(v_ref[:, blk(j)], e.astype(BF16))
                lsum = jnp.sum(e, axis=0, keepdims=True)
                l, acc = (lsum, pv) if l is None else (l + lsum, acc + pv)
        if scores_slot is not None:
            m_ref[scores_slot] = jnp.max(m_run, axis=0, keepdims=True)
        if softmax_slot is not None:
            lp = lam_ref[...]
            lam = (jnp.exp(jnp.sum(lp[0:1] * lp[1:2], axis=-1, keepdims=True))
                   - jnp.exp(jnp.sum(lp[2:3] * lp[3:4], axis=-1, keepdims=True)) + lam_init)
            inv = 1.0 / l
            o = acc[:, :tq] * inv[:, :tq] - acc[:, tq:] * (lam * inv[:, tq:])
            o = o * lax.rsqrt(jnp.mean(o * o, axis=0, keepdims=True) + RMS_EPS)
            o_ref[...] = (o.T * sg_ref[...] * (1.0 - lam_init)).astype(o_ref.dtype)

    @pl.when(t == 0)
    def _():
        run(0, None)

    for par in range(2):
        @pl.when((t > 0) & (t < nq) & (t % 2 == par))
        def _():
            run(par, 1 - par)

    @pl.when(t == nq)
    def _():
        run(None, (nq - 1) % 2)


def attention_lat(qn, kn, vt, lam_p, sg, n_ctx, lam_init, tq=512):
    B, S, W = qn.shape
    H = W // LANES
    nq = (S - n_ctx) // tq
    q_lat = qn[:, n_ctx:]
    kern = functools.partial(_attn_lat_kernel, lam_init=lam_init, nq=nq)
    return pl.pallas_call(
        kern,
        grid=(B, H, nq + 1),
        in_specs=[pl.BlockSpec((None, tq, LANES), lambda b, h, t: (b, jnp.minimum(t, nq - 1), h)),
                  pl.BlockSpec((None, S, LANES), lambda b, h, t: (b, 0, h)),
                  pl.BlockSpec((None, None, LANES, S), lambda b, h, t: (b, h, 0, 0)),
                  pl.BlockSpec((4, DA_HEAD_DIM), lambda b, h, t: (0, 0)),
                  pl.BlockSpec((1, LANES), lambda b, h, t: (0, 0))],
        out_specs=pl.BlockSpec((None, tq, LANES), lambda b, h, t: (b, jnp.maximum(t - 1, 0), h)),
        out_shape=jax.ShapeDtypeStruct((B, S - n_ctx, W), BF16),
        scratch_shapes=[pltpu.VMEM((2, S, 2 * tq), F32), pltpu.VMEM((2, 1, 2 * tq), F32)],
        compiler_params=_cparams(("arbitrary", "arbitrary", "arbitrary")),
        name="attention_lat",
    )(q_lat, kn, vt, lam_p, sg)


def _rw_prep_kernel(zk_ref, zt_ref, w0_ref, w2_ref, a0_ref, a2_ref, kk_ref, bd_ref,
                    kkn_ref, lw_ref, as_ref):
    tw = jnp.tanh(zt_ref[:, 0:LANES].astype(F32)).astype(BF16)
    za = zt_ref[:, LANES:2 * LANES]
    for d in range(2):
        x = w0_ref[d:d + 1, :] + _dot(tw, w2_ref[d])
        lw_ref[d] = -math.exp(-0.5) * jax.nn.sigmoid(x)
        as_ref[d] = jax.nn.sigmoid(a0_ref[d:d + 1, :] + _dot(za, a2_ref[d])).astype(as_ref.dtype)
    kk = zk_ref[...].astype(F32) * kk_ref[...]
    nrm = jnp.sqrt(_group_sum(kk * kk, bd_ref[...]))
    kkn_ref[...] = (kk / jnp.maximum(nrm, 1e-12)).astype(kkn_ref.dtype)


def rw_prep(P, w0, w2p, a0, a2p, kkp, bd, tt=768):
    B, S, _ = P.shape
    W = D_MODEL
    tokw = pl.BlockSpec((None, tt, W), lambda b, i: (b, i, COL_RW // W + 1))
    tail = pl.BlockSpec((None, tt, 512), lambda b, i: (b, i, COL_TAIL // 512))
    full = lambda shape: pl.BlockSpec(shape, lambda b, i: (0,) * len(shape))
    out2 = pl.BlockSpec((2, None, tt, W), lambda b, i: (0, b, i, 0))
    return pl.pallas_call(
        _rw_prep_kernel,
        grid=(B, S // tt),
        in_specs=[tokw, tail, full((2, W)), full((2, LANES, W)), full((2, W)), full((2, LANES, W)),
                  full((1, W)), full((GROUP_W, GROUP_W))],
        out_specs=[pl.BlockSpec((None, tt, W), lambda b, i: (b, i, 0)), out2, out2],
        out_shape=[jax.ShapeDtypeStruct((B, S, W), BF16),
                   jax.ShapeDtypeStruct((2, B, S, W), F32),
                   jax.ShapeDtypeStruct((2, B, S, W), BF16)],
        compiler_params=_cparams(("arbitrary", "arbitrary")),
        name="rw_prep",
    )(P, P, w0, w2p, a0, a2p, kkp, bd)


def _rw_scan_kernel(r_ref, zk_ref, v_ref, kkn_ref, lw_ref, as_ref, ka_ref, y_ref, h_ref):
    @pl.when(pl.program_id(1) == 0)
    def _():
        h_ref[...] = jnp.zeros_like(h_ref)

    C = RW_CHUNK
    n_sub = lw_ref[0].shape[0] // C
    for sub in range(n_sub):
        rows = (pl.ds(sub * C, C), pl.ds((n_sub - 1 - sub) * C, C))
        view = lambda pair: tuple(pair[d].at[rows[d]] for d in range(2))
        _scan_chunk(view(r_ref), view(zk_ref), view(v_ref), view(kkn_ref), view(lw_ref), view(as_ref),
                    ka_ref, view(y_ref), h_ref)


def _scan_chunk(r_ref, zk_ref, v_ref, kkn_ref, lw_ref, as_ref, ka_ref, y_ref, h_ref):
    C = RW_CHUNK
    R2 = 2 * C
    n_pairs = ka_ref.shape[1] // LANES

    ri = lax.broadcasted_iota(jnp.int32, (C, C), 0)
    ci = lax.broadcasted_iota(jnp.int32, (C, C), 1)
    rt = lax.broadcasted_iota(jnp.int32, (C, LANES), 0)
    lane = lax.broadcasted_iota(jnp.int32, (C, LANES), 1)
    cs = lane & (C - 1)
    levels = int(math.log2(C))
    same_blk = [(rt >> k) == (cs >> k) for k in range(levels + 1)]
    joins = [same_blk[k + 1] & ~same_blk[k] for k in range(levels)]
    eye = jnp.where(rt == cs, 1.0, 0.0)
    first = lane < RW_HEAD
    hr = lax.broadcasted_iota(jnp.int32, (LANES, LANES), 0)
    hc = lax.broadcasted_iota(jnp.int32, (LANES, LANES), 1)
    same_head = (hr >= RW_HEAD) == (hc >= RW_HEAD)
    ka = ka_ref[...]

    def stack(x):
        return jnp.concatenate([jnp.where(first, x, 0.0), jnp.where(first, 0.0, x)], axis=0)

    chains = []
    for d in range(2):
        before_c = (ci <= ri) if d == 0 else (ci >= ri)
        strict = (cs < rt) if d == 0 else (cs > rt)
        incl = (cs <= rt) if d == 0 else (cs >= rt)
        lw = lw_ref[d][...]
        L = _split3_dot_left(jnp.where(before_c, 1.0, 0.0).astype(BF16), lw)
        ltot = jnp.sum(lw, axis=0, keepdims=True)
        lmid = 0.5 * ltot
        e_r = jnp.exp(L - lmid)
        e_a = jnp.exp(L - lw - lmid)
        e_b = jnp.exp(lmid - L)
        e_mid = jnp.exp(lmid)
        asig = as_ref[d][...].astype(F32)
        kkn = kkn_ref[d][...].astype(F32)
        kd = zk_ref[d][...].astype(F32) * (1.0 + (asig - 1.0) * ka)
        a_in = -kkn * e_a
        r_in = r_ref[d][...].astype(F32) * e_r
        b_in = kkn * asig * e_b
        k_in = kd * e_b
        a_h = a_in * e_mid
        r_h = r_in * e_mid
        b_o = b_in * e_mid
        k_o = k_in * e_mid
        wtot = jnp.exp(ltot)
        vv = v_ref[d][...].astype(F32)
        for p in range(n_pairs):
            sl = slice(p * LANES, (p + 1) * LANES)
            chains.append(dict(
                d=d, p=p, sl=sl, strict=strict, incl=incl, wtot=wtot[:, sl],
                AR=jnp.concatenate([a_in[:, sl], r_in[:, sl]], axis=0).astype(BF16),
                BK=jnp.concatenate([stack(b_in[:, sl]), stack(k_in[:, sl])], axis=0).astype(BF16),
                ARh=jnp.concatenate([a_h[:, sl], r_h[:, sl]], axis=0).astype(BF16),
                BKo=jnp.concatenate([b_o[:, sl], k_o[:, sl]], axis=0).astype(BF16),
                V=vv[:, sl]))

    for ch in chains:
        G1 = _dot_nt(ch["AR"], ch["BK"])
        ch["Aab"] = jnp.where(ch["strict"], G1[:C, :R2], 0.0)
        ch["Aak"] = jnp.where(ch["strict"], G1[:C, R2:], 0.0).astype(BF16)
        ch["Ar"] = jnp.concatenate([jnp.where(ch["incl"], G1[C:, :R2], 0.0),
                                    jnp.where(ch["incl"], G1[C:, R2:], 0.0)], axis=1).astype(BF16)
        ch["H"] = h_ref[ch["d"], ch["p"]]
        ch["T"] = eye + jnp.where(joins[0], ch["Aab"], 0.0)
        ch["As"] = stack(ch["Aab"]).astype(BF16)
    for ch in chains:
        ch["G2"] = _dot_nt(ch["ARh"], ch["H"].astype(BF16))
        ch["Vs"] = stack(ch["V"]).astype(BF16)
        ch["rhs"] = ch["G2"][:C] + _dot(ch["Aak"], ch["Vs"])
    for join in joins[1:]:
        for ch in chains:
            ch["TA"] = _dot(ch["T"].astype(BF16), ch["As"]).astype(BF16)
        for ch in chains:
            ch["T"] = ch["T"] + jnp.where(join, _dot(ch["TA"], stack(ch["T"]).astype(BF16)), 0.0)
    for ch in chains:
        ch["U"] = _dot(ch["T"].astype(BF16), stack(ch["rhs"]).astype(BF16))
    for ch in chains:
        UVs = jnp.concatenate([stack(ch["U"]).astype(BF16), ch["Vs"]], axis=0)
        y_ref[ch["d"]][:, ch["sl"]] = ch["G2"][C:] + _dot(ch["Ar"], UVs)
    for ch in chains:
        UVt = jnp.concatenate([ch["U"], ch["V"]], axis=0).T.astype(BF16)
        upd = jnp.where(same_head, _dot(UVt, ch["BKo"]), 0.0)
        h_ref[ch["d"], ch["p"]] = ch["H"] * ch["wtot"] + upd


def rw_scan(P, kkn, lw, asig, ka, n_ctx):
    B, S, _ = P.shape
    W = D_MODEL
    C = RW_CHUNK * RW_SUB
    nc = S // C
    ncc = n_ctx // C

    def rev(c):
        return jnp.where(c < ncc, ncc - 1 - c, nc - 1 + ncc - c)

    def both(make):
        return [make(lambda c: c), make(rev)]

    pcol = lambda col: both(lambda f: pl.BlockSpec((None, C, W), lambda b, c, f=f: (b, f(c), col)))
    tok = both(lambda f: pl.BlockSpec((None, C, W), lambda b, c, f=f: (b, f(c), 0)))
    dtok = [pl.BlockSpec((None, None, C, W), lambda b, c: (0, b, c, 0)),
            pl.BlockSpec((None, None, C, W), lambda b, c: (1, b, rev(c), 0))]
    c0 = COL_RW // W

    def kern(rf, rb_, kf, kb_, vf, vb_, nf, nb_, lwf, lwb, asf, asb, ka_ref, yf, yb, h_ref):
        _rw_scan_kernel((rf, rb_), (kf, kb_), (vf, vb_), (nf, nb_), (lwf, lwb), (asf, asb), ka_ref,
                        (yf, yb), h_ref)

    return pl.pallas_call(
        kern,
        grid=(B, nc),
        in_specs=[*pcol(c0), *pcol(c0 + 1), *pcol(c0 + 2), *tok, *dtok, *dtok,
                  pl.BlockSpec((1, W), lambda b, c: (0, 0))],
        out_specs=[pl.BlockSpec((None, C, W), lambda b, c: (b, c, 0)),
                   pl.BlockSpec((None, C, W), lambda b, c: (b, rev(c), 0))],
        out_shape=[jax.ShapeDtypeStruct((B, S, W), F32)] * 2,
        scratch_shapes=[pltpu.VMEM((2, W // LANES, LANES, LANES), F32)],
        compiler_params=_cparams(("arbitrary", "arbitrary")),
        name="rw_scan",
    )(P, P, P, P, P, P, kkn, kkn, lw, lw, asig, asig, ka)


def _merge_kernel(x_ref, m_ref, yf_ref, yb_ref, as_ref, r_ref, zk_ref, v_ref, zt_ref, ga_ref, gb_ref, gc_ref,
                  a_ref, bc_ref, bl_ref, lnw_ref, lnb_ref, rk_ref, ka_ref, g2_ref, bd_ref,
                  wa_ref, wb_ref, wc_ref, wo_ref, o_ref, *, n_ctx):
    tt = x_ref.shape[0]
    row0 = pl.program_id(1) * tt
    bd = bd_ref[...]
    inv_n = 1.0 / RW_HEAD
    y = yf_ref[...] + yb_ref[...]
    mean = _group_sum(y, bd) * inv_n
    yc = y - mean
    var = _group_sum(yc * yc, bd) * inv_n
    yn = yc * lax.rsqrt(var + GN_EPS) * lnw_ref[...] + lnb_ref[...]
    zk = zk_ref[...].astype(F32)
    ka = ka_ref[...]
    kd_sum = (zk * (1.0 + (as_ref[0].astype(F32) - 1.0) * ka)
              + zk * (1.0 + (as_ref[1].astype(F32) - 1.0) * ka))
    vv = v_ref[...].astype(F32)
    bonus = _group_sum(r_ref[...].astype(F32) * kd_sum * rk_ref[...], bd) * vv
    g = _dot(jax.nn.sigmoid(zt_ref[:, 2 * LANES:].astype(F32)).astype(BF16), g2_ref[...])
    c_br = ((yn + bonus) * g).astype(BF16)
    gate = lambda ref: jax.nn.sigmoid(ref[...].astype(F32))
    m = (gate(ga_ref) * _dot(a_ref[...], wa_ref[...])
         + gate(gb_ref) * _dot(jnp.where(row0 < n_ctx, bc_ref[...], bl_ref[...]), wb_ref[...])
         + gate(gc_ref) * _dot(c_br, wc_ref[...]))
    out = _dot(m.astype(BF16), wo_ref[...])
    o_ref[...] = x_ref[...] + _gate_rows(m_ref, 2, row0, tt, n_ctx) * out


def merge(xs, mod, y, asig, P, a_br, b_ctx, b_lat, lnw, lnb, rk, ka, g2p, bd, wa, wb, wc, wo, n_ctx, tt=256):
    B, S, W = xs.shape
    tok = pl.BlockSpec((None, tt, W), lambda b, i: (b, i, 0))
    pcol = lambda col: pl.BlockSpec((None, tt, W), lambda b, i, col=col: (b, i, col))
    dtok = pl.BlockSpec((2, None, tt, W), lambda b, i: (0, b, i, 0))
    row = pl.BlockSpec((1, W), lambda b, i: (0, 0))
    wsq = pl.BlockSpec((W, W), lambda b, i: (0, 0))
    cr, cg = COL_RW // W, COL_GATE // W
    nct = n_ctx // tt
    kern = functools.partial(_merge_kernel, n_ctx=n_ctx)
    return pl.pallas_call(
        kern,
        grid=(B, S // tt),
        in_specs=[tok, pl.BlockSpec((None, 12, W), lambda b, i: (b, 0, 0)), tok, tok, dtok,
                  pcol(cr), pcol(cr + 1), pcol(cr + 2),
                  pl.BlockSpec((None, tt, 512), lambda b, i: (b, i, COL_TAIL // 512)),
                  pcol(cg), pcol(cg + 1), pcol(cg + 2), tok,
                  pl.BlockSpec((None, tt, W), lambda b, i: (b, jnp.minimum(i, nct - 1), 0)),
                  pl.BlockSpec((None, tt, W), lambda b, i: (b, jnp.maximum(i - nct, 0), 0)),
                  row, row, row, row,
                  pl.BlockSpec((2 * LANES, W), lambda b, i: (0, 0)),
                  pl.BlockSpec((GROUP_W, GROUP_W), lambda b, i: (0, 0)),
                  wsq, wsq, wsq, wsq],
        out_specs=tok,
        out_shape=jax.ShapeDtypeStruct((B, S, W), F32),
        compiler_params=_cparams(("arbitrary", "arbitrary")),
        name="merge",
    )(xs, mod, y[0], y[1], asig, P, P, P, P, P, P, P, a_br, b_ctx, b_lat, lnw, lnb, rk, ka, g2p, bd, wa, wb, wc, wo)


def _ffn_kernel(x_ref, m_ref, g_ref, wi_ref, wo_ref, o_ref, *, n_ctx, fc):
    tm = x_ref.shape[0]
    row0 = pl.program_id(1) * tm
    FF = wo_ref.shape[0]
    x = x_ref[...]
    h = _modulated_norm(x, g_ref[...], m_ref, 3, row0, n_ctx).astype(BF16)

    def gate_up(k):
        return (_dot(h, wi_ref[:, k * fc:(k + 1) * fc]), _dot(h, wi_ref[:, FF + k * fc:FF + (k + 1) * fc]))

    nxt = gate_up(0)
    acc = None
    for k in range(FF // fc):
        gt, up = nxt
        if (k + 1) * fc < FF:
            nxt = gate_up(k + 1)
        act = (gt * jax.nn.sigmoid(gt) * up).astype(BF16)
        part = _dot(act, wo_ref[k * fc:(k + 1) * fc, :])
        acc = part if acc is None else acc + part
    o_ref[...] = x + _gate_rows(m_ref, 5, row0, tm, n_ctx) * acc


def ffn(xs, mod, g, wi, wo, n_ctx, tm=384, fc=256):
    B, S, W = xs.shape
    FF = wo.shape[0]
    kern = functools.partial(_ffn_kernel, n_ctx=n_ctx, fc=fc)
    tok = pl.BlockSpec((None, tm, W), lambda b, i: (b, i, 0))
    return pl.pallas_call(
        kern,
        grid=(B, S // tm),
        in_specs=[tok, pl.BlockSpec((None, 12, W), lambda b, i: (b, 0, 0)),
                  pl.BlockSpec((1, W), lambda b, i: (0, 0)),
                  pl.BlockSpec((W, 2 * FF), lambda b, i: (0, 0)),
                  pl.BlockSpec((FF, W), lambda b, i: (0, 0))],
        out_specs=tok,
        out_shape=jax.ShapeDtypeStruct((B, S, W), F32),
        compiler_params=_cparams(("arbitrary", "arbitrary")),
        name="ffn",
    )(xs, mod, g, wi, wo)


def _rope_tables(n_rows, n_ctx):
    row = jnp.repeat(jnp.arange(n_rows), GRID_W).astype(F32)
    col = jnp.tile(jnp.arange(GRID_W), n_rows).astype(F32)
    axis_dim = DA_HEAD_DIM // 2
    inv_freq = ROPE_BASE ** (-jnp.arange(0, axis_dim, 2, dtype=F32) / axis_dim)
    ang = jnp.concatenate([row[:, None] * inv_freq, col[:, None] * inv_freq], axis=-1)
    ang = jnp.concatenate([ang, ang], axis=-1)
    cos, sin = jnp.cos(ang), jnp.sin(ang)
    cos = jnp.concatenate([jnp.ones((n_ctx, DA_HEAD_DIM), F32), cos], axis=0)
    sin = jnp.concatenate([jnp.zeros((n_ctx, DA_HEAD_DIM), F32), sin], axis=0)
    first = jnp.arange(DA_HEAD_DIM) < DA_HEAD_DIM // 2
    sin_a = jnp.where(first, -sin, 0.0)
    sin_b = jnp.where(first, 0.0, sin)
    dup = lambda t: jnp.concatenate([t, t], axis=-1)
    return dup(cos), dup(sin_a), dup(sin_b)


def kernel(x, c, ctx, c_ctx, ada_w, ada_b, norm1_g, norm2_g, w_in, gm_v_g, gm_ws, gm_bs,
           da_q_g, da_k_g, da_lambda, da_subln_g, rw_mu, rw_w0, rw_w2, rw_a0, rw_a2, rw_g2,
           rw_kk, rw_ka, rw_rk, rw_ln_w, rw_ln_b, w_br_a, w_br_b, w_br_c, w_o, ffn_wi, ffn_wo):
    B, T, D = x.shape
    n_ctx = ctx.shape[1]
    L = ada_w.shape[0]
    cos, sin_a, sin_b = _rope_tables(T // GRID_W, n_ctx)
    xs = jnp.concatenate([ctx, x], axis=1)

    cc = jnp.zeros((16, D), F32).at[:B].set(c).at[B].set(c_ctx)
    mods = ada_mod(cc, ada_w, ada_b)
    mod_l = mods[:, :B].reshape(L, B, 6, 1, D)
    mod_c = jnp.broadcast_to(mods[:, B].reshape(L, 1, 6, 1, D), (L, B, 6, 1, D))
    mod12 = jnp.concatenate([mod_c, mod_l], axis=3).reshape(L, B, 12, D)

    gi = jnp.arange(GROUP_W) // RW_HEAD
    bd = (gi[:, None] == gi[None, :]).astype(BF16)
    row = lambda t: t.reshape(1, -1)
    g0, d0, r0, t0 = 0, 2048, 5120, 5120 + 3072
    k0 = r0 + 3488

    for l in range(L):
        lam_init = 0.8 - 0.6 * math.exp(-0.3 * l)
        wl = w_in[l]
        w_cat = jnp.concatenate([wl[:, a:b].astype(BF16) for a, b in
                                 ((g0, d0), (d0, r0), (k0, wl.shape[1]), (r0, t0), (t0, k0))]
                                + [jnp.zeros((D, 96), BF16)], axis=1)
        mu = jnp.concatenate([jnp.zeros((COL_RW,), F32), rw_mu[l], jnp.zeros((96,), F32)]).reshape(1, -1)
        P = proj_in(xs, mod12[l], row(norm1_g[l]), w_cat, mu, n_ctx)

        bs_full = jnp.repeat(gm_bs[l].T, CHUNK, axis=1)
        a_br = gmlp(P, row(gm_v_g[l]), gm_ws[l].astype(BF16), bs_full)

        tile = lambda t, n: jnp.tile(t, n).reshape(1, -1)
        qn, kn, vb = qk_prep(P, cos, sin_a, sin_b, tile(da_q_g[l], 16), tile(da_k_g[l], 16), bd)
        b_ctx = attention_ctx(qn, kn, vb, da_lambda[l], row(da_subln_g[l]), n_ctx, lam_init)
        b_lat = attention_lat(qn, kn, vb, da_lambda[l], row(da_subln_g[l]), n_ctx, lam_init)

        zpad = jnp.zeros((DECAY_LORA, D), F32)
        w2p = jnp.stack([jnp.concatenate([rw_w2[l, 0], zpad]), jnp.concatenate([zpad, rw_w2[l, 1]])]).astype(BF16)
        a2p = jnp.stack([jnp.concatenate([rw_a2[l, 0], zpad]), jnp.concatenate([zpad, rw_a2[l, 1]])]).astype(BF16)
        kkn, lw, asig = rw_prep(P, rw_w0[l], w2p, rw_a0[l], a2p, row(rw_kk[l]), bd)
        y = rw_scan(P, kkn, lw, asig, row(rw_ka[l]), n_ctx)

        g2p = jnp.concatenate([rw_g2[l], jnp.zeros((2 * LANES - GATE_LORA, D), F32)]).astype(BF16)
        xs = merge(xs, mod12[l], y, asig, P, a_br, b_ctx, b_lat, row(rw_ln_w[l]), row(rw_ln_b[l]),
                   row(rw_rk[l]), row(rw_ka[l]), g2p, bd,
                   w_br_a[l].astype(BF16), w_br_b[l].astype(BF16), w_br_c[l].astype(BF16),
                   w_o[l].astype(BF16), n_ctx)
        xs = ffn(xs, mod12[l], row(norm2_g[l]), ffn_wi[l].astype(BF16), ffn_wo[l].astype(BF16), n_ctx)
    return xs[:, n_ctx:]
```

```python
import functools
import math

import jax
import jax.numpy as jnp
from jax import lax
from jax.experimental import pallas as pl
from jax.experimental.pallas import tpu as pltpu

D_MODEL = 1024
GRID_W = 64
CHUNK = 128
GM_GROUPS = 8
DA_HEADS = 8
DA_HEAD_DIM = 64
DA_V_DIM = 128
ROPE_BASE = 10000.0
RW_HEAD = 64
RW_HEADS = 16
DECAY_LORA = 64
AAA_LORA = 64
GATE_LORA = 160
D_FF = 2816
RMS_EPS = 1e-6
GN_EPS = 64e-5

COL_GM = 0
COL_DA = 2048
COL_GATE = 5120
COL_RW = 8192
COL_TAIL = 11264
NP_COLS = 11776
PROJ_TN = 512
RW_CHUNK = 64
RW_SUB = 4
ATTN_KB = 256
ATTN_AHEAD = 3
LANES = 128
GROUP_W = 256
VMEM_LIMIT = 56 * 1024 * 1024

BF16 = jnp.bfloat16
F32 = jnp.float32


def _dot(a, b):
    return jnp.dot(a, b, preferred_element_type=F32)


def _dot_nt(a, b):
    return lax.dot_general(a, b, (((1,), (1,)), ((), ())), preferred_element_type=F32)


def _split_dot(x, m):
    hi = x.astype(BF16)
    lo = (x - hi.astype(F32)).astype(BF16)
    return _dot(hi, m) + _dot(lo, m)


def _split3_dot_left(m, x):
    hi = x.astype(BF16)
    r1 = x - hi.astype(F32)
    mid = r1.astype(BF16)
    lo = (r1 - mid.astype(F32)).astype(BF16)
    return _dot(m, hi) + _dot(m, mid) + _dot(m, lo)


def _group_sum(x, bd):
    width = bd.shape[0]
    parts = [_split_dot(x[:, i:i + width], bd) for i in range(0, x.shape[1], width)]
    return parts[0] if len(parts) == 1 else jnp.concatenate(parts, axis=1)


def _cparams(sem):
    return pltpu.CompilerParams(dimension_semantics=sem, vmem_limit_bytes=VMEM_LIMIT)


def _ada_kernel(c_ref, w_ref, b_ref, o_ref):
    c = c_ref[...]
    s = c * jax.nn.sigmoid(c)
    o_ref[...] = jnp.dot(s, w_ref[...], precision=lax.Precision.HIGHEST,
                         preferred_element_type=F32) + b_ref[...]


def ada_mod(cc, ada_w, ada_b):
    L, D, N = ada_w.shape
    tn = 1536
    return pl.pallas_call(
        _ada_kernel,
        grid=(L, N // tn),
        in_specs=[pl.BlockSpec((16, D), lambda l, j: (0, 0)),
                  pl.BlockSpec((None, D, tn), lambda l, j: (l, 0, j)),
                  pl.BlockSpec((None, 1, tn), lambda l, j: (l, 0, j))],
        out_specs=pl.BlockSpec((None, 16, tn), lambda l, j: (l, 0, j)),
        out_shape=jax.ShapeDtypeStruct((L, 16, N), F32),
        compiler_params=_cparams(("arbitrary", "arbitrary")),
        name="ada_mod",
    )(cc, ada_w, ada_b.reshape(L, 1, N))


def _modulated_norm(x, g, m_ref, piece, row0, n_ctx):
    rows = row0 + lax.broadcasted_iota(jnp.int32, (x.shape[0], 1), 0)
    is_lat = rows >= n_ctx
    sh = jnp.where(is_lat, m_ref[2 * piece + 1:2 * piece + 2, :], m_ref[2 * piece:2 * piece + 1, :])
    sc = jnp.where(is_lat, m_ref[2 * piece + 3:2 * piece + 4, :], m_ref[2 * piece + 2:2 * piece + 3, :])
    y = x * lax.rsqrt(jnp.mean(x * x, axis=-1, keepdims=True) + RMS_EPS) * g
    return y * (1.0 + sc) + sh


def _gate_rows(m_ref, piece, row0, nrows, n_ctx):
    rows = row0 + lax.broadcasted_iota(jnp.int32, (nrows, 1), 0)
    return jnp.where(rows >= n_ctx, m_ref[2 * piece + 1:2 * piece + 2, :], m_ref[2 * piece:2 * piece + 1, :])


def _proj_kernel(x_ref, m_ref, g_ref, w_ref, mu_ref, o_ref, h_ref, p_ref, *, n_ctx, rb, mb, rw_j0):
    S = x_ref.shape[0]
    j = pl.program_id(1)
    nblk = S // rb

    @pl.when(j == 0)
    def _():
        def body(i, carry):
            r0 = pl.multiple_of(i * rb, rb)
            x = x_ref[pl.ds(r0, rb), :]
            h_ref[pl.ds(r0, rb), :] = _modulated_norm(x, g_ref[...], m_ref, 0, r0, n_ctx).astype(BF16)
            return carry
        lax.fori_loop(0, nblk, body, 0)

    def project(store):
        store(0, n_ctx, _dot(h_ref[0:n_ctx, :], w_ref[...]))

        def body(i, carry):
            r0 = pl.multiple_of(n_ctx + i * mb, math.gcd(n_ctx, mb))
            store(r0, mb, _dot(h_ref[pl.ds(r0, mb), :], w_ref[...]))
            return carry
        lax.fori_loop(0, (S - n_ctx) // mb, body, 0)

    @pl.when(j < rw_j0)
    def _():
        def store(r0, rows, val):
            o_ref[pl.ds(r0, rows), :] = val.astype(o_ref.dtype)
        project(store)

    @pl.when(j >= rw_j0)
    def _():
        zero = jnp.zeros((8, p_ref.shape[1]), F32)
        p_ref[0:8, :] = zero
        p_ref[n_ctx + 8:n_ctx + 16, :] = zero
        p_ref[S + 16:S + 24, :] = zero

        def store(r0, rows, val):
            off = 8 if isinstance(r0, int) and r0 < n_ctx else 16
            p_ref[pl.ds(r0 + off, rows), :] = val
        project(store)
        c_cur =1.0 - mu_ref[...]
        c_nb = 0.5 * mu_ref[...]
        for i in range(nblk):
            r0 = i * rb
            src = r0 + (16 if r0 >= n_ctx else 8)
            nb = p_ref[src - 1:src - 1 + rb, :] + p_ref[src + 1:src + 1 + rb, :]
            o_ref[r0:r0 + rb, :] = (p_ref[src:src + rb, :] * c_cur + nb * c_nb).astype(o_ref.dtype)


def proj_in(xs, mod, g, w, mu, n_ctx):
    B, S, D = xs.shape
    NP = w.shape[1]
    tn = PROJ_TN
    kern = functools.partial(_proj_kernel, n_ctx=n_ctx, rb=n_ctx, mb=min(1024, S - n_ctx), rw_j0=COL_RW // tn)
    return pl.pallas_call(
        kern,
        grid=(B, NP // tn),
        in_specs=[pl.BlockSpec((None, S, D), lambda b, j: (b, 0, 0)),
                  pl.BlockSpec((None, 12, D), lambda b, j: (b, 0, 0)),
                  pl.BlockSpec((1, D), lambda b, j: (0, 0)),
                  pl.BlockSpec((D, tn), lambda b, j: (0, j)),
                  pl.BlockSpec((1, tn), lambda b, j: (0, j))],
        out_specs=pl.BlockSpec((None, S, tn), lambda b, j: (b, 0, j)),
        out_shape=jax.ShapeDtypeStruct((B, S, NP), BF16),
        scratch_shapes=[pltpu.VMEM((S, D), BF16), pltpu.VMEM((S + 24, tn), F32)],
        compiler_params=_cparams(("arbitrary", "arbitrary")),
        name="proj_in",
    )(xs, mod, g, w, mu)


def _gelu(x):
    return 0.5 * x * (1.0 + lax.erf(x * (1.0 / math.sqrt(2.0))))


def _gmlp_kernel(p_ref, vg_ref, ws_ref, bs_ref, o_ref):
    W = vg_ref.shape[1]
    gd = W // GM_GROUPS
    for c in range(p_ref.shape[0] // CHUNK):
        rows = slice(c * CHUNK, (c + 1) * CHUNK)
        u = _gelu(p_ref[rows, :W].astype(F32))
        v = _gelu(p_ref[rows, W:].astype(F32))
        v = v * lax.rsqrt(jnp.mean(v * v, axis=-1, keepdims=True) + RMS_EPS) * vg_ref[...]
        vb = v.astype(BF16)
        for g in range(GM_GROUPS):
            sl = slice(g * gd, (g + 1) * gd)
            f = _dot(ws_ref[g], vb[:, sl]) + bs_ref[:, sl]
            o_ref[rows, sl] = (u[:, sl] * f).astype(o_ref.dtype)


def gmlp(P, vg, ws, bs_full, tt=768):
    B, S, _ = P.shape
    W = vg.shape[1]
    return pl.pallas_call(
        _gmlp_kernel,
        grid=(B, S // tt),
        in_specs=[pl.BlockSpec((None, tt, 2 * W), lambda b, i: (b, i, 0)),
                  pl.BlockSpec((1, W), lambda b, i: (0, 0)),
                  pl.BlockSpec((GM_GROUPS, CHUNK, CHUNK), lambda b, i: (0, 0, 0)),
                  pl.BlockSpec((CHUNK, W), lambda b, i: (0, 0))],
        out_specs=pl.BlockSpec((None, tt, W), lambda b, i: (b, i, 0)),
        out_shape=jax.ShapeDtypeStruct((B, S, W), BF16),
        compiler_params=_cparams(("arbitrary", "arbitrary")),
        name="gmlp",
    )(P, vg, ws, bs_full)


def _qk_kernel(q_ref, k_ref, v_ref, cos_ref, sa_ref, sb_ref, qg_ref, kg_ref, bd_ref,
               qo_ref, ko_ref, vo_ref):
    W = q_ref.shape[1]
    reps = W // LANES
    bd = bd_ref[...]
    sw = bd.shape[0]
    cos = jnp.tile(cos_ref[...], (1, sw // LANES))
    sin_a = jnp.tile(sa_ref[...], (1, sw // LANES))
    sin_b = jnp.tile(sb_ref[...], (1, sw // LANES))
    half = DA_HEAD_DIM // 2

    def prep(src, g_ref, dst, scale):
        for c0 in range(0, W, sw):
            t = src[:, c0:c0 + sw].astype(F32)
            ms = _group_sum(t * t, bd) * (1.0 / DA_HEAD_DIM)
            t = t * lax.rsqrt(ms + RMS_EPS) * g_ref[:, c0:c0 + sw]
            t = t * cos + pltpu.roll(t, sw - half, 1) * sin_a + pltpu.roll(t, half, 1) * sin_b
            dst[:, c0:c0 + sw] = (t * scale if scale != 1.0 else t).astype(dst.dtype)

    prep(q_ref, qg_ref, qo_ref, DA_HEAD_DIM ** -0.5 * math.log2(math.e))
    prep(k_ref, kg_ref, ko_ref, 1.0)
    for h in range(reps):
        vo_ref[h] = v_ref[:, h * LANES:(h + 1) * LANES].astype(F32).T.astype(BF16)


def qk_prep(P, cos, sin_a, sin_b, qg, kg, bd, tt=768):
    B, S, _ = P.shape
    W = D_MODEL
    c0 = COL_DA // W
    tok = lambda c: pl.BlockSpec((None, tt, W), lambda b, i, c=c: (b, i, c))
    tab = pl.BlockSpec((tt, LANES), lambda b, i: (i, 0))
    row = pl.BlockSpec((1, W), lambda b, i: (0, 0))
    out = pl.BlockSpec((None, tt, W), lambda b, i: (b, i, 0))
    shp = jax.ShapeDtypeStruct((B, S, W), BF16)
    return pl.pallas_call(
        _qk_kernel,
        grid=(B, S // tt),
        in_specs=[tok(c0), tok(c0 + 1), tok(c0 + 2), tab, tab, tab, row, row,
                  pl.BlockSpec((GROUP_W, GROUP_W), lambda b, i: (0, 0))],
        out_specs=[out, out, pl.BlockSpec((None, W // LANES, LANES, tt), lambda b, i: (b, 0, 0, i))],
        out_shape=[shp, shp, jax.ShapeDtypeStruct((B, W // LANES, LANES, S), BF16)],
        compiler_params=_cparams(("arbitrary", "arbitrary")),
        name="qk_prep",
    )(P, P, P, cos, sin_a, sin_b, qg, kg, bd)


def _attn_kernel(q_ref, k_ref, v_ref, lam_ref, sg_ref, o_ref, *, n_ctx, lam_init):
    i = pl.program_id(2)
    tq = q_ref.shape[0]
    lp = lam_ref[...]
    lam = (jnp.exp(jnp.sum(lp[0:1] * lp[1:2], axis=-1, keepdims=True))
           - jnp.exp(jnp.sum(lp[2:3] * lp[3:4], axis=-1, keepdims=True)) + lam_init)

    def attend(nk):
        q = q_ref[...]
        lane = lax.broadcasted_iota(jnp.int32, q.shape, 1)
        zero = jnp.zeros_like(q)
        qs = jnp.concatenate([jnp.where(lane < DA_HEAD_DIM, q, zero),
                              jnp.where(lane >= DA_HEAD_DIM, q, zero)], axis=0)
        m = l = acc = None
        nblk = nk // ATTN_KB
        blk = lambda j: slice(j * ATTN_KB, (j + 1) * ATTN_KB)
        scores = lambda j: _dot_nt(k_ref[blk(j), :], qs)
        queue = [scores(j) for j in range(min(ATTN_AHEAD, nblk))]
        for j in range(nblk):
            ks = blk(j)
            if j + ATTN_AHEAD < nblk:
                queue.append(scores(j + ATTN_AHEAD))
            s = queue.pop(0)
            m_blk = jnp.max(s, axis=0, keepdims=True)
            m_new = m_blk if m is None else jnp.maximum(m, m_blk)
            e = jnp.exp2(s - m_new)
            pv = _dot(v_ref[:, ks], e.astype(BF16))
            if m is None:
                l, acc = jnp.sum(e, axis=0, keepdims=True), pv
            else:
                alpha = jnp.exp2(m - m_new)
                l = alpha * l + jnp.sum(e, axis=0, keepdims=True)
                acc = alpha * acc + pv
            m = m_new
        inv = 1.0 / l
        o = acc[:, :tq] * inv[:, :tq] - acc[:, tq:] * (lam * inv[:, tq:])
        o = o * lax.rsqrt(jnp.mean(o * o, axis=0, keepdims=True) + RMS_EPS)
        o_ref[...] = (o.T * sg_ref[...] * (1.0 - lam_init)).astype(o_ref.dtype)

    attend(k_ref.shape[0])


def attention_ctx(qn, kn, vt, lam_p, sg, n_ctx, lam_init, tq=256):
    B, S, W = qn.shape
    H = W // LANES
    kern = functools.partial(_attn_kernel, n_ctx=n_ctx, lam_init=lam_init)
    return pl.pallas_call(
        kern,
        grid=(B, H, n_ctx // tq),
        in_specs=[pl.BlockSpec((None, tq, LANES), lambda b, h, i: (b, i, h)),
                  pl.BlockSpec((None, n_ctx, LANES), lambda b, h, i: (b, 0, h)),
                  pl.BlockSpec((None, None, LANES, n_ctx), lambda b, h, i: (b, h, 0, 0)),
                  pl.BlockSpec((4, DA_HEAD_DIM), lambda b, h, i: (0, 0)),
                  pl.BlockSpec((1, LANES), lambda b, h, i: (0, 0))],
        out_specs=pl.BlockSpec((None, tq, LANES), lambda b, h, i: (b, i, h)),
        out_shape=jax.ShapeDtypeStruct((B, n_ctx, W), BF16),
        compiler_params=_cparams(("arbitrary", "arbitrary", "arbitrary")),
        name="attention_ctx",
    )(qn, kn, vt, lam_p, sg)


def _attn_lat_kernel(q_ref, k_ref, v_ref, lam_ref, sg_ref, o_ref, s_ref, m_ref, *, lam_init, n_tiles):
    t = pl.program_id(0)
    tq = q_ref.shape[0]
    nblk = k_ref.shape[0] // ATTN_KB
    blk = lambda j: slice(j * ATTN_KB, (j + 1) * ATTN_KB)

    def stacked_q():
        q = q_ref[...]
        lane = lax.broadcasted_iota(jnp.int32, q.shape, 1)
        zero = jnp.zeros_like(q)
        return jnp.concatenate([jnp.where(lane < DA_HEAD_DIM, q, zero),
                                jnp.where(lane >= DA_HEAD_DIM, q, zero)], axis=0)

    def run(scores_slot, softmax_slot):
        qs = stacked_q() if scores_slot is not None else None
        m_run = l = acc = None
        if softmax_slot is not None:
            m_prev = m_ref[softmax_slot]
        for j in range(nblk):
            if scores_slot is not None:
                s = _dot_nt(k_ref[blk(j), :], qs)
                s_ref[scores_slot, blk(j), :] = s
                for r in range(0, ATTN_KB, 8):
                    m_run = s[r:r + 8] if m_run is None else jnp.maximum(m_run, s[r:r + 8])
            if softmax_slot is not None:
                e = jnp.exp2(s_ref[softmax_slot, blk(j), :] - m_prev)
                pv = _dot(v_ref[:, blk(j)], e.astype(BF16))
                lsum = jnp.sum(e, axis=0, keepdims=True)
                l, acc = (lsum, pv) if l is None else (l + lsum, acc + pv)
        if scores_slot is not None:
            m_ref[scores_slot] = jnp.max(m_run, axis=0, keepdims=True)
        if softmax_slot is not None:
            lp = lam_ref[...]
            lam = (jnp.exp(jnp.sum(lp[0:1] * lp[1:2], axis=-1, keepdims=True))
                   - jnp.exp(jnp.sum(lp[2:3] * lp[3:4], axis=-1, keepdims=True)) + lam_init)
            inv = 1.0 / l
            o = acc[:, :tq] * inv[:, :tq] - acc[:, tq:] * (lam * inv[:, tq:])
            o = o * lax.rsqrt(jnp.mean(o * o, axis=0, keepdims=True) + RMS_EPS)
            o_ref[...] = (o.T * sg_ref[...] * (1.0 - lam_init)).astype(o_ref.dtype)

    @pl.when(t == 0)
    def _():
        run(0, None)

    for par in range(2):
        @pl.when((t > 0) & (t < n_tiles) & (t % 2 == par))
        def _():
            run(par, 1 - par)

    @pl.when(t == n_tiles)
    def _():
        run(None, (n_tiles - 1) % 2)


def attention_lat(qn, kn, vt, lam_p, sg, n_ctx, lam_init, tq=512):
    B, S, W = qn.shape
    H = W // LANES
    nq = (S - n_ctx) // tq
    n_tiles = B * H * nq
    q_lat = qn[:, n_ctx:]
    kern = functools.partial(_attn_lat_kernel, lam_init=lam_init, n_tiles=n_tiles)

    def tile(n):
        return n // (H * nq), n % nq, (n // nq) % H

    def scored(t):
        return tile(jnp.minimum(t, n_tiles - 1))

    def finished(t):
        return tile(jnp.maximum(t - 1, 0))

    return pl.pallas_call(
        kern,
        grid=(n_tiles + 1,),
        in_specs=[pl.BlockSpec((None, tq, LANES), lambda t: scored(t)),
                  pl.BlockSpec((None, S, LANES), lambda t: (scored(t)[0], 0, scored(t)[2])),
                  pl.BlockSpec((None, None, LANES, S), lambda t: (finished(t)[0], finished(t)[2], 0, 0)),
                  pl.BlockSpec((4, DA_HEAD_DIM), lambda t: (0, 0)),
                  pl.BlockSpec((1, LANES), lambda t: (0, 0))],
        out_specs=pl.BlockSpec((None, tq, LANES), lambda t: finished(t)),
        out_shape=jax.ShapeDtypeStruct((B, S - n_ctx, W), BF16),
        scratch_shapes=[pltpu.VMEM((2, S, 2 * tq), F32), pltpu.VMEM((2, 1, 2 * tq), F32)],
        compiler_params=_cparams(("arbitrary",)),
        name="attention_lat",
    )(q_lat, kn, vt, lam_p, sg)


def _rw_prep_kernel(zk_ref, zt_ref, w0_ref, w2_ref, a0_ref, a2_ref, kk_ref, bd_ref,
                    kkn_ref, lw_ref, as_ref):
    tw = jnp.tanh(zt_ref[:, 0:LANES].astype(F32)).astype(BF16)
    za = zt_ref[:, LANES:2 * LANES]
    for d in range(2):
        x = w0_ref[d:d + 1, :] + _dot(tw, w2_ref[d])
        lw_ref[d] = -math.exp(-0.5) * jax.nn.sigmoid(x)
        as_ref[d] = jax.nn.sigmoid(a0_ref[d:d + 1, :] + _dot(za, a2_ref[d])).astype(as_ref.dtype)
    kk = zk_ref[...].astype(F32) * kk_ref[...]
    nrm = jnp.sqrt(_group_sum(kk * kk, bd_ref[...]))
    kkn_ref[...] = (kk / jnp.maximum(nrm, 1e-12)).astype(kkn_ref.dtype)


def rw_prep(P, w0, w2p, a0, a2p, kkp, bd, tt=768):
    B, S, _ = P.shape
    W = D_MODEL
    tokw = pl.BlockSpec((None, tt, W), lambda b, i: (b, i, COL_RW // W + 1))
    tail = pl.BlockSpec((None, tt, 512), lambda b, i: (b, i, COL_TAIL // 512))
    full = lambda shape: pl.BlockSpec(shape, lambda b, i: (0,) * len(shape))
    out2 = pl.BlockSpec((2, None, tt, W), lambda b, i: (0, b, i, 0))
    return pl.pallas_call(
        _rw_prep_kernel,
        grid=(B, S // tt),
        in_specs=[tokw, tail, full((2, W)), full((2, LANES, W)), full((2, W)), full((2, LANES, W)),
                  full((1, W)), full((GROUP_W, GROUP_W))],
        out_specs=[pl.BlockSpec((None, tt, W), lambda b, i: (b, i, 0)), out2, out2],
        out_shape=[jax.ShapeDtypeStruct((B, S, W), BF16),
                   jax.ShapeDtypeStruct((2, B, S, W), F32),
                   jax.ShapeDtypeStruct((2, B, S, W), BF16)],
        compiler_params=_cparams(("arbitrary", "arbitrary")),
        name="rw_prep",
    )(P, P, w0, w2p, a0, a2p, kkp, bd)


def _rw_scan_kernel(r_ref, zk_ref, v_ref, kkn_ref, lw_ref, as_ref, ka_ref, y_ref, h_ref):
    @pl.when(pl.program_id(1) == 0)
    def _():
        h_ref[...] = jnp.zeros_like(h_ref)

    C = RW_CHUNK
    n_sub = lw_ref[0].shape[0] // C
    for sub in range(n_sub):
        rows = (pl.ds(sub * C, C), pl.ds((n_sub - 1 - sub) * C, C))
        view = lambda pair: tuple(pair[d].at[rows[d]] for d in range(2))
        _scan_chunk(view(r_ref), view(zk_ref), view(v_ref), view(kkn_ref), view(lw_ref), view(as_ref),
                    ka_ref, view(y_ref), h_ref)


def _scan_chunk(r_ref, zk_ref, v_ref, kkn_ref, lw_ref, as_ref, ka_ref, y_ref, h_ref):
    C = RW_CHUNK
    R2 = 2 * C
    n_pairs = ka_ref.shape[1] // LANES

    ri = lax.broadcasted_iota(jnp.int32, (C, C), 0)
    ci = lax.broadcasted_iota(jnp.int32, (C, C), 1)
    rt = lax.broadcasted_iota(jnp.int32, (C, LANES), 0)
    lane = lax.broadcasted_iota(jnp.int32, (C, LANES), 1)
    cs = lane & (C - 1)
    levels = int(math.log2(C))
    same_blk = [(rt >> k) == (cs >> k) for k in range(levels + 1)]
    joins = [same_blk[k + 1] & ~same_blk[k] for k in range(levels)]
    eye = jnp.where(rt == cs, 1.0, 0.0)
    first = lane < RW_HEAD
    hr = lax.broadcasted_iota(jnp.int32, (LANES, LANES), 0)
    hc = lax.broadcasted_iota(jnp.int32, (LANES, LANES), 1)
    same_head = (hr >= RW_HEAD) == (hc >= RW_HEAD)
    ka = ka_ref[...]

    def stack(x):
        return jnp.concatenate([jnp.where(first, x, 0.0), jnp.where(first, 0.0, x)], axis=0)

    chains = []
    for d in range(2):
        before_c = (ci <= ri) if d == 0 else (ci >= ri)
        strict = (cs < rt) if d == 0 else (cs > rt)
        incl = (cs <= rt) if d == 0 else (cs >= rt)
        lw = lw_ref[d][...]
        L = _split3_dot_left(jnp.where(before_c, 1.0, 0.0).astype(BF16), lw)
        ltot = jnp.sum(lw, axis=0, keepdims=True)
        lmid = 0.5 * ltot
        e_r = jnp.exp(L - lmid)
        e_a = jnp.exp(L - lw - lmid)
        e_b = jnp.exp(lmid - L)
        e_mid = jnp.exp(lmid)
        asig = as_ref[d][...].astype(F32)
        kkn = kkn_ref[d][...].astype(F32)
        kd = zk_ref[d][...].astype(F32) * (1.0 + (asig - 1.0) * ka)
        a_in = -kkn * e_a
        r_in = r_ref[d][...].astype(F32) * e_r
        b_in = kkn * asig * e_b
        k_in = kd * e_b
        a_h = a_in * e_mid
        r_h = r_in * e_mid
        b_o = b_in * e_mid
        k_o = k_in * e_mid
        wtot = jnp.exp(ltot)
        vv = v_ref[d][...].astype(F32)
        for p in range(n_pairs):
            sl = slice(p * LANES, (p + 1) * LANES)
            chains.append(dict(
                d=d, p=p, sl=sl, strict=strict, incl=incl, wtot=wtot[:, sl],
                AR=jnp.concatenate([a_in[:, sl], r_in[:, sl]], axis=0).astype(BF16),
                BK=jnp.concatenate([stack(b_in[:, sl]), stack(k_in[:, sl])], axis=0).astype(BF16),
                ARh=jnp.concatenate([a_h[:, sl], r_h[:, sl]], axis=0).astype(BF16),
                BKo=jnp.concatenate([b_o[:, sl], k_o[:, sl]], axis=0).astype(BF16),
                V=vv[:, sl]))

    for ch in chains:
        G1 = _dot_nt(ch["AR"], ch["BK"])
        ch["Aab"] = jnp.where(ch["strict"], G1[:C, :R2], 0.0)
        ch["Aak"] = jnp.where(ch["strict"], G1[:C, R2:], 0.0).astype(BF16)
        ch["Ar"] = jnp.concatenate([jnp.where(ch["incl"], G1[C:, :R2], 0.0),
                                    jnp.where(ch["incl"], G1[C:, R2:], 0.0)], axis=1).astype(BF16)
        ch["H"] = h_ref[ch["d"], ch["p"]]
        ch["T"] = eye + jnp.where(joins[0], ch["Aab"], 0.0)
        ch["As"] = stack(ch["Aab"]).astype(BF16)
    for ch in chains:
        ch["G2"] = _dot_nt(ch["ARh"], ch["H"].astype(BF16))
        ch["Vs"] = stack(ch["V"]).astype(BF16)
        ch["rhs"] = ch["G2"][:C] + _dot(ch["Aak"], ch["Vs"])
    for join in joins[1:]:
        for ch in chains:
            ch["TA"] = _dot(ch["T"].astype(BF16), ch["As"]).astype(BF16)
        for ch in chains:
            ch["T"] = ch["T"] + jnp.where(join, _dot(ch["TA"], stack(ch["T"]).astype(BF16)), 0.0)
    for ch in chains:
        ch["U"] = _dot(ch["T"].astype(BF16), stack(ch["rhs"]).astype(BF16))
    for ch in chains:
        UVs = jnp.concatenate([stack(ch["U"]).astype(BF16), ch["Vs"]], axis=0)
        y_ref[ch["d"]][:, ch["sl"]] = ch["G2"][C:] + _dot(ch["Ar"], UVs)
    for ch in chains:
        UVt = jnp.concatenate([ch["U"], ch["V"]], axis=0).T.astype(BF16)
        upd = jnp.where(same_head, _dot(UVt, ch["BKo"]), 0.0)
        h_ref[ch["d"], ch["p"]] = ch["H"] * ch["wtot"] + upd


def rw_scan(P, kkn, lw, asig, ka, n_ctx):
    B, S, _ = P.shape
    W = D_MODEL
    C = RW_CHUNK * RW_SUB
    nc = S // C
    ncc = n_ctx // C

    def rev(c):
        return jnp.where(c < ncc, ncc - 1 - c, nc - 1 + ncc - c)

    def both(make):
        return [make(lambda c: c), make(rev)]

    pcol = lambda col: both(lambda f: pl.BlockSpec((None, C, W), lambda b, c, f=f: (b, f(c), col)))
    tok = both(lambda f: pl.BlockSpec((None, C, W), lambda b, c, f=f: (b, f(c), 0)))
    dtok = [pl.BlockSpec((None, None, C, W), lambda b, c: (0, b, c, 0)),
            pl.BlockSpec((None, None, C, W), lambda b, c: (1, b, rev(c), 0))]
    c0 = COL_RW // W

    def kern(rf, rb_, kf, kb_, vf, vb_, nf, nb_, lwf, lwb, asf, asb, ka_ref, yf, yb, h_ref):
        _rw_scan_kernel((rf, rb_), (kf, kb_), (vf, vb_), (nf, nb_), (lwf, lwb), (asf, asb), ka_ref,
                        (yf, yb), h_ref)

    return pl.pallas_call(
        kern,
        grid=(B, nc),
        in_specs=[*pcol(c0), *pcol(c0 + 1), *pcol(c0 + 2), *tok, *dtok, *dtok,
                  pl.BlockSpec((1, W), lambda b, c: (0, 0))],
        out_specs=[pl.BlockSpec((None, C, W), lambda b, c: (b, c, 0)),
                   pl.BlockSpec((None, C, W), lambda b, c: (b, rev(c), 0))],
        out_shape=[jax.ShapeDtypeStruct((B, S, W), F32)] * 2,
        scratch_shapes=[pltpu.VMEM((2, W // LANES, LANES, LANES), F32)],
        compiler_params=_cparams(("arbitrary", "arbitrary")),
        name="rw_scan",
    )(P, P, P, P, P, P, kkn, kkn, lw, lw, asig, asig, ka)


def _merge_kernel(x_ref, m_ref, yf_ref, yb_ref, as_ref, r_ref, zk_ref, v_ref, zt_ref, ga_ref, gb_ref, gc_ref,
                  a_ref, bc_ref, bl_ref, lnw_ref, lnb_ref, rk_ref, ka_ref, g2_ref, bd_ref,
                  wa_ref, wb_ref, wc_ref, wo_ref, o_ref, *, n_ctx):
    tt = x_ref.shape[0]
    row0 = pl.program_id(1) * tt
    bd = bd_ref[...]
    inv_n = 1.0 / RW_HEAD
    y = yf_ref[...] + yb_ref[...]
    mean = _group_sum(y, bd) * inv_n
    yc = y - mean
    var = _group_sum(yc * yc, bd) * inv_n
    yn = yc * lax.rsqrt(var + GN_EPS) * lnw_ref[...] + lnb_ref[...]
    zk = zk_ref[...].astype(F32)
    ka = ka_ref[...]
    kd_sum = (zk * (1.0 + (as_ref[0].astype(F32) - 1.0) * ka)
              + zk * (1.0 + (as_ref[1].astype(F32) - 1.0) * ka))
    vv = v_ref[...].astype(F32)
    bonus = _group_sum(r_ref[...].astype(F32) * kd_sum * rk_ref[...], bd) * vv
    g = _dot(jax.nn.sigmoid(zt_ref[:, 2 * LANES:].astype(F32)).astype(BF16), g2_ref[...])
    c_br = ((yn + bonus) * g).astype(BF16)
    gate = lambda ref: jax.nn.sigmoid(ref[...].astype(F32))
    m = (gate(ga_ref) * _dot(a_ref[...], wa_ref[...])
         + gate(gb_ref) * _dot(jnp.where(row0 < n_ctx, bc_ref[...], bl_ref[...]), wb_ref[...])
         + gate(gc_ref) * _dot(c_br, wc_ref[...]))
    out = _dot(m.astype(BF16), wo_ref[...])
    o_ref[...] = x_ref[...] + _gate_rows(m_ref, 2, row0, tt, n_ctx) * out


def merge(xs, mod, y, asig, P, a_br, b_ctx, b_lat, lnw, lnb, rk, ka, g2p, bd, wa, wb, wc, wo, n_ctx, tt=256):
    B, S, W = xs.shape
    tok = pl.BlockSpec((None, tt, W), lambda b, i: (b, i, 0))
    pcol = lambda col: pl.BlockSpec((None, tt, W), lambda b, i, col=col: (b, i, col))
    dtok = pl.BlockSpec((2, None, tt, W), lambda b, i: (0, b, i, 0))
    row = pl.BlockSpec((1, W), lambda b, i: (0, 0))
    wsq = pl.BlockSpec((W, W), lambda b, i: (0, 0))
    cr, cg = COL_RW // W, COL_GATE // W
    nct = n_ctx // tt
    kern = functools.partial(_merge_kernel, n_ctx=n_ctx)
    return pl.pallas_call(
        kern,
        grid=(B, S // tt),
        in_specs=[tok, pl.BlockSpec((None, 12, W), lambda b, i: (b, 0, 0)), tok, tok, dtok,
                  pcol(cr), pcol(cr + 1), pcol(cr + 2),
                  pl.BlockSpec((None, tt, 512), lambda b, i: (b, i, COL_TAIL // 512)),
                  pcol(cg), pcol(cg + 1), pcol(cg + 2), tok,
                  pl.BlockSpec((None, tt, W), lambda b, i: (b, jnp.minimum(i, nct - 1), 0)),
                  pl.BlockSpec((None, tt, W), lambda b, i: (b, jnp.maximum(i - nct, 0), 0)),
                  row, row, row, row,
                  pl.BlockSpec((2 * LANES, W), lambda b, i: (0, 0)),
                  pl.BlockSpec((GROUP_W, GROUP_W), lambda b, i: (0, 0)),
                  wsq, wsq, wsq, wsq],
        out_specs=tok,
        out_shape=jax.ShapeDtypeStruct((B, S, W), F32),
        compiler_params=_cparams(("arbitrary", "arbitrary")),
        name="merge",
    )(xs, mod, y[0], y[1], asig, P, P, P, P, P, P, P, a_br, b_ctx, b_lat, lnw, lnb, rk, ka, g2p, bd, wa, wb, wc, wo)


def _ffn_kernel(x_ref, m_ref, g_ref, wi_ref, wo_ref, o_ref, *, n_ctx, fc):
    tm = x_ref.shape[0]
    row0 = pl.program_id(1) * tm
    FF = wo_ref.shape[0]
    x = x_ref[...]
    h = _modulated_norm(x, g_ref[...], m_ref, 3, row0, n_ctx).astype(BF16)

    def gate_up(k):
        return (_dot(h, wi_ref[:, k * fc:(k + 1) * fc]), _dot(h, wi_ref[:, FF + k * fc:FF + (k + 1) * fc]))

    nxt = gate_up(0)
    acc = None
    for k in range(FF // fc):
        gt, up = nxt
        if (k + 1) * fc < FF:
            nxt = gate_up(k + 1)
        act = (gt * jax.nn.sigmoid(gt) * up).astype(BF16)
        part = _dot(act, wo_ref[k * fc:(k + 1) * fc, :])
        acc = part if acc is None else acc + part
    o_ref[...] = x + _gate_rows(m_ref, 5, row0, tm, n_ctx) * acc


def ffn(xs, mod, g, wi, wo, n_ctx, tm=384, fc=256):
    B, S, W = xs.shape
    FF = wo.shape[0]
    kern = functools.partial(_ffn_kernel, n_ctx=n_ctx, fc=fc)
    tok = pl.BlockSpec((None, tm, W), lambda b, i: (b, i, 0))
    return pl.pallas_call(
        kern,
        grid=(B, S // tm),
        in_specs=[tok, pl.BlockSpec((None, 12, W), lambda b, i: (b, 0, 0)),
                  pl.BlockSpec((1, W), lambda b, i: (0, 0)),
                  pl.BlockSpec((W, 2 * FF), lambda b, i: (0, 0)),
                  pl.BlockSpec((FF, W), lambda b, i: (0, 0))],
        out_specs=tok,
        out_shape=jax.ShapeDtypeStruct((B, S, W), F32),
        compiler_params=_cparams(("arbitrary", "arbitrary")),
        name="ffn",
    )(xs, mod, g, wi, wo)


def _rope_tables(n_rows, n_ctx):
    row = jnp.repeat(jnp.arange(n_rows), GRID_W).astype(F32)
    col = jnp.tile(jnp.arange(GRID_W), n_rows).astype(F32)
    axis_dim = DA_HEAD_DIM // 2
    inv_freq = ROPE_BASE ** (-jnp.arange(0, axis_dim, 2, dtype=F32) / axis_dim)
    ang = jnp.concatenate([row[:, None] * inv_freq, col[:, None] * inv_freq], axis=-1)
    ang = jnp.concatenate([ang, ang], axis=-1)
    cos, sin = jnp.cos(ang), jnp.sin(ang)
    cos = jnp.concatenate([jnp.ones((n_ctx, DA_HEAD_DIM), F32), cos], axis=0)
    sin = jnp.concatenate([jnp.zeros((n_ctx, DA_HEAD_DIM), F32), sin], axis=0)
    first = jnp.arange(DA_HEAD_DIM) < DA_HEAD_DIM // 2
    sin_a = jnp.where(first, -sin, 0.0)
    sin_b = jnp.where(first, 0.0, sin)
    dup = lambda t: jnp.concatenate([t, t], axis=-1)
    return dup(cos), dup(sin_a), dup(sin_b)


def kernel(x, c, ctx, c_ctx, ada_w, ada_b, norm1_g, norm2_g, w_in, gm_v_g, gm_ws, gm_bs,
           da_q_g, da_k_g, da_lambda, da_subln_g, rw_mu, rw_w0, rw_w2, rw_a0, rw_a2, rw_g2,
           rw_kk, rw_ka, rw_rk, rw_ln_w, rw_ln_b, w_br_a, w_br_b, w_br_c, w_o, ffn_wi, ffn_wo):
    B, T, D = x.shape
    n_ctx = ctx.shape[1]
    L = ada_w.shape[0]
    cos, sin_a, sin_b = _rope_tables(T // GRID_W, n_ctx)
    xs = jnp.concatenate([ctx, x], axis=1)

    cc = jnp.zeros((16, D), F32).at[:B].set(c).at[B].set(c_ctx)
    mods = ada_mod(cc, ada_w, ada_b)
    mod_l = mods[:, :B].reshape(L, B, 6, 1, D)
    mod_c = jnp.broadcast_to(mods[:, B].reshape(L, 1, 6, 1, D), (L, B, 6, 1, D))
    mod12 = jnp.concatenate([mod_c, mod_l], axis=3).reshape(L, B, 12, D)

    gi = jnp.arange(GROUP_W) // RW_HEAD
    bd = (gi[:, None] == gi[None, :]).astype(BF16)
    row = lambda t: t.reshape(1, -1)
    g0, d0, r0, t0 = 0, 2048, 5120, 5120 + 3072
    k0 = r0 + 3488

    for l in range(L):
        lam_init = 0.8 - 0.6 * math.exp(-0.3 * l)
        wl = w_in[l]
        w_cat = jnp.concatenate([wl[:, a:b].astype(BF16) for a, b in
                                 ((g0, d0), (d0, r0), (k0, wl.shape[1]), (r0, t0), (t0, k0))]
                                + [jnp.zeros((D, 96), BF16)], axis=1)
        mu = jnp.concatenate([jnp.zeros((COL_RW,), F32), rw_mu[l], jnp.zeros((96,), F32)]).reshape(1, -1)
        P = proj_in(xs, mod12[l], row(norm1_g[l]), w_cat, mu, n_ctx)

        bs_full = jnp.repeat(gm_bs[l].T, CHUNK, axis=1)
        a_br = gmlp(P, row(gm_v_g[l]), gm_ws[l].astype(BF16), bs_full)

        tile = lambda t, n: jnp.tile(t, n).reshape(1, -1)
        qn, kn, vb = qk_prep(P, cos, sin_a, sin_b, tile(da_q_g[l], 16), tile(da_k_g[l], 16), bd)
        b_ctx = attention_ctx(qn, kn, vb, da_lambda[l], row(da_subln_g[l]), n_ctx, lam_init)
        b_lat = attention_lat(qn, kn, vb, da_lambda[l], row(da_subln_g[l]), n_ctx, lam_init)

        zpad = jnp.zeros((DECAY_LORA, D), F32)
        w2p = jnp.stack([jnp.concatenate([rw_w2[l, 0], zpad]), jnp.concatenate([zpad, rw_w2[l, 1]])]).astype(BF16)
        a2p = jnp.stack([jnp.concatenate([rw_a2[l, 0], zpad]), jnp.concatenate([zpad, rw_a2[l, 1]])]).astype(BF16)
        kkn, lw, asig = rw_prep(P, rw_w0[l], w2p, rw_a0[l], a2p, row(rw_kk[l]), bd)
        y = rw_scan(P, kkn, lw, asig, row(rw_ka[l]), n_ctx)

        g2p = jnp.concatenate([rw_g2[l], jnp.zeros((2 * LANES - GATE_LORA, D), F32)]).astype(BF16)
        xs = merge(xs, mod12[l], y, asig, P, a_br, b_ctx, b_lat, row(rw_ln_w[l]), row(rw_ln_b[l]),
                   row(rw_rk[l]), row(rw_ka[l]), g2p, bd,
                   w_br_a[l].astype(BF16), w_br_b[l].astype(BF16), w_br_c[l].astype(BF16),
                   w_o[l].astype(BF16), n_ctx)
        xs = ffn(xs, mod12[l], row(norm2_g[l]), ffn_wi[l].astype(BF16), ffn_wo[l].astype(BF16), n_ctx)
    return xs[:, n_ctx:]
```

```python
import functools
import math

import jax
import jax.numpy as jnp
from jax import lax
from jax.experimental import pallas as pl
from jax.experimental.pallas import tpu as pltpu

D_MODEL = 1024
GRID_W = 64
CHUNK = 128
GM_GROUPS = 8
DA_HEADS = 8
DA_HEAD_DIM = 64
DA_V_DIM = 128
ROPE_BASE = 10000.0
RW_HEAD = 64
RW_HEADS = 16
DECAY_LORA = 64
AAA_LORA = 64
GATE_LORA = 160
D_FF = 2816
RMS_EPS = 1e-6
GN_EPS = 64e-5

COL_GM = 0
COL_DA = 2048
COL_GATE = 5120
COL_RW = 8192
COL_TAIL = 11264
NP_COLS = 11776
PROJ_TN = 512
RW_CHUNK = 64
RW_SUB = 4
ATTN_KB = 256
ATTN_AHEAD = 3
LANES = 128
GROUP_W = 256
VMEM_LIMIT = 56 * 1024 * 1024

BF16 = jnp.bfloat16
F32 = jnp.float32


def _dot(a, b):
    return jnp.dot(a, b, preferred_element_type=F32)


def _dot_nt(a, b):
    return lax.dot_general(a, b, (((1,), (1,)), ((), ())), preferred_element_type=F32)


def _split_dot(x, m):
    hi = x.astype(BF16)
    lo = (x - hi.astype(F32)).astype(BF16)
    return _dot(hi, m) + _dot(lo, m)


def _split3_dot_left(m, x):
    hi = x.astype(BF16)
    r1 = x - hi.astype(F32)
    mid = r1.astype(BF16)
    lo = (r1 - mid.astype(F32)).astype(BF16)
    return _dot(m, hi) + _dot(m, mid) + _dot(m, lo)


def _group_sum(x, bd):
    width = bd.shape[0]
    parts = [_split_dot(x[:, i:i + width], bd) for i in range(0, x.shape[1], width)]
    return parts[0] if len(parts) == 1 else jnp.concatenate(parts, axis=1)


def _cparams(sem):
    return pltpu.CompilerParams(dimension_semantics=sem, vmem_limit_bytes=VMEM_LIMIT)


def _ada_kernel(c_ref, w_ref, b_ref, o_ref):
    c = c_ref[...]
    s = c * jax.nn.sigmoid(c)
    o_ref[...] = jnp.dot(s, w_ref[...], precision=lax.Precision.HIGHEST,
                         preferred_element_type=F32) + b_ref[...]


def ada_mod(cc, ada_w, ada_b):
    L, D, N = ada_w.shape
    tn = 1536
    return pl.pallas_call(
        _ada_kernel,
        grid=(L, N // tn),
        in_specs=[pl.BlockSpec((16, D), lambda l, j: (0, 0)),
                  pl.BlockSpec((None, D, tn), lambda l, j: (l, 0, j)),
                  pl.BlockSpec((None, 1, tn), lambda l, j: (l, 0, j))],
        out_specs=pl.BlockSpec((None, 16, tn), lambda l, j: (l, 0, j)),
        out_shape=jax.ShapeDtypeStruct((L, 16, N), F32),
        compiler_params=_cparams(("arbitrary", "arbitrary")),
        name="ada_mod",
    )(cc, ada_w, ada_b.reshape(L, 1, N))


def _modulated_norm(x, g, m_ref, piece, row0, n_ctx):
    rows = row0 + lax.broadcasted_iota(jnp.int32, (x.shape[0], 1), 0)
    is_lat = rows >= n_ctx
    sh = jnp.where(is_lat, m_ref[2 * piece + 1:2 * piece + 2, :], m_ref[2 * piece:2 * piece + 1, :])
    sc = jnp.where(is_lat, m_ref[2 * piece + 3:2 * piece + 4, :], m_ref[2 * piece + 2:2 * piece + 3, :])
    y = x * lax.rsqrt(jnp.mean(x * x, axis=-1, keepdims=True) + RMS_EPS) * g
    return y * (1.0 + sc) + sh


def _gate_rows(m_ref, piece, row0, nrows, n_ctx):
    rows = row0 + lax.broadcasted_iota(jnp.int32, (nrows, 1), 0)
    return jnp.where(rows >= n_ctx, m_ref[2 * piece + 1:2 * piece + 2, :], m_ref[2 * piece:2 * piece + 1, :])


def _proj_kernel(x_ref, m_ref, g_ref, w_ref, mu_ref, o_ref, h_ref, p_ref, *, n_ctx, rb, mb, rw_j0):
    S = x_ref.shape[0]
    j = pl.program_id(1)
    nblk = S // rb

    @pl.when(j == 0)
    def _():
        def body(i, carry):
            r0 = pl.multiple_of(i * rb, rb)
            x = x_ref[pl.ds(r0, rb), :]
            h_ref[pl.ds(r0, rb), :] = _modulated_norm(x, g_ref[...], m_ref, 0, r0, n_ctx).astype(BF16)
            return carry
        lax.fori_loop(0, nblk, body, 0)

    def project(store):
        store(0, n_ctx, _dot(h_ref[0:n_ctx, :], w_ref[...]))

        def body(i, carry):
            r0 = pl.multiple_of(n_ctx + i * mb, math.gcd(n_ctx, mb))
            store(r0, mb, _dot(h_ref[pl.ds(r0, mb), :], w_ref[...]))
            return carry
        lax.fori_loop(0, (S - n_ctx) // mb, body, 0)

    @pl.when(j < rw_j0)
    def _():
        def store(r0, rows, val):
            o_ref[pl.ds(r0, rows), :] = val.astype(o_ref.dtype)
        project(store)

    @pl.when(j >= rw_j0)
    def _():
        zero = jnp.zeros((8, p_ref.shape[1]), F32)
        p_ref[0:8, :] = zero
        p_ref[n_ctx + 8:n_ctx + 16, :] = zero
        p_ref[S + 16:S + 24, :] = zero

        def store(r0, rows, val):
            off = 8 if isinstance(r0, int) and r0 < n_ctx else 16
            p_ref[pl.ds(r0 + off, rows), :] = val
        project(store)
        c_cur =1.0 - mu_ref[...]
        c_nb = 0.5 * mu_ref[...]
        for i in range(nblk):
            r0 = i * rb
            src = r0 + (16 if r0 >= n_ctx else 8)
            nb = p_ref[src - 1:src - 1 + rb, :] + p_ref[src + 1:src + 1 + rb, :]
            o_ref[r0:r0 + rb, :] = (p_ref[src:src + rb, :] * c_cur + nb * c_nb).astype(o_ref.dtype)


def proj_in(xs, mod, g, w, mu, n_ctx):
    B, S, D = xs.shape
    NP = w.shape[1]
    tn = PROJ_TN
    kern = functools.partial(_proj_kernel, n_ctx=n_ctx, rb=n_ctx, mb=min(1024, S - n_ctx), rw_j0=COL_RW // tn)
    return pl.pallas_call(
        kern,
        grid=(B, NP // tn),
        in_specs=[pl.BlockSpec((None, S, D), lambda b, j: (b, 0, 0)),
                  pl.BlockSpec((None, 12, D), lambda b, j: (b, 0, 0)),
                  pl.BlockSpec((1, D), lambda b, j: (0, 0)),
                  pl.BlockSpec((D, tn), lambda b, j: (0, j)),
                  pl.BlockSpec((1, tn), lambda b, j: (0, j))],
        out_specs=pl.BlockSpec((None, S, tn), lambda b, j: (b, 0, j)),
        out_shape=jax.ShapeDtypeStruct((B, S, NP), BF16),
        scratch_shapes=[pltpu.VMEM((S, D), BF16), pltpu.VMEM((S + 24, tn), F32)],
        compiler_params=_cparams(("arbitrary", "arbitrary")),
        name="proj_in",
    )(xs, mod, g, w, mu)


def _gelu(x):
    return 0.5 * x * (1.0 + lax.erf(x * (1.0 / math.sqrt(2.0))))


def _gmlp_kernel(p_ref, vg_ref, ws_ref, bs_ref, o_ref):
    W = vg_ref.shape[1]
    gd = W // GM_GROUPS
    for c in range(p_ref.shape[0] // CHUNK):
        rows = slice(c * CHUNK, (c + 1) * CHUNK)
        u = _gelu(p_ref[rows, :W].astype(F32))
        v = _gelu(p_ref[rows, W:].astype(F32))
        v = v * lax.rsqrt(jnp.mean(v * v, axis=-1, keepdims=True) + RMS_EPS) * vg_ref[...]
        vb = v.astype(BF16)
        for g in range(GM_GROUPS):
            sl = slice(g * gd, (g + 1) * gd)
            f = _dot(ws_ref[g], vb[:, sl]) + bs_ref[:, sl]
            o_ref[rows, sl] = (u[:, sl] * f).astype(o_ref.dtype)


def gmlp(P, vg, ws, bs_full, tt=768):
    B, S, _ = P.shape
    W = vg.shape[1]
    return pl.pallas_call(
        _gmlp_kernel,
        grid=(B, S // tt),
        in_specs=[pl.BlockSpec((None, tt, 2 * W), lambda b, i: (b, i, 0)),
                  pl.BlockSpec((1, W), lambda b, i: (0, 0)),
                  pl.BlockSpec((GM_GROUPS, CHUNK, CHUNK), lambda b, i: (0, 0, 0)),
                  pl.BlockSpec((CHUNK, W), lambda b, i: (0, 0))],
        out_specs=pl.BlockSpec((None, tt, W), lambda b, i: (b, i, 0)),
        out_shape=jax.ShapeDtypeStruct((B, S, W), BF16),
        compiler_params=_cparams(("arbitrary", "arbitrary")),
        name="gmlp",
    )(P, vg, ws, bs_full)


def _qk_kernel(q_ref, k_ref, v_ref, cos_ref, sa_ref, sb_ref, qg_ref, kg_ref, bd_ref,
               qo_ref, ko_ref, vo_ref):
    W = q_ref.shape[1]
    reps = W // LANES
    bd = bd_ref[...]
    sw = bd.shape[0]
    cos = jnp.tile(cos_ref[...], (1, sw // LANES))
    sin_a = jnp.tile(sa_ref[...], (1, sw // LANES))
    sin_b = jnp.tile(sb_ref[...], (1, sw // LANES))
    half = DA_HEAD_DIM // 2

    def prep(src, g_ref, dst, scale):
        for c0 in range(0, W, sw):
            t = src[:, c0:c0 + sw].astype(F32)
            ms = _group_sum(t * t, bd) * (1.0 / DA_HEAD_DIM)
            t = t * lax.rsqrt(ms + RMS_EPS) * g_ref[:, c0:c0 + sw]
            t = t * cos + pltpu.roll(t, sw - half, 1) * sin_a + pltpu.roll(t, half, 1) * sin_b
            dst[:, c0:c0 + sw] = (t * scale if scale != 1.0 else t).astype(dst.dtype)

    prep(q_ref, qg_ref, qo_ref, DA_HEAD_DIM ** -0.5 * math.log2(math.e))
    prep(k_ref, kg_ref, ko_ref, 1.0)
    for h in range(reps):
        vo_ref[h] = v_ref[:, h * LANES:(h + 1) * LANES].astype(F32).T.astype(BF16)


def qk_prep(P, cos, sin_a, sin_b, qg, kg, bd, tt=768):
    B, S, _ = P.shape
    W = D_MODEL
    c0 = COL_DA // W
    tok = lambda c: pl.BlockSpec((None, tt, W), lambda b, i, c=c: (b, i, c))
    tab = pl.BlockSpec((tt, LANES), lambda b, i: (i, 0))
    row = pl.BlockSpec((1, W), lambda b, i: (0, 0))
    out = pl.BlockSpec((None, tt, W), lambda b, i: (b, i, 0))
    shp = jax.ShapeDtypeStruct((B, S, W), BF16)
    return pl.pallas_call(
        _qk_kernel,
        grid=(B, S // tt),
        in_specs=[tok(c0), tok(c0 + 1), tok(c0 + 2), tab, tab, tab, row, row,
                  pl.BlockSpec((GROUP_W, GROUP_W), lambda b, i: (0, 0))],
        out_specs=[out, out, pl.BlockSpec((None, W // LANES, LANES, tt), lambda b, i: (b, 0, 0, i))],
        out_shape=[shp, shp, jax.ShapeDtypeStruct((B, W // LANES, LANES, S), BF16)],
        compiler_params=_cparams(("arbitrary", "arbitrary")),
        name="qk_prep",
    )(P, P, P, cos, sin_a, sin_b, qg, kg, bd)


def _attn_kernel(q_ref, k_ref, v_ref, lam_ref, sg_ref, o_ref, *, n_ctx, lam_init):
    i = pl.program_id(2)
    tq = q_ref.shape[0]
    lp = lam_ref[...]
    lam = (jnp.exp(jnp.sum(lp[0:1] * lp[1:2], axis=-1, keepdims=True))
           - jnp.exp(jnp.sum(lp[2:3] * lp[3:4], axis=-1, keepdims=True)) + lam_init)

    def attend(nk):
        q = q_ref[...]
        lane = lax.broadcasted_iota(jnp.int32, q.shape, 1)
        zero = jnp.zeros_like(q)
        qs = jnp.concatenate([jnp.where(lane < DA_HEAD_DIM, q, zero),
                              jnp.where(lane >= DA_HEAD_DIM, q, zero)], axis=0)
        m = l = acc = None
        nblk = nk // ATTN_KB
        blk = lambda j: slice(j * ATTN_KB, (j + 1) * ATTN_KB)
        scores = lambda j: _dot_nt(k_ref[blk(j), :], qs)
        queue = [scores(j) for j in range(min(ATTN_AHEAD, nblk))]
        for j in range(nblk):
            ks = blk(j)
            if j + ATTN_AHEAD < nblk:
                queue.append(scores(j + ATTN_AHEAD))
            s = queue.pop(0)
            m_blk = jnp.max(s, axis=0, keepdims=True)
            m_new = m_blk if m is None else jnp.maximum(m, m_blk)
            e = jnp.exp2(s - m_new)
            pv = _dot(v_ref[:, ks], e.astype(BF16))
            if m is None:
                l, acc = jnp.sum(e, axis=0, keepdims=True), pv
            else:
                alpha = jnp.exp2(m - m_new)
                l = alpha * l + jnp.sum(e, axis=0, keepdims=True)
                acc = alpha * acc + pv
            m = m_new
        inv = 1.0 / l
        o = acc[:, :tq] * inv[:, :tq] - acc[:, tq:] * (lam * inv[:, tq:])
        o = o * lax.rsqrt(jnp.mean(o * o, axis=0, keepdims=True) + RMS_EPS)
        o_ref[...] = (o.T * sg_ref[...] * (1.0 - lam_init)).astype(o_ref.dtype)

    attend(k_ref.shape[0])


def attention_ctx(qn, kn, vt, lam_p, sg, n_ctx, lam_init, tq=256):
    B, S, W = qn.shape
    H = W // LANES
    kern = functools.partial(_attn_kernel, n_ctx=n_ctx, lam_init=lam_init)
    return pl.pallas_call(
        kern,
        grid=(B, H, n_ctx // tq),
        in_specs=[pl.BlockSpec((None, tq, LANES), lambda b, h, i: (b, i, h)),
                  pl.BlockSpec((None, n_ctx, LANES), lambda b, h, i: (b, 0, h)),
                  pl.BlockSpec((None, None, LANES, n_ctx), lambda b, h, i: (b, h, 0, 0)),
                  pl.BlockSpec((4, DA_HEAD_DIM), lambda b, h, i: (0, 0)),
                  pl.BlockSpec((1, LANES), lambda b, h, i: (0, 0))],
        out_specs=pl.BlockSpec((None, tq, LANES), lambda b, h, i: (b, i, h)),
        out_shape=jax.ShapeDtypeStruct((B, n_ctx, W), BF16),
        compiler_params=_cparams(("arbitrary", "arbitrary", "arbitrary")),
        name="attention_ctx",
    )(qn, kn, vt, lam_p, sg)


def _attn_lat_kernel(*refs, lam_init, n_tiles, n_qsub):
    q_refs = refs[:n_qsub]
    k_ref, v_ref, lam_ref, sg_ref, o_ref, s_ref, m_ref = refs[n_qsub:]
    t = pl.program_id(0)
    tq = o_ref.shape[0]
    nblk = k_ref.shape[0] // ATTN_KB
    blk = lambda j: slice(j * ATTN_KB, (j + 1) * ATTN_KB)

    def stacked_q():
        q = jnp.concatenate([r[...] for r in q_refs], axis=0)
        lane = lax.broadcasted_iota(jnp.int32, q.shape, 1)
        zero = jnp.zeros_like(q)
        return jnp.concatenate([jnp.where(lane < DA_HEAD_DIM, q, zero),
                                jnp.where(lane >= DA_HEAD_DIM, q, zero)], axis=0)

    def run(scores_slot, softmax_slot):
        qs = stacked_q() if scores_slot is not None else None
        m_run = l = acc = None
        if softmax_slot is not None:
            m_prev = m_ref[softmax_slot]
        for j in range(nblk):
            if scores_slot is not None:
                s = _dot_nt(k_ref[blk(j), :], qs)
                s_ref[scores_slot, blk(j), :] = s
                for r in range(0, ATTN_KB, 8):
                    m_run = s[r:r + 8] if m_run is None else jnp.maximum(m_run, s[r:r + 8])
            if softmax_slot is not None:
                e = jnp.exp2(s_ref[softmax_slot, blk(j), :] - m_prev)
                pv = _dot(v_ref[:, blk(j)], e.astype(BF16))
                lsum = jnp.sum(e, axis=0, keepdims=True)
                l, acc = (lsum, pv) if l is None else (l + lsum, acc + pv)
        if scores_slot is not None:
            m_ref[scores_slot] = jnp.max(m_run, axis=0, keepdims=True)
        if softmax_slot is not None:
            lp = lam_ref[...]
            lam = (jnp.exp(jnp.sum(lp[0:1] * lp[1:2], axis=-1, keepdims=True))
                   - jnp.exp(jnp.sum(lp[2:3] * lp[3:4], axis=-1, keepdims=True)) + lam_init)
            inv = 1.0 / l
            o = acc[:, :tq] * inv[:, :tq] - acc[:, tq:] * (lam * inv[:, tq:])
            o = o * lax.rsqrt(jnp.mean(o * o, axis=0, keepdims=True) + RMS_EPS)
            o_ref[...] = (o.T * sg_ref[...] * (1.0 - lam_init)).astype(o_ref.dtype)

    @pl.when(t == 0)
    def _():
        run(0, None)

    for par in range(2):
        @pl.when((t > 0) & (t < n_tiles) & (t % 2 == par))
        def _():
            run(par, 1 - par)

    @pl.when(t == n_tiles)
    def _():
        run(None, (n_tiles - 1) % 2)


def attention_lat(qn, kn, vt, lam_p, sg, n_ctx, lam_init, tq=512):
    B, S, W = qn.shape
    H = W // LANES
    nq = (S - n_ctx) // tq
    n_tiles = B * H * nq
    qb = math.gcd(tq, n_ctx)
    n_qsub = tq // qb
    kern = functools.partial(_attn_lat_kernel, lam_init=lam_init, n_tiles=n_tiles, n_qsub=n_qsub)

    def tile(n):
        return n // (H * nq), n % nq, (n // nq) % H

    def scored(t):
        return tile(jnp.minimum(t, n_tiles - 1))

    def finished(t):
        return tile(jnp.maximum(t - 1, 0))

    return pl.pallas_call(
        kern,
        grid=(n_tiles + 1,),
        in_specs=[*[pl.BlockSpec((None, qb, LANES), lambda t, s=s: (
                        scored(t)[0], n_ctx // qb + scored(t)[1] * n_qsub + s, scored(t)[2]))
                    for s in range(n_qsub)],
                  pl.BlockSpec((None, S, LANES), lambda t: (scored(t)[0], 0, scored(t)[2])),
                  pl.BlockSpec((None, None, LANES, S), lambda t: (finished(t)[0], finished(t)[2], 0, 0)),
                  pl.BlockSpec((4, DA_HEAD_DIM), lambda t: (0, 0)),
                  pl.BlockSpec((1, LANES), lambda t: (0, 0))],
        out_specs=pl.BlockSpec((None, tq, LANES), lambda t: finished(t)),
        out_shape=jax.ShapeDtypeStruct((B, S - n_ctx, W), BF16),
        scratch_shapes=[pltpu.VMEM((2, S, 2 * tq), F32), pltpu.VMEM((2, 1, 2 * tq), F32)],
        compiler_params=_cparams(("arbitrary",)),
        name="attention_lat",
    )(*([qn] * n_qsub), kn, vt, lam_p, sg)


def _rw_prep_kernel(zk_ref, zt_ref, w0_ref, w2_ref, a0_ref, a2_ref, kk_ref, bd_ref,
                    kkn_ref, lw_ref, as_ref):
    tw = jnp.tanh(zt_ref[:, 0:LANES].astype(F32)).astype(BF16)
    za = zt_ref[:, LANES:2 * LANES]
    for d in range(2):
        x = w0_ref[d:d + 1, :] + _dot(tw, w2_ref[d])
        lw_ref[d] = -math.exp(-0.5) * jax.nn.sigmoid(x)
        as_ref[d] = jax.nn.sigmoid(a0_ref[d:d + 1, :] + _dot(za, a2_ref[d])).astype(as_ref.dtype)
    kk = zk_ref[...].astype(F32) * kk_ref[...]
    nrm = jnp.sqrt(_group_sum(kk * kk, bd_ref[...]))
    kkn_ref[...] = (kk / jnp.maximum(nrm, 1e-12)).astype(kkn_ref.dtype)


def rw_prep(P, w0, w2p, a0, a2p, kkp, bd, tt=768):
    B, S, _ = P.shape
    W = D_MODEL
    tokw = pl.BlockSpec((None, tt, W), lambda b, i: (b, i, COL_RW // W + 1))
    tail = pl.BlockSpec((None, tt, 512), lambda b, i: (b, i, COL_TAIL // 512))
    full = lambda shape: pl.BlockSpec(shape, lambda b, i: (0,) * len(shape))
    out2 = pl.BlockSpec((2, None, tt, W), lambda b, i: (0, b, i, 0))
    return pl.pallas_call(
        _rw_prep_kernel,
        grid=(B, S // tt),
        in_specs=[tokw, tail, full((2, W)), full((2, LANES, W)), full((2, W)), full((2, LANES, W)),
                  full((1, W)), full((GROUP_W, GROUP_W))],
        out_specs=[pl.BlockSpec((None, tt, W), lambda b, i: (b, i, 0)), out2, out2],
        out_shape=[jax.ShapeDtypeStruct((B, S, W), BF16),
                   jax.ShapeDtypeStruct((2, B, S, W), F32),
                   jax.ShapeDtypeStruct((2, B, S, W), BF16)],
        compiler_params=_cparams(("arbitrary", "arbitrary")),
        name="rw_prep",
    )(P, P, w0, w2p, a0, a2p, kkp, bd)


def _rw_scan_kernel(r_ref, zk_ref, v_ref, kkn_ref, lw_ref, as_ref, ka_ref, y_ref, h_ref):
    @pl.when(pl.program_id(1) == 0)
    def _():
        h_ref[...] = jnp.zeros_like(h_ref)

    C = RW_CHUNK
    n_sub = lw_ref[0].shape[0] // C
    for sub in range(n_sub):
        rows = (pl.ds(sub * C, C), pl.ds((n_sub - 1 - sub) * C, C))
        view = lambda pair: tuple(pair[d].at[rows[d]] for d in range(2))
        _scan_chunk(view(r_ref), view(zk_ref), view(v_ref), view(kkn_ref), view(lw_ref), view(as_ref),
                    ka_ref, view(y_ref), h_ref)


def _scan_chunk(r_ref, zk_ref, v_ref, kkn_ref, lw_ref, as_ref, ka_ref, y_ref, h_ref):
    C = RW_CHUNK
    R2 = 2 * C
    n_pairs = ka_ref.shape[1] // LANES

    ri = lax.broadcasted_iota(jnp.int32, (C, C), 0)
    ci = lax.broadcasted_iota(jnp.int32, (C, C), 1)
    rt = lax.broadcasted_iota(jnp.int32, (C, LANES), 0)
    lane = lax.broadcasted_iota(jnp.int32, (C, LANES), 1)
    cs = lane & (C - 1)
    levels = int(math.log2(C))
    same_blk = [(rt >> k) == (cs >> k) for k in range(levels + 1)]
    joins = [same_blk[k + 1] & ~same_blk[k] for k in range(levels)]
    eye = jnp.where(rt == cs, 1.0, 0.0)
    first = lane < RW_HEAD
    hr = lax.broadcasted_iota(jnp.int32, (LANES, LANES), 0)
    hc = lax.broadcasted_iota(jnp.int32, (LANES, LANES), 1)
    same_head = (hr >= RW_HEAD) == (hc >= RW_HEAD)
    ka = ka_ref[...]

    def stack(x):
        return jnp.concatenate([jnp.where(first, x, 0.0), jnp.where(first, 0.0, x)], axis=0)

    chains = []
    for d in range(2):
        before_c = (ci <= ri) if d == 0 else (ci >= ri)
        strict = (cs < rt) if d == 0 else (cs > rt)
        incl = (cs <= rt) if d == 0 else (cs >= rt)
        lw = lw_ref[d][...]
        L = _split3_dot_left(jnp.where(before_c, 1.0, 0.0).astype(BF16), lw)
        ltot = jnp.sum(lw, axis=0, keepdims=True)
        lmid = 0.5 * ltot
        e_r = jnp.exp(L - lmid)
        e_a = jnp.exp(L - lw - lmid)
        e_b = jnp.exp(lmid - L)
        e_mid = jnp.exp(lmid)
        asig = as_ref[d][...].astype(F32)
        kkn = kkn_ref[d][...].astype(F32)
        kd = zk_ref[d][...].astype(F32) * (1.0 + (asig - 1.0) * ka)
        a_in = -kkn * e_a
        r_in = r_ref[d][...].astype(F32) * e_r
        b_in = kkn * asig * e_b
        k_in = kd * e_b
        a_h = a_in * e_mid
        r_h = r_in * e_mid
        b_o = b_in * e_mid
        k_o = k_in * e_mid
        wtot = jnp.exp(ltot)
        vv = v_ref[d][...].astype(F32)
        for p in range(n_pairs):
            sl = slice(p * LANES, (p + 1) * LANES)
            chains.append(dict(
                d=d, p=p, sl=sl, strict=strict, incl=incl, wtot=wtot[:, sl],
                AR=jnp.concatenate([a_in[:, sl], r_in[:, sl]], axis=0).astype(BF16),
                BK=jnp.concatenate([stack(b_in[:, sl]), stack(k_in[:, sl])], axis=0).astype(BF16),
                ARh=jnp.concatenate([a_h[:, sl], r_h[:, sl]], axis=0).astype(BF16),
                BKo=jnp.concatenate([b_o[:, sl], k_o[:, sl]], axis=0).astype(BF16),
                V=vv[:, sl]))

    for ch in chains:
        G1 = _dot_nt(ch["AR"], ch["BK"])
        ch["Aab"] = jnp.where(ch["strict"], G1[:C, :R2], 0.0)
        ch["Aak"] = jnp.where(ch["strict"], G1[:C, R2:], 0.0).astype(BF16)
        ch["Ar"] = jnp.concatenate([jnp.where(ch["incl"], G1[C:, :R2], 0.0),
                                    jnp.where(ch["incl"], G1[C:, R2:], 0.0)], axis=1).astype(BF16)
        ch["H"] = h_ref[ch["d"], ch["p"]]
        ch["T"] = eye + jnp.where(joins[0], ch["Aab"], 0.0)
        ch["As"] = stack(ch["Aab"]).astype(BF16)
    for ch in chains:
        ch["G2"] = _dot_nt(ch["ARh"], ch["H"].astype(BF16))
        ch["Vs"] = stack(ch["V"]).astype(BF16)
        ch["rhs"] = ch["G2"][:C] + _dot(ch["Aak"], ch["Vs"])
    for join in joins[1:]:
        for ch in chains:
            ch["TA"] = _dot(ch["T"].astype(BF16), ch["As"]).astype(BF16)
        for ch in chains:
            ch["T"] = ch["T"] + jnp.where(join, _dot(ch["TA"], stack(ch["T"]).astype(BF16)), 0.0)
    for ch in chains:
        ch["U"] = _dot(ch["T"].astype(BF16), stack(ch["rhs"]).astype(BF16))
    for ch in chains:
        UVs = jnp.concatenate([stack(ch["U"]).astype(BF16), ch["Vs"]], axis=0)
        y_ref[ch["d"]][:, ch["sl"]] = ch["G2"][C:] + _dot(ch["Ar"], UVs)
    for ch in chains:
        UVt = jnp.concatenate([ch["U"], ch["V"]], axis=0).T.astype(BF16)
        upd = jnp.where(same_head, _dot(UVt, ch["BKo"]), 0.0)
        h_ref[ch["d"], ch["p"]] = ch["H"] * ch["wtot"] + upd


def rw_scan(P, kkn, lw, asig, ka, n_ctx):
    B, S, _ = P.shape
    W = D_MODEL
    C = RW_CHUNK * RW_SUB
    nc = S // C
    ncc = n_ctx // C

    def rev(c):
        return jnp.where(c < ncc, ncc - 1 - c, nc - 1 + ncc - c)

    def both(make):
        return [make(lambda c: c), make(rev)]

    pcol = lambda col: both(lambda f: pl.BlockSpec((None, C, W), lambda b, c, f=f: (b, f(c), col)))
    tok = both(lambda f: pl.BlockSpec((None, C, W), lambda b, c, f=f: (b, f(c), 0)))
    dtok = [pl.BlockSpec((None, None, C, W), lambda b, c: (0, b, c, 0)),
            pl.BlockSpec((None, None, C, W), lambda b, c: (1, b, rev(c), 0))]
    c0 = COL_RW // W

    def kern(rf, rb_, kf, kb_, vf, vb_, nf, nb_, lwf, lwb, asf, asb, ka_ref, yf, yb, h_ref):
        _rw_scan_kernel((rf, rb_), (kf, kb_), (vf, vb_), (nf, nb_), (lwf, lwb), (asf, asb), ka_ref,
                        (yf, yb), h_ref)

    return pl.pallas_call(
        kern,
        grid=(B, nc),
        in_specs=[*pcol(c0), *pcol(c0 + 1), *pcol(c0 + 2), *tok, *dtok, *dtok,
                  pl.BlockSpec((1, W), lambda b, c: (0, 0))],
        out_specs=[pl.BlockSpec((None, C, W), lambda b, c: (b, c, 0)),
                   pl.BlockSpec((None, C, W), lambda b, c: (b, rev(c), 0))],
        out_shape=[jax.ShapeDtypeStruct((B, S, W), F32)] * 2,
        scratch_shapes=[pltpu.VMEM((2, W // LANES, LANES, LANES), F32)],
        compiler_params=_cparams(("arbitrary", "arbitrary")),
        name="rw_scan",
    )(P, P, P, P, P, P, kkn, kkn, lw, lw, asig, asig, ka)


def _merge_kernel(x_ref, m_ref, yf_ref, yb_ref, as_ref, r_ref, zk_ref, v_ref, zt_ref, ga_ref, gb_ref, gc_ref,
                  a_ref, bc_ref, bl_ref, lnw_ref, lnb_ref, rk_ref, ka_ref, g2_ref, bd_ref,
                  wa_ref, wb_ref, wc_ref, wo_ref, o_ref, *, n_ctx):
    tt = x_ref.shape[0]
    row0 = pl.program_id(1) * tt
    bd = bd_ref[...]
    inv_n = 1.0 / RW_HEAD
    y = yf_ref[...] + yb_ref[...]
    mean = _group_sum(y, bd) * inv_n
    yc = y - mean
    var = _group_sum(yc * yc, bd) * inv_n
    yn = yc * lax.rsqrt(var + GN_EPS) * lnw_ref[...] + lnb_ref[...]
    zk = zk_ref[...].astype(F32)
    ka = ka_ref[...]
    kd_sum = (zk * (1.0 + (as_ref[0].astype(F32) - 1.0) * ka)
              + zk * (1.0 + (as_ref[1].astype(F32) - 1.0) * ka))
    vv = v_ref[...].astype(F32)
    bonus = _group_sum(r_ref[...].astype(F32) * kd_sum * rk_ref[...], bd) * vv
    g = _dot(jax.nn.sigmoid(zt_ref[:, 2 * LANES:].astype(F32)).astype(BF16), g2_ref[...])
    c_br = ((yn + bonus) * g).astype(BF16)
    gate = lambda ref: jax.nn.sigmoid(ref[...].astype(F32))
    m = (gate(ga_ref) * _dot(a_ref[...], wa_ref[...])
         + gate(gb_ref) * _dot(jnp.where(row0 < n_ctx, bc_ref[...], bl_ref[...]), wb_ref[...])
         + gate(gc_ref) * _dot(c_br, wc_ref[...]))
    out = _dot(m.astype(BF16), wo_ref[...])
    o_ref[...] = x_ref[...] + _gate_rows(m_ref, 2, row0, tt, n_ctx) * out


def merge(xs, mod, y, asig, P, a_br, b_ctx, b_lat, lnw, lnb, rk, ka, g2p, bd, wa, wb, wc, wo, n_ctx, tt=256):
    B, S, W = xs.shape
    tok = pl.BlockSpec((None, tt, W), lambda b, i: (b, i, 0))
    pcol = lambda col: pl.BlockSpec((None, tt, W), lambda b, i, col=col: (b, i, col))
    dtok = pl.BlockSpec((2, None, tt, W), lambda b, i: (0, b, i, 0))
    row = pl.BlockSpec((1, W), lambda b, i: (0, 0))
    wsq = pl.BlockSpec((W, W), lambda b, i: (0, 0))
    cr, cg = COL_RW // W, COL_GATE // W
    nct = n_ctx // tt
    kern = functools.partial(_merge_kernel, n_ctx=n_ctx)
    return pl.pallas_call(
        kern,
        grid=(B, S // tt),
        in_specs=[tok, pl.BlockSpec((None, 12, W), lambda b, i: (b, 0, 0)), tok, tok, dtok,
                  pcol(cr), pcol(cr + 1), pcol(cr + 2),
                  pl.BlockSpec((None, tt, 512), lambda b, i: (b, i, COL_TAIL // 512)),
                  pcol(cg), pcol(cg + 1), pcol(cg + 2), tok,
                  pl.BlockSpec((None, tt, W), lambda b, i: (b, jnp.minimum(i, nct - 1), 0)),
                  pl.BlockSpec((None, tt, W), lambda b, i: (b, jnp.maximum(i - nct, 0), 0)),
                  row, row, row, row,
                  pl.BlockSpec((2 * LANES, W), lambda b, i: (0, 0)),
                  pl.BlockSpec((GROUP_W, GROUP_W), lambda b, i: (0, 0)),
                  wsq, wsq, wsq, wsq],
        out_specs=tok,
        out_shape=jax.ShapeDtypeStruct((B, S, W), F32),
        compiler_params=_cparams(("arbitrary", "arbitrary")),
        name="merge",
    )(xs, mod, y[0], y[1], asig, P, P, P, P, P, P, P, a_br, b_ctx, b_lat, lnw, lnb, rk, ka, g2p, bd, wa, wb, wc, wo)


def _ffn_kernel(x_ref, m_ref, g_ref, wi_ref, wo_ref, o_ref, *, n_ctx, fc):
    tm = x_ref.shape[0]
    row0 = pl.program_id(1) * tm
    FF = wo_ref.shape[0]
    x = x_ref[...]
    h = _modulated_norm(x, g_ref[...], m_ref, 3, row0, n_ctx).astype(BF16)

    def gate_up(k):
        return (_dot(h, wi_ref[:, k * fc:(k + 1) * fc]), _dot(h, wi_ref[:, FF + k * fc:FF + (k + 1) * fc]))

    nxt = gate_up(0)
    acc = None
    for k in range(FF // fc):
        gt, up = nxt
        if (k + 1) * fc < FF:
            nxt = gate_up(k + 1)
        act = (gt * jax.nn.sigmoid(gt) * up).astype(BF16)
        part = _dot(act, wo_ref[k * fc:(k + 1) * fc, :])
        acc = part if acc is None else acc + part
    o_ref[...] = x + _gate_rows(m_ref, 5, row0, tm, n_ctx) * acc


def ffn(xs, mod, g, wi, wo, n_ctx, tm=384, fc=256):
    B, S, W = xs.shape
    FF = wo.shape[0]
    kern = functools.partial(_ffn_kernel, n_ctx=n_ctx, fc=fc)
    tok = pl.BlockSpec((None, tm, W), lambda b, i: (b, i, 0))
    return pl.pallas_call(
        kern,
        grid=(B, S // tm),
        in_specs=[tok, pl.BlockSpec((None, 12, W), lambda b, i: (b, 0, 0)),
                  pl.BlockSpec((1, W), lambda b, i: (0, 0)),
                  pl.BlockSpec((W, 2 * FF), lambda b, i: (0, 0)),
                  pl.BlockSpec((FF, W), lambda b, i: (0, 0))],
        out_specs=tok,
        out_shape=jax.ShapeDtypeStruct((B, S, W), F32),
        compiler_params=_cparams(("arbitrary", "arbitrary")),
        name="ffn",
    )(xs, mod, g, wi, wo)


def _rope_tables(n_rows, n_ctx):
    row = jnp.repeat(jnp.arange(n_rows), GRID_W).astype(F32)
    col = jnp.tile(jnp.arange(GRID_W), n_rows).astype(F32)
    axis_dim = DA_HEAD_DIM // 2
    inv_freq = ROPE_BASE ** (-jnp.arange(0, axis_dim, 2, dtype=F32) / axis_dim)
    ang = jnp.concatenate([row[:, None] * inv_freq, col[:, None] * inv_freq], axis=-1)
    ang = jnp.concatenate([ang, ang], axis=-1)
    cos, sin = jnp.cos(ang), jnp.sin(ang)
    cos = jnp.concatenate([jnp.ones((n_ctx, DA_HEAD_DIM), F32), cos], axis=0)
    sin = jnp.concatenate([jnp.zeros((n_ctx, DA_HEAD_DIM), F32), sin], axis=0)
    first = jnp.arange(DA_HEAD_DIM) < DA_HEAD_DIM // 2
    sin_a = jnp.where(first, -sin, 0.0)
    sin_b = jnp.where(first, 0.0, sin)
    dup = lambda t: jnp.concatenate([t, t], axis=-1)
    return dup(cos), dup(sin_a), dup(sin_b)


def kernel(x, c, ctx, c_ctx, ada_w, ada_b, norm1_g, norm2_g, w_in, gm_v_g, gm_ws, gm_bs,
           da_q_g, da_k_g, da_lambda, da_subln_g, rw_mu, rw_w0, rw_w2, rw_a0, rw_a2, rw_g2,
           rw_kk, rw_ka, rw_rk, rw_ln_w, rw_ln_b, w_br_a, w_br_b, w_br_c, w_o, ffn_wi, ffn_wo):
    B, T, D = x.shape
    n_ctx = ctx.shape[1]
    L = ada_w.shape[0]
    cos, sin_a, sin_b = _rope_tables(T // GRID_W, n_ctx)
    xs = jnp.concatenate([ctx, x], axis=1)

    cc = jnp.zeros((16, D), F32).at[:B].set(c).at[B].set(c_ctx)
    mods = ada_mod(cc, ada_w, ada_b)
    mod_l = mods[:, :B].reshape(L, B, 6, 1, D)
    mod_c = jnp.broadcast_to(mods[:, B].reshape(L, 1, 6, 1, D), (L, B, 6, 1, D))
    mod12 = jnp.concatenate([mod_c, mod_l], axis=3).reshape(L, B, 12, D)

    gi = jnp.arange(GROUP_W) // RW_HEAD
    bd = (gi[:, None] == gi[None, :]).astype(BF16)
    row = lambda t: t.reshape(1, -1)
    g0, d0, r0, t0 = 0, 2048, 5120, 5120 + 3072
    k0 = r0 + 3488

    for l in range(L):
        lam_init = 0.8 - 0.6 * math.exp(-0.3 * l)
        wl = w_in[l]
        w_cat = jnp.concatenate([wl[:, a:b].astype(BF16) for a, b in
                                 ((g0, d0), (d0, r0), (k0, wl.shape[1]), (r0, t0), (t0, k0))]
                                + [jnp.zeros((D, 96), BF16)], axis=1)
        mu = jnp.concatenate([jnp.zeros((COL_RW,), F32), rw_mu[l], jnp.zeros((96,), F32)]).reshape(1, -1)
        P = proj_in(xs, mod12[l], row(norm1_g[l]), w_cat, mu, n_ctx)

        bs_full = jnp.repeat(gm_bs[l].T, CHUNK, axis=1)
        a_br = gmlp(P, row(gm_v_g[l]), gm_ws[l].astype(BF16), bs_full)

        tile = lambda t, n: jnp.tile(t, n).reshape(1, -1)
        qn, kn, vb = qk_prep(P, cos, sin_a, sin_b, tile(da_q_g[l], 16), tile(da_k_g[l], 16), bd)
        if l < L - 1:
            b_ctx = attention_ctx(qn, kn, vb, da_lambda[l], row(da_subln_g[l]), n_ctx, lam_init)
        else:
            b_ctx = jnp.zeros((B, n_ctx, D), BF16)
        b_lat = attention_lat(qn, kn, vb, da_lambda[l], row(da_subln_g[l]), n_ctx, lam_init)

        zpad = jnp.zeros((DECAY_LORA, D), F32)
        w2p = jnp.stack([jnp.concatenate([rw_w2[l, 0], zpad]), jnp.concatenate([zpad, rw_w2[l, 1]])]).astype(BF16)
        a2p = jnp.stack([jnp.concatenate([rw_a2[l, 0], zpad]), jnp.concatenate([zpad, rw_a2[l, 1]])]).astype(BF16)
        kkn, lw, asig = rw_prep(P, rw_w0[l], w2p, rw_a0[l], a2p, row(rw_kk[l]), bd)
        y = rw_scan(P, kkn, lw, asig, row(rw_ka[l]), n_ctx)

        g2p = jnp.concatenate([rw_g2[l], jnp.zeros((2 * LANES - GATE_LORA, D), F32)]).astype(BF16)
        xs = merge(xs, mod12[l], y, asig, P, a_br, b_ctx, b_lat, row(rw_ln_w[l]), row(rw_ln_b[l]),
                   row(rw_rk[l]), row(rw_ka[l]), g2p, bd,
                   w_br_a[l].astype(BF16), w_br_b[l].astype(BF16), w_br_c[l].astype(BF16),
                   w_o[l].astype(BF16), n_ctx)
        xs = ffn(xs, mod12[l], row(norm2_g[l]), ffn_wi[l].astype(BF16), ffn_wo[l].astype(BF16), n_ctx)
    return xs[:, n_ctx:]
```

```python
import functools
import math

import jax
import jax.numpy as jnp
from jax import lax
from jax.experimental import pallas as pl
from jax.experimental.pallas import tpu as pltpu

D_MODEL = 1024
GRID_W = 64
CHUNK = 128
GM_GROUPS = 8
DA_HEAD_DIM = 64
ROPE_BASE = 10000.0
RW_HEAD = 64
DECAY_LORA = 64
GATE_LORA = 160
RMS_EPS = 1e-6
GN_EPS = 64e-5

LANES = 128
SUBLANES = 8
GROUP_W = 256
VMEM_LIMIT = 56 * 1024 * 1024

COL_DA = 2048
COL_GATE = 5120
COL_RW = 8192
COL_TAIL = 11264

ADA_TN = 1536
PROJ_TN = 512
PROJ_MB = 1024
PREP_TT = 768
MERGE_TT = 256
FFN_TM = 384
FFN_FC = 256
ATTN_TQ = 512
ATTN_TQ_CTX = 256
ATTN_KB = 256
ATTN_AHEAD = 3
RW_CHUNK = 64
RW_SUB = 4

BF16 = jnp.bfloat16
F32 = jnp.float32


def _dot(a, b):
    return jnp.dot(a, b, preferred_element_type=F32)


def _dot_nt(a, b):
    return lax.dot_general(a, b, (((1,), (1,)), ((), ())), preferred_element_type=F32)


def _split_dot(x, m):
    hi = x.astype(BF16)
    lo = (x - hi.astype(F32)).astype(BF16)
    return _dot(hi, m) + _dot(lo, m)


def _split3_dot_left(m, x):
    hi = x.astype(BF16)
    r1 = x - hi.astype(F32)
    mid = r1.astype(BF16)
    lo = (r1 - mid.astype(F32)).astype(BF16)
    return _dot(m, hi) + _dot(m, mid) + _dot(m, lo)


def _group_sum(x, bd):
    width = bd.shape[0]
    parts = [_split_dot(x[:, i:i + width], bd) for i in range(0, x.shape[1], width)]
    return parts[0] if len(parts) == 1 else jnp.concatenate(parts, axis=1)


def _cparams(sem):
    return pltpu.CompilerParams(dimension_semantics=sem, vmem_limit_bytes=VMEM_LIMIT)


def _ada_kernel(c_ref, w_ref, b_ref, o_ref):
    c = c_ref[...]
    s = c * jax.nn.sigmoid(c)
    o_ref[...] = jnp.dot(s, w_ref[...], precision=lax.Precision.HIGHEST,
                         preferred_element_type=F32) + b_ref[...]


def ada_mod(cc, ada_w, ada_b):
    L, D, N = ada_w.shape
    tn = ADA_TN
    return pl.pallas_call(
        _ada_kernel,
        grid=(L, N // tn),
        in_specs=[pl.BlockSpec((16, D), lambda l, j: (0, 0)),
                  pl.BlockSpec((None, D, tn), lambda l, j: (l, 0, j)),
                  pl.BlockSpec((None, 1, tn), lambda l, j: (l, 0, j))],
        out_specs=pl.BlockSpec((None, 16, tn), lambda l, j: (l, 0, j)),
        out_shape=jax.ShapeDtypeStruct((L, 16, N), F32),
        compiler_params=_cparams(("arbitrary", "arbitrary")),
        name="ada_mod",
    )(cc, ada_w, ada_b.reshape(L, 1, N))


def _modulated_norm(x, g, m_ref, piece, row0, n_ctx):
    rows = row0 + lax.broadcasted_iota(jnp.int32, (x.shape[0], 1), 0)
    is_lat = rows >= n_ctx
    sh = jnp.where(is_lat, m_ref[2 * piece + 1:2 * piece + 2, :], m_ref[2 * piece:2 * piece + 1, :])
    sc = jnp.where(is_lat, m_ref[2 * piece + 3:2 * piece + 4, :], m_ref[2 * piece + 2:2 * piece + 3, :])
    y = x * lax.rsqrt(jnp.mean(x * x, axis=-1, keepdims=True) + RMS_EPS) * g
    return y * (1.0 + sc) + sh


def _gate_rows(m_ref, piece, row0, nrows, n_ctx):
    rows = row0 + lax.broadcasted_iota(jnp.int32, (nrows, 1), 0)
    return jnp.where(rows >= n_ctx, m_ref[2 * piece + 1:2 * piece + 2, :], m_ref[2 * piece:2 * piece + 1, :])


def _proj_kernel(x_ref, m_ref, g_ref, w_ref, mu_ref, o_ref, h_ref, p_ref, *, n_ctx, rb, mb, rw_j0):
    S = x_ref.shape[0]
    j = pl.program_id(1)
    nblk = S // rb

    @pl.when(j == 0)
    def _():
        def body(i, carry):
            r0 = pl.multiple_of(i * rb, rb)
            x = x_ref[pl.ds(r0, rb), :]
            h_ref[pl.ds(r0, rb), :] = _modulated_norm(x, g_ref[...], m_ref, 0, r0, n_ctx).astype(BF16)
            return carry
        lax.fori_loop(0, nblk, body, 0)

    def project(store):
        store(0, n_ctx, False, _dot(h_ref[0:n_ctx, :], w_ref[...]))

        def body(i, carry):
            r0 = pl.multiple_of(n_ctx + i * mb, math.gcd(n_ctx, mb))
            store(r0, mb, True, _dot(h_ref[pl.ds(r0, mb), :], w_ref[...]))
            return carry
        lax.fori_loop(0, (S - n_ctx) // mb, body, 0)

    @pl.when(j < rw_j0)
    def _():
        def store(r0, rows, is_latent, val):
            o_ref[pl.ds(r0, rows), :] = val.astype(o_ref.dtype)
        project(store)

    @pl.when(j >= rw_j0)
    def _():
        pad = SUBLANES
        zero = jnp.zeros((pad, p_ref.shape[1]), F32)
        p_ref[0:pad, :] = zero
        p_ref[n_ctx + pad:n_ctx + 2 * pad, :] = zero
        p_ref[S + 2 * pad:S + 3 * pad, :] = zero

        def store(r0, rows, is_latent, val):
            p_ref[pl.ds(r0 + (2 * pad if is_latent else pad), rows), :] = val
        project(store)
        c_cur = 1.0 - mu_ref[...]
        c_nb = 0.5 * mu_ref[...]
        for i in range(nblk):
            r0 = i * rb
            src = r0 + (2 * pad if r0 >= n_ctx else pad)
            nb = p_ref[src - 1:src - 1 + rb, :] + p_ref[src + 1:src + 1 + rb, :]
            o_ref[r0:r0 + rb, :] = (p_ref[src:src + rb, :] * c_cur + nb * c_nb).astype(o_ref.dtype)


def proj_in(xs, mod, g, w, mu, n_ctx):
    B, S, D = xs.shape
    NP = w.shape[1]
    tn = PROJ_TN
    kern = functools.partial(_proj_kernel, n_ctx=n_ctx, rb=n_ctx, mb=min(PROJ_MB, S - n_ctx), rw_j0=COL_RW // tn)
    return pl.pallas_call(
        kern,
        grid=(B, NP // tn),
        in_specs=[pl.BlockSpec((None, S, D), lambda b, j: (b, 0, 0)),
                  pl.BlockSpec((None, 12, D), lambda b, j: (b, 0, 0)),
                  pl.BlockSpec((1, D), lambda b, j: (0, 0)),
                  pl.BlockSpec((D, tn), lambda b, j: (0, j)),
                  pl.BlockSpec((1, tn), lambda b, j: (0, j))],
        out_specs=pl.BlockSpec((None, S, tn), lambda b, j: (b, 0, j)),
        out_shape=jax.ShapeDtypeStruct((B, S, NP), BF16),
        scratch_shapes=[pltpu.VMEM((S, D), BF16), pltpu.VMEM((S + 3 * SUBLANES, tn), F32)],
        compiler_params=_cparams(("arbitrary", "arbitrary")),
        name="proj_in",
    )(xs, mod, g, w, mu)


def _gelu(x):
    return 0.5 * x * (1.0 + lax.erf(x * (1.0 / math.sqrt(2.0))))


def _gmlp_kernel(p_ref, vg_ref, ws_ref, bs_ref, o_ref):
    W = vg_ref.shape[1]
    gd = W // GM_GROUPS
    for c in range(p_ref.shape[0] // CHUNK):
        rows = slice(c * CHUNK, (c + 1) * CHUNK)
        u = _gelu(p_ref[rows, :W].astype(F32))
        v = _gelu(p_ref[rows, W:].astype(F32))
        v = v * lax.rsqrt(jnp.mean(v * v, axis=-1, keepdims=True) + RMS_EPS) * vg_ref[...]
        vb = v.astype(BF16)
        for g in range(GM_GROUPS):
            sl = slice(g * gd, (g + 1) * gd)
            f = _dot(ws_ref[g], vb[:, sl]) + bs_ref[:, sl]
            o_ref[rows, sl] = (u[:, sl] * f).astype(o_ref.dtype)


def gmlp(P, vg, ws, bs_full, tt=PREP_TT):
    B, S, _ = P.shape
    W = vg.shape[1]
    return pl.pallas_call(
        _gmlp_kernel,
        grid=(B, S // tt),
        in_specs=[pl.BlockSpec((None, tt, 2 * W), lambda b, i: (b, i, 0)),
                  pl.BlockSpec((1, W), lambda b, i: (0, 0)),
                  pl.BlockSpec((GM_GROUPS, CHUNK, CHUNK), lambda b, i: (0, 0, 0)),
                  pl.BlockSpec((CHUNK, W), lambda b, i: (0, 0))],
        out_specs=pl.BlockSpec((None, tt, W), lambda b, i: (b, i, 0)),
        out_shape=jax.ShapeDtypeStruct((B, S, W), BF16),
        compiler_params=_cparams(("arbitrary", "arbitrary")),
        name="gmlp",
    )(P, vg, ws, bs_full)


def _qk_kernel(q_ref, k_ref, v_ref, cos_ref, sa_ref, sb_ref, qg_ref, kg_ref, bd_ref,
               qo_ref, ko_ref, vo_ref):
    W = q_ref.shape[1]
    reps = W // LANES
    bd = bd_ref[...]
    sw = bd.shape[0]
    cos = jnp.tile(cos_ref[...], (1, sw // LANES))
    sin_a = jnp.tile(sa_ref[...], (1, sw // LANES))
    sin_b = jnp.tile(sb_ref[...], (1, sw // LANES))
    half = DA_HEAD_DIM // 2

    def prep(src, g_ref, dst, scale):
        for c0 in range(0, W, sw):
            t = src[:, c0:c0 + sw].astype(F32)
            ms = _group_sum(t * t, bd) * (1.0 / DA_HEAD_DIM)
            t = t * lax.rsqrt(ms + RMS_EPS) * g_ref[:, c0:c0 + sw]
            t = t * cos + pltpu.roll(t, sw - half, 1) * sin_a + pltpu.roll(t, half, 1) * sin_b
            dst[:, c0:c0 + sw] = (t * scale if scale != 1.0 else t).astype(dst.dtype)

    prep(q_ref, qg_ref, qo_ref, DA_HEAD_DIM ** -0.5 * math.log2(math.e))
    prep(k_ref, kg_ref, ko_ref, 1.0)
    for h in range(reps):
        vo_ref[h] = v_ref[:, h * LANES:(h + 1) * LANES].astype(F32).T.astype(BF16)


def qk_prep(P, cos, sin_a, sin_b, qg, kg, bd, tt=PREP_TT):
    B, S, _ = P.shape
    W = D_MODEL
    c0 = COL_DA // W
    tok = lambda c: pl.BlockSpec((None, tt, W), lambda b, i, c=c: (b, i, c))
    tab = pl.BlockSpec((tt, LANES), lambda b, i: (i, 0))
    row = pl.BlockSpec((1, W), lambda b, i: (0, 0))
    out = pl.BlockSpec((None, tt, W), lambda b, i: (b, i, 0))
    shp = jax.ShapeDtypeStruct((B, S, W), BF16)
    return pl.pallas_call(
        _qk_kernel,
        grid=(B, S // tt),
        in_specs=[tok(c0), tok(c0 + 1), tok(c0 + 2), tab, tab, tab, row, row,
                  pl.BlockSpec((GROUP_W, GROUP_W), lambda b, i: (0, 0))],
        out_specs=[out, out, pl.BlockSpec((None, W // LANES, LANES, tt), lambda b, i: (b, 0, 0, i))],
        out_shape=[shp, shp, jax.ShapeDtypeStruct((B, W // LANES, LANES, S), BF16)],
        compiler_params=_cparams(("arbitrary", "arbitrary")),
        name="qk_prep",
    )(P, P, P, cos, sin_a, sin_b, qg, kg, bd)


def _stack_components(q):
    lane = lax.broadcasted_iota(jnp.int32, q.shape, 1)
    zero = jnp.zeros_like(q)
    return jnp.concatenate([jnp.where(lane < DA_HEAD_DIM, q, zero),
                            jnp.where(lane >= DA_HEAD_DIM, q, zero)], axis=0)


def _attn_finish(acc, l, lam_ref, sg_ref, o_ref, lam_init):
    tq = o_ref.shape[0]
    lp = lam_ref[...]
    lam = (jnp.exp(jnp.sum(lp[0:1] * lp[1:2], axis=-1, keepdims=True))
           - jnp.exp(jnp.sum(lp[2:3] * lp[3:4], axis=-1, keepdims=True)) + lam_init)
    inv = 1.0 / l
    o = acc[:, :tq] * inv[:, :tq] - acc[:, tq:] * (lam * inv[:, tq:])
    o = o * lax.rsqrt(jnp.mean(o * o, axis=0, keepdims=True) + RMS_EPS)
    o_ref[...] = (o.T * sg_ref[...] * (1.0 - lam_init)).astype(o_ref.dtype)


def _attn_ctx_kernel(q_ref, k_ref, v_ref, lam_ref, sg_ref, o_ref, *, lam_init):
    qs = _stack_components(q_ref[...])
    m = l = acc = None
    nblk = k_ref.shape[0] // ATTN_KB
    blk = lambda j: slice(j * ATTN_KB, (j + 1) * ATTN_KB)
    scores = lambda j: _dot_nt(k_ref[blk(j), :], qs)
    queue = [scores(j) for j in range(min(ATTN_AHEAD, nblk))]
    for j in range(nblk):
        if j + ATTN_AHEAD < nblk:
            queue.append(scores(j + ATTN_AHEAD))
        s = queue.pop(0)
        m_blk = jnp.max(s, axis=0, keepdims=True)
        m_new = m_blk if m is None else jnp.maximum(m, m_blk)
        e = jnp.exp2(s - m_new)
        pv = _dot(v_ref[:, blk(j)], e.astype(BF16))
        if m is None:
            l, acc = jnp.sum(e, axis=0, keepdims=True), pv
        else:
            alpha = jnp.exp2(m - m_new)
            l = alpha * l + jnp.sum(e, axis=0, keepdims=True)
            acc = alpha * acc + pv
        m = m_new
    _attn_finish(acc, l, lam_ref, sg_ref, o_ref, lam_init)


def attention_ctx(qn, kn, vt, lam_p, sg, n_ctx, lam_init, tq=ATTN_TQ_CTX):
    B, S, W = qn.shape
    H = W // LANES
    kern = functools.partial(_attn_ctx_kernel, lam_init=lam_init)
    return pl.pallas_call(
        kern,
        grid=(B, H, n_ctx // tq),
        in_specs=[pl.BlockSpec((None, tq, LANES), lambda b, h, i: (b, i, h)),
                  pl.BlockSpec((None, n_ctx, LANES), lambda b, h, i: (b, 0, h)),
                  pl.BlockSpec((None, None, LANES, n_ctx), lambda b, h, i: (b, h, 0, 0)),
                  pl.BlockSpec((4, DA_HEAD_DIM), lambda b, h, i: (0, 0)),
                  pl.BlockSpec((1, LANES), lambda b, h, i: (0, 0))],
        out_specs=pl.BlockSpec((None, tq, LANES), lambda b, h, i: (b, i, h)),
        out_shape=jax.ShapeDtypeStruct((B, n_ctx, W), BF16),
        compiler_params=_cparams(("arbitrary", "arbitrary", "arbitrary")),
        name="attention_ctx",
    )(qn, kn, vt, lam_p, sg)


def _attn_lat_kernel(*refs, lam_init, n_tiles, n_qsub):
    q_refs = refs[:n_qsub]
    k_ref, v_ref, lam_ref, sg_ref, o_ref, s_ref, m_ref = refs[n_qsub:]
    t = pl.program_id(0)
    nblk = k_ref.shape[0] // ATTN_KB
    blk = lambda j: slice(j * ATTN_KB, (j + 1) * ATTN_KB)

    def run(scores_slot, softmax_slot):
        if scores_slot is not None:
            qs = _stack_components(jnp.concatenate([r[...] for r in q_refs], axis=0))
        m_run = l = acc = None
        if softmax_slot is not None:
            m_prev = m_ref[softmax_slot]
        for j in range(nblk):
            if scores_slot is not None:
                s = _dot_nt(k_ref[blk(j), :], qs)
                s_ref[scores_slot, blk(j), :] = s
                for r in range(0, ATTN_KB, 8):
                    m_run = s[r:r + 8] if m_run is None else jnp.maximum(m_run, s[r:r + 8])
            if softmax_slot is not None:
                e = jnp.exp2(s_ref[softmax_slot, blk(j), :] - m_prev)
                pv = _dot(v_ref[:, blk(j)], e.astype(BF16))
                lsum = jnp.sum(e, axis=0, keepdims=True)
                l, acc = (lsum, pv) if l is None else (l + lsum, acc + pv)
        if scores_slot is not None:
            m_ref[scores_slot] = jnp.max(m_run, axis=0, keepdims=True)
        if softmax_slot is not None:
            _attn_finish(acc, l, lam_ref, sg_ref, o_ref, lam_init)

    @pl.when(t == 0)
    def _():
        run(0, None)

    for par in range(2):
        @pl.when((t > 0) & (t < n_tiles) & (t % 2 == par))
        def _():
            run(par, 1 - par)

    @pl.when(t == n_tiles)
    def _():
        run(None, (n_tiles - 1) % 2)


def attention_lat(qn, kn, vt, lam_p, sg, n_ctx, lam_init, tq=ATTN_TQ):
    B, S, W = qn.shape
    H = W // LANES
    nq = (S - n_ctx) // tq
    n_tiles = B * H * nq
    qb = math.gcd(tq, n_ctx)
    n_qsub = tq // qb
    kern = functools.partial(_attn_lat_kernel, lam_init=lam_init, n_tiles=n_tiles, n_qsub=n_qsub)

    def tile(n):
        return n // (H * nq), n % nq, (n // nq) % H

    def scored(t):
        return tile(jnp.minimum(t, n_tiles - 1))

    def finished(t):
        return tile(jnp.maximum(t - 1, 0))

    return pl.pallas_call(
        kern,
        grid=(n_tiles + 1,),
        in_specs=[*[pl.BlockSpec((None, qb, LANES), lambda t, s=s: (
                        scored(t)[0], n_ctx // qb + scored(t)[1] * n_qsub + s, scored(t)[2]))
                    for s in range(n_qsub)],
                  pl.BlockSpec((None, S, LANES), lambda t: (scored(t)[0], 0, scored(t)[2])),
                  pl.BlockSpec((None, None, LANES, S), lambda t: (finished(t)[0], finished(t)[2], 0, 0)),
                  pl.BlockSpec((4, DA_HEAD_DIM), lambda t: (0, 0)),
                  pl.BlockSpec((1, LANES), lambda t: (0, 0))],
        out_specs=pl.BlockSpec((None, tq, LANES), lambda t: finished(t)),
        out_shape=jax.ShapeDtypeStruct((B, S - n_ctx, W), BF16),
        scratch_shapes=[pltpu.VMEM((2, S, 2 * tq), F32), pltpu.VMEM((2, 1, 2 * tq), F32)],
        compiler_params=_cparams(("arbitrary",)),
        name="attention_lat",
    )(*([qn] * n_qsub), kn, vt, lam_p, sg)


def _rw_prep_kernel(zk_ref, zt_ref, w0_ref, w2_ref, a0_ref, a2_ref, kk_ref, bd_ref,
                    kkn_ref, lw_ref, as_ref):
    tw = jnp.tanh(zt_ref[:, 0:LANES].astype(F32)).astype(BF16)
    za = zt_ref[:, LANES:2 * LANES]
    for d in range(2):
        x = w0_ref[d:d + 1, :] + _dot(tw, w2_ref[d])
        lw_ref[d] = -math.exp(-0.5) * jax.nn.sigmoid(x)
        as_ref[d] = jax.nn.sigmoid(a0_ref[d:d + 1, :] + _dot(za, a2_ref[d])).astype(as_ref.dtype)
    kk = zk_ref[...].astype(F32) * kk_ref[...]
    nrm = jnp.sqrt(_group_sum(kk * kk, bd_ref[...]))
    kkn_ref[...] = (kk / jnp.maximum(nrm, 1e-12)).astype(kkn_ref.dtype)


def rw_prep(P, w0, w2p, a0, a2p, kkp, bd, tt=PREP_TT):
    B, S, _ = P.shape
    W = D_MODEL
    tokw = pl.BlockSpec((None, tt, W), lambda b, i: (b, i, COL_RW // W + 1))
    tail = pl.BlockSpec((None, tt, 512), lambda b, i: (b, i, COL_TAIL // 512))
    full = lambda shape: pl.BlockSpec(shape, lambda b, i: (0,) * len(shape))
    out2 = pl.BlockSpec((2, None, tt, W), lambda b, i: (0, b, i, 0))
    return pl.pallas_call(
        _rw_prep_kernel,
        grid=(B, S // tt),
        in_specs=[tokw, tail, full((2, W)), full((2, LANES, W)), full((2, W)), full((2, LANES, W)),
                  full((1, W)), full((GROUP_W, GROUP_W))],
        out_specs=[pl.BlockSpec((None, tt, W), lambda b, i: (b, i, 0)), out2, out2],
        out_shape=[jax.ShapeDtypeStruct((B, S, W), BF16),
                   jax.ShapeDtypeStruct((2, B, S, W), F32),
                   jax.ShapeDtypeStruct((2, B, S, W), BF16)],
        compiler_params=_cparams(("arbitrary", "arbitrary")),
        name="rw_prep",
    )(P, P, w0, w2p, a0, a2p, kkp, bd)


def _rw_scan_kernel(r_ref, zk_ref, v_ref, kkn_ref, lw_ref, as_ref, ka_ref, y_ref, h_ref):
    @pl.when(pl.program_id(1) == 0)
    def _():
        h_ref[...] = jnp.zeros_like(h_ref)

    C = RW_CHUNK
    n_sub = lw_ref[0].shape[0] // C
    for sub in range(n_sub):
        rows = (pl.ds(sub * C, C), pl.ds((n_sub - 1 - sub) * C, C))
        view = lambda pair: tuple(pair[d].at[rows[d]] for d in range(2))
        _scan_chunk(view(r_ref), view(zk_ref), view(v_ref), view(kkn_ref), view(lw_ref), view(as_ref),
                    ka_ref, view(y_ref), h_ref)


def _scan_chunk(r_ref, zk_ref, v_ref, kkn_ref, lw_ref, as_ref, ka_ref, y_ref, h_ref):
    C = RW_CHUNK
    R2 = 2 * C
    n_pairs = ka_ref.shape[1] // LANES

    ri = lax.broadcasted_iota(jnp.int32, (C, C), 0)
    ci = lax.broadcasted_iota(jnp.int32, (C, C), 1)
    rt = lax.broadcasted_iota(jnp.int32, (C, LANES), 0)
    lane = lax.broadcasted_iota(jnp.int32, (C, LANES), 1)
    cs = lane & (C - 1)
    levels = int(math.log2(C))
    same_blk = [(rt >> k) == (cs >> k) for k in range(levels + 1)]
    joins = [same_blk[k + 1] & ~same_blk[k] for k in range(levels)]
    eye = jnp.where(rt == cs, 1.0, 0.0)
    first = lane < RW_HEAD
    hr = lax.broadcasted_iota(jnp.int32, (LANES, LANES), 0)
    hc = lax.broadcasted_iota(jnp.int32, (LANES, LANES), 1)
    same_head = (hr >= RW_HEAD) == (hc >= RW_HEAD)
    ka = ka_ref[...]

    def stack(x):
        return jnp.concatenate([jnp.where(first, x, 0.0), jnp.where(first, 0.0, x)], axis=0)

    chains = []
    for d in range(2):
        before_c = (ci <= ri) if d == 0 else (ci >= ri)
        strict = (cs < rt) if d == 0 else (cs > rt)
        incl = (cs <= rt) if d == 0 else (cs >= rt)
        lw = lw_ref[d][...]
        L = _split3_dot_left(jnp.where(before_c, 1.0, 0.0).astype(BF16), lw)
        ltot = jnp.sum(lw, axis=0, keepdims=True)
        lmid = 0.5 * ltot
        e_r = jnp.exp(L - lmid)
        e_a = jnp.exp(L - lw - lmid)
        e_b = jnp.exp(lmid - L)
        e_mid = jnp.exp(lmid)
        asig = as_ref[d][...].astype(F32)
        kkn = kkn_ref[d][...].astype(F32)
        kd = zk_ref[d][...].astype(F32) * (1.0 + (asig - 1.0) * ka)
        a_in = -kkn * e_a
        r_in = r_ref[d][...].astype(F32) * e_r
        b_in = kkn * asig * e_b
        k_in = kd * e_b
        a_h = a_in * e_mid
        r_h = r_in * e_mid
        b_o = b_in * e_mid
        k_o = k_in * e_mid
        wtot = jnp.exp(ltot)
        vv = v_ref[d][...].astype(F32)
        for p in range(n_pairs):
            sl = slice(p * LANES, (p + 1) * LANES)
            chains.append(dict(
                d=d, p=p, sl=sl, strict=strict, incl=incl, wtot=wtot[:, sl],
                AR=jnp.concatenate([a_in[:, sl], r_in[:, sl]], axis=0).astype(BF16),
                BK=jnp.concatenate([stack(b_in[:, sl]), stack(k_in[:, sl])], axis=0).astype(BF16),
                ARh=jnp.concatenate([a_h[:, sl], r_h[:, sl]], axis=0).astype(BF16),
                BKo=jnp.concatenate([b_o[:, sl], k_o[:, sl]], axis=0).astype(BF16),
                V=vv[:, sl]))

    for ch in chains:
        G1 = _dot_nt(ch["AR"], ch["BK"])
        ch["Aab"] = jnp.where(ch["strict"], G1[:C, :R2], 0.0)
        ch["Aak"] = jnp.where(ch["strict"], G1[:C, R2:], 0.0).astype(BF16)
        ch["Ar"] = jnp.concatenate([jnp.where(ch["incl"], G1[C:, :R2], 0.0),
                                    jnp.where(ch["incl"], G1[C:, R2:], 0.0)], axis=1).astype(BF16)
        ch["H"] = h_ref[ch["d"], ch["p"]]
        ch["T"] = eye + jnp.where(joins[0], ch["Aab"], 0.0)
        ch["As"] = stack(ch["Aab"]).astype(BF16)
    for ch in chains:
        ch["G2"] = _dot_nt(ch["ARh"], ch["H"].astype(BF16))
        ch["Vs"] = stack(ch["V"]).astype(BF16)
        ch["rhs"] = ch["G2"][:C] + _dot(ch["Aak"], ch["Vs"])
    for join in joins[1:]:
        for ch in chains:
            ch["TA"] = _dot(ch["T"].astype(BF16), ch["As"]).astype(BF16)
        for ch in chains:
            ch["T"] = ch["T"] + jnp.where(join, _dot(ch["TA"], stack(ch["T"]).astype(BF16)), 0.0)
    for ch in chains:
        ch["U"] = _dot(ch["T"].astype(BF16), stack(ch["rhs"]).astype(BF16))
    for ch in chains:
        UVs = jnp.concatenate([stack(ch["U"]).astype(BF16), ch["Vs"]], axis=0)
        y_ref[ch["d"]][:, ch["sl"]] = ch["G2"][C:] + _dot(ch["Ar"], UVs)
    for ch in chains:
        UVt = jnp.concatenate([ch["U"], ch["V"]], axis=0).T.astype(BF16)
        upd = jnp.where(same_head, _dot(UVt, ch["BKo"]), 0.0)
        h_ref[ch["d"], ch["p"]] = ch["H"] * ch["wtot"] + upd


def rw_scan(P, kkn, lw, asig, ka, n_ctx):
    B, S, _ = P.shape
    W = D_MODEL
    C = RW_CHUNK * RW_SUB
    nc = S // C
    ncc = n_ctx // C

    def rev(c):
        return jnp.where(c < ncc, ncc - 1 - c, nc - 1 + ncc - c)

    def both(make):
        return [make(lambda c: c), make(rev)]

    pcol = lambda col: both(lambda f: pl.BlockSpec((None, C, W), lambda b, c, f=f: (b, f(c), col)))
    tok = both(lambda f: pl.BlockSpec((None, C, W), lambda b, c, f=f: (b, f(c), 0)))
    dtok = [pl.BlockSpec((None, None, C, W), lambda b, c: (0, b, c, 0)),
            pl.BlockSpec((None, None, C, W), lambda b, c: (1, b, rev(c), 0))]
    c0 = COL_RW // W

    def kern(rf, rb_, kf, kb_, vf, vb_, nf, nb_, lwf, lwb, asf, asb, ka_ref, yf, yb, h_ref):
        _rw_scan_kernel((rf, rb_), (kf, kb_), (vf, vb_), (nf, nb_), (lwf, lwb), (asf, asb), ka_ref,
                        (yf, yb), h_ref)

    return pl.pallas_call(
        kern,
        grid=(B, nc),
        in_specs=[*pcol(c0), *pcol(c0 + 1), *pcol(c0 + 2), *tok, *dtok, *dtok,
                  pl.BlockSpec((1, W), lambda b, c: (0, 0))],
        out_specs=[pl.BlockSpec((None, C, W), lambda b, c: (b, c, 0)),
                   pl.BlockSpec((None, C, W), lambda b, c: (b, rev(c), 0))],
        out_shape=[jax.ShapeDtypeStruct((B, S, W), F32)] * 2,
        scratch_shapes=[pltpu.VMEM((2, W // LANES, LANES, LANES), F32)],
        compiler_params=_cparams(("arbitrary", "arbitrary")),
        name="rw_scan",
    )(P, P, P, P, P, P, kkn, kkn, lw, lw, asig, asig, ka)


def _merge_kernel(x_ref, m_ref, yf_ref, yb_ref, as_ref, r_ref, zk_ref, v_ref, zt_ref, ga_ref, gb_ref, gc_ref,
                  a_ref, bc_ref, bl_ref, lnw_ref, lnb_ref, rk_ref, ka_ref, g2_ref, bd_ref,
                  wa_ref, wb_ref, wc_ref, wo_ref, o_ref, *, n_ctx):
    tt = x_ref.shape[0]
    row0 = pl.program_id(1) * tt
    bd = bd_ref[...]
    inv_n = 1.0 / RW_HEAD
    y = yf_ref[...] + yb_ref[...]
    mean = _group_sum(y, bd) * inv_n
    yc = y - mean
    var = _group_sum(yc * yc, bd) * inv_n
    yn = yc * lax.rsqrt(var + GN_EPS) * lnw_ref[...] + lnb_ref[...]
    zk = zk_ref[...].astype(F32)
    ka = ka_ref[...]
    kd_sum = (zk * (1.0 + (as_ref[0].astype(F32) - 1.0) * ka)
              + zk * (1.0 + (as_ref[1].astype(F32) - 1.0) * ka))
    vv = v_ref[...].astype(F32)
    bonus = _group_sum(r_ref[...].astype(F32) * kd_sum * rk_ref[...], bd) * vv
    g = _dot(jax.nn.sigmoid(zt_ref[:, 2 * LANES:].astype(F32)).astype(BF16), g2_ref[...])
    c_br = ((yn + bonus) * g).astype(BF16)
    gate = lambda ref: jax.nn.sigmoid(ref[...].astype(F32))
    m = (gate(ga_ref) * _dot(a_ref[...], wa_ref[...])
         + gate(gb_ref) * _dot(jnp.where(row0 < n_ctx, bc_ref[...], bl_ref[...]), wb_ref[...])
         + gate(gc_ref) * _dot(c_br, wc_ref[...]))
    out = _dot(m.astype(BF16), wo_ref[...])
    o_ref[...] = x_ref[...] + _gate_rows(m_ref, 2, row0, tt, n_ctx) * out


def merge(xs, mod, y, asig, P, a_br, b_ctx, b_lat, lnw, lnb, rk, ka, g2p, bd, wa, wb, wc, wo, n_ctx, tt=MERGE_TT):
    B, S, W = xs.shape
    tok = pl.BlockSpec((None, tt, W), lambda b, i: (b, i, 0))
    pcol = lambda col: pl.BlockSpec((None, tt, W), lambda b, i, col=col: (b, i, col))
    dtok = pl.BlockSpec((2, None, tt, W), lambda b, i: (0, b, i, 0))
    row = pl.BlockSpec((1, W), lambda b, i: (0, 0))
    wsq = pl.BlockSpec((W, W), lambda b, i: (0, 0))
    cr, cg = COL_RW // W, COL_GATE // W
    nct = n_ctx // tt
    kern = functools.partial(_merge_kernel, n_ctx=n_ctx)
    return pl.pallas_call(
        kern,
        grid=(B, S // tt),
        in_specs=[tok, pl.BlockSpec((None, 12, W), lambda b, i: (b, 0, 0)), tok, tok, dtok,
                  pcol(cr), pcol(cr + 1), pcol(cr + 2),
                  pl.BlockSpec((None, tt, 512), lambda b, i: (b, i, COL_TAIL // 512)),
                  pcol(cg), pcol(cg + 1), pcol(cg + 2), tok,
                  pl.BlockSpec((None, tt, W), lambda b, i: (b, jnp.minimum(i, nct - 1), 0)),
                  pl.BlockSpec((None, tt, W), lambda b, i: (b, jnp.maximum(i - nct, 0), 0)),
                  row, row, row, row,
                  pl.BlockSpec((2 * LANES, W), lambda b, i: (0, 0)),
                  pl.BlockSpec((GROUP_W, GROUP_W), lambda b, i: (0, 0)),
                  wsq, wsq, wsq, wsq],
        out_specs=tok,
        out_shape=jax.ShapeDtypeStruct((B, S, W), F32),
        compiler_params=_cparams(("arbitrary", "arbitrary")),
        name="merge",
    )(xs, mod, y[0], y[1], asig, P, P, P, P, P, P, P, a_br, b_ctx, b_lat, lnw, lnb, rk, ka, g2p, bd, wa, wb, wc, wo)


def _ffn_kernel(x_ref, m_ref, g_ref, wi_ref, wo_ref, o_ref, *, n_ctx, fc):
    tm = x_ref.shape[0]
    row0 = pl.program_id(1) * tm
    FF = wo_ref.shape[0]
    x = x_ref[...]
    h = _modulated_norm(x, g_ref[...], m_ref, 3, row0, n_ctx).astype(BF16)

    def gate_up(k):
        return (_dot(h, wi_ref[:, k * fc:(k + 1) * fc]), _dot(h, wi_ref[:, FF + k * fc:FF + (k + 1) * fc]))

    nxt = gate_up(0)
    acc = None
    for k in range(FF // fc):
        gt, up = nxt
        if (k + 1) * fc < FF:
            nxt = gate_up(k + 1)
        act = (gt * jax.nn.sigmoid(gt) * up).astype(BF16)
        part = _dot(act, wo_ref[k * fc:(k + 1) * fc, :])
        acc = part if acc is None else acc + part
    o_ref[...] = x + _gate_rows(m_ref, 5, row0, tm, n_ctx) * acc


def ffn(xs, mod, g, wi, wo, n_ctx, tm=FFN_TM, fc=FFN_FC):
    B, S, W = xs.shape
    FF = wo.shape[0]
    kern = functools.partial(_ffn_kernel, n_ctx=n_ctx, fc=fc)
    tok = pl.BlockSpec((None, tm, W), lambda b, i: (b, i, 0))
    return pl.pallas_call(
        kern,
        grid=(B, S // tm),
        in_specs=[tok, pl.BlockSpec((None, 12, W), lambda b, i: (b, 0, 0)),
                  pl.BlockSpec((1, W), lambda b, i: (0, 0)),
                  pl.BlockSpec((W, 2 * FF), lambda b, i: (0, 0)),
                  pl.BlockSpec((FF, W), lambda b, i: (0, 0))],
        out_specs=tok,
        out_shape=jax.ShapeDtypeStruct((B, S, W), F32),
        compiler_params=_cparams(("arbitrary", "arbitrary")),
        name="ffn",
    )(xs, mod, g, wi, wo)


def _rope_tables(n_rows, n_ctx):
    row = jnp.repeat(jnp.arange(n_rows), GRID_W).astype(F32)
    col = jnp.tile(jnp.arange(GRID_W), n_rows).astype(F32)
    axis_dim = DA_HEAD_DIM // 2
    inv_freq = ROPE_BASE ** (-jnp.arange(0, axis_dim, 2, dtype=F32) / axis_dim)
    ang = jnp.concatenate([row[:, None] * inv_freq, col[:, None] * inv_freq], axis=-1)
    ang = jnp.concatenate([ang, ang], axis=-1)
    cos, sin = jnp.cos(ang), jnp.sin(ang)
    cos = jnp.concatenate([jnp.ones((n_ctx, DA_HEAD_DIM), F32), cos], axis=0)
    sin = jnp.concatenate([jnp.zeros((n_ctx, DA_HEAD_DIM), F32), sin], axis=0)
    first = jnp.arange(DA_HEAD_DIM) < DA_HEAD_DIM // 2
    sin_a = jnp.where(first, -sin, 0.0)
    sin_b = jnp.where(first, 0.0, sin)
    dup = lambda t: jnp.concatenate([t, t], axis=-1)
    return dup(cos), dup(sin_a), dup(sin_b)


def kernel(x, c, ctx, c_ctx, ada_w, ada_b, norm1_g, norm2_g, w_in, gm_v_g, gm_ws, gm_bs,
           da_q_g, da_k_g, da_lambda, da_subln_g, rw_mu, rw_w0, rw_w2, rw_a0, rw_a2, rw_g2,
           rw_kk, rw_ka, rw_rk, rw_ln_w, rw_ln_b, w_br_a, w_br_b, w_br_c, w_o, ffn_wi, ffn_wo):
    B, T, D = x.shape
    n_ctx = ctx.shape[1]
    L = ada_w.shape[0]
    cos, sin_a, sin_b = _rope_tables(T // GRID_W, n_ctx)
    xs = jnp.concatenate([ctx, x], axis=1)

    cc = jnp.zeros((16, D), F32).at[:B].set(c).at[B].set(c_ctx)
    mods = ada_mod(cc, ada_w, ada_b)
    mod_l = mods[:, :B].reshape(L, B, 6, 1, D)
    mod_c = jnp.broadcast_to(mods[:, B].reshape(L, 1, 6, 1, D), (L, B, 6, 1, D))
    mod12 = jnp.concatenate([mod_c, mod_l], axis=3).reshape(L, B, 12, D)

    gi = jnp.arange(GROUP_W) // RW_HEAD
    bd = (gi[:, None] == gi[None, :]).astype(BF16)
    row = lambda t: t.reshape(1, -1)
    d0, r0 = 2 * D, 5 * D
    t0 = r0 + 3 * D
    k0 = r0 + rw_mu.shape[1]
    tail_pad = COL_TAIL + 4 * LANES - COL_RW - rw_mu.shape[1]

    for l in range(L):
        lam_init = 0.8 - 0.6 * math.exp(-0.3 * l)
        wl = w_in[l]
        w_cat = jnp.concatenate([wl[:, a:b].astype(BF16) for a, b in
                                 ((0, d0), (d0, r0), (k0, wl.shape[1]), (r0, t0), (t0, k0))]
                                + [jnp.zeros((D, tail_pad), BF16)], axis=1)
        mu = jnp.concatenate([jnp.zeros((COL_RW,), F32), rw_mu[l], jnp.zeros((tail_pad,), F32)]).reshape(1, -1)
        P = proj_in(xs, mod12[l], row(norm1_g[l]), w_cat, mu, n_ctx)

        bs_full = jnp.repeat(gm_bs[l].T, CHUNK, axis=1)
        a_br = gmlp(P, row(gm_v_g[l]), gm_ws[l].astype(BF16), bs_full)

        tile = lambda t, n: jnp.tile(t, n).reshape(1, -1)
        qn, kn, vb = qk_prep(P, cos, sin_a, sin_b, tile(da_q_g[l], 16), tile(da_k_g[l], 16), bd)
        if l < L - 1:
            b_ctx = attention_ctx(qn, kn, vb, da_lambda[l], row(da_subln_g[l]), n_ctx, lam_init)
        else:
            b_ctx = jnp.zeros((B, n_ctx, D), BF16)
        b_lat = attention_lat(qn, kn, vb, da_lambda[l], row(da_subln_g[l]), n_ctx, lam_init)

        zpad = jnp.zeros((DECAY_LORA, D), F32)
        w2p = jnp.stack([jnp.concatenate([rw_w2[l, 0], zpad]), jnp.concatenate([zpad, rw_w2[l, 1]])]).astype(BF16)
        a2p = jnp.stack([jnp.concatenate([rw_a2[l, 0], zpad]), jnp.concatenate([zpad, rw_a2[l, 1]])]).astype(BF16)
        kkn, lw, asig = rw_prep(P, rw_w0[l], w2p, rw_a0[l], a2p, row(rw_kk[l]), bd)
        y = rw_scan(P, kkn, lw, asig, row(rw_ka[l]), n_ctx)

        g2p = jnp.concatenate([rw_g2[l], jnp.zeros((2 * LANES - GATE_LORA, D), F32)]).astype(BF16)
        xs = merge(xs, mod12[l], y, asig, P, a_br, b_ctx, b_lat, row(rw_ln_w[l]), row(rw_ln_b[l]),
                   row(rw_rk[l]), row(rw_ka[l]), g2p, bd,
                   w_br_a[l].astype(BF16), w_br_b[l].astype(BF16), w_br_c[l].astype(BF16),
                   w_o[l].astype(BF16), n_ctx)
        xs = ffn(xs, mod12[l], row(norm2_g[l]), ffn_wi[l].astype(BF16), ffn_wo[l].astype(BF16), n_ctx)
    return xs[:, n_ctx:]
```

```python
import functools
import math

import jax
import jax.numpy as jnp
from jax import lax
from jax.experimental import pallas as pl
from jax.experimental.pallas import tpu as pltpu

D_MODEL = 1024
GRID_W = 64
CHUNK = 128
GM_GROUPS = 8
DA_HEAD_DIM = 64
ROPE_BASE = 10000.0
RW_HEAD = 64
DECAY_LORA = 64
GATE_LORA = 160
RMS_EPS = 1e-6
GN_EPS = 64e-5

LANES = 128
SUBLANES = 8
GROUP_W = 256
VMEM_LIMIT = 56 * 1024 * 1024

COL_DA = 2048
COL_GATE = 5120
COL_RW = 8192
COL_TAIL = 11264

ADA_TN = 1536
PROJ_TN = 512
PROJ_MB = 1024
PREP_TT = 768
MERGE_TT = 256
FFN_TM = 768
FFN_FC = 256
ATTN_TQ = 512
ATTN_TQ_CTX = 256
ATTN_KB = 256
ATTN_AHEAD = 3
RW_CHUNK = 64
RW_SUB = 4

BF16 = jnp.bfloat16
F32 = jnp.float32


def _dot(a, b):
    return jnp.dot(a, b, preferred_element_type=F32)


def _dot_nt(a, b):
    return lax.dot_general(a, b, (((1,), (1,)), ((), ())), preferred_element_type=F32)


def _split_dot(x, m):
    hi = x.astype(BF16)
    lo = (x - hi.astype(F32)).astype(BF16)
    return _dot(hi, m) + _dot(lo, m)


def _split3_dot_left(m, x):
    hi = x.astype(BF16)
    r1 = x - hi.astype(F32)
    mid = r1.astype(BF16)
    lo = (r1 - mid.astype(F32)).astype(BF16)
    return _dot(m, hi) + _dot(m, mid) + _dot(m, lo)


def _group_sum(x, bd):
    width = bd.shape[0]
    parts = [_split_dot(x[:, i:i + width], bd) for i in range(0, x.shape[1], width)]
    return parts[0] if len(parts) == 1 else jnp.concatenate(parts, axis=1)


def _cparams(sem):
    return pltpu.CompilerParams(dimension_semantics=sem, vmem_limit_bytes=VMEM_LIMIT)


def _ada_kernel(c_ref, w_ref, b_ref, o_ref):
    c = c_ref[...]
    s = c * jax.nn.sigmoid(c)
    o_ref[...] = jnp.dot(s, w_ref[...], precision=lax.Precision.HIGHEST,
                         preferred_element_type=F32) + b_ref[...]


def ada_mod(cc, ada_w, ada_b):
    L, D, N = ada_w.shape
    tn = ADA_TN
    return pl.pallas_call(
        _ada_kernel,
        grid=(L, N // tn),
        in_specs=[pl.BlockSpec((16, D), lambda l, j: (0, 0)),
                  pl.BlockSpec((None, D, tn), lambda l, j: (l, 0, j)),
                  pl.BlockSpec((None, 1, tn), lambda l, j: (l, 0, j))],
        out_specs=pl.BlockSpec((None, 16, tn), lambda l, j: (l, 0, j)),
        out_shape=jax.ShapeDtypeStruct((L, 16, N), F32),
        compiler_params=_cparams(("arbitrary", "arbitrary")),
        name="ada_mod",
    )(cc, ada_w, ada_b.reshape(L, 1, N))


def _modulated_norm(x, g, m_ref, piece, row0, n_ctx):
    rows = row0 + lax.broadcasted_iota(jnp.int32, (x.shape[0], 1), 0)
    is_lat = rows >= n_ctx
    sh = jnp.where(is_lat, m_ref[2 * piece + 1:2 * piece + 2, :], m_ref[2 * piece:2 * piece + 1, :])
    sc = jnp.where(is_lat, m_ref[2 * piece + 3:2 * piece + 4, :], m_ref[2 * piece + 2:2 * piece + 3, :])
    y = x * lax.rsqrt(jnp.mean(x * x, axis=-1, keepdims=True) + RMS_EPS) * g
    return y * (1.0 + sc) + sh


def _gate_rows(m_ref, piece, row0, nrows, n_ctx):
    rows = row0 + lax.broadcasted_iota(jnp.int32, (nrows, 1), 0)
    return jnp.where(rows >= n_ctx, m_ref[2 * piece + 1:2 * piece + 2, :], m_ref[2 * piece:2 * piece + 1, :])


def _proj_kernel(x_ref, m_ref, g_ref, w_ref, mu_ref, o_ref, h_ref, p_ref, *, n_ctx, rb, mb, rw_j0):
    S = x_ref.shape[0]
    j = pl.program_id(1)
    nblk = S // rb

    @pl.when(j == 0)
    def _():
        def body(i, carry):
            r0 = pl.multiple_of(i * rb, rb)
            x = x_ref[pl.ds(r0, rb), :]
            h_ref[pl.ds(r0, rb), :] = _modulated_norm(x, g_ref[...], m_ref, 0, r0, n_ctx).astype(BF16)
            return carry
        lax.fori_loop(0, nblk, body, 0)

    def project(store):
        store(0, n_ctx, False, _dot(h_ref[0:n_ctx, :], w_ref[...]))

        def body(i, carry):
            r0 = pl.multiple_of(n_ctx + i * mb, math.gcd(n_ctx, mb))
            store(r0, mb, True, _dot(h_ref[pl.ds(r0, mb), :], w_ref[...]))
            return carry
        lax.fori_loop(0, (S - n_ctx) // mb, body, 0)

    @pl.when(j < rw_j0)
    def _():
        def store(r0, rows, is_latent, val):
            o_ref[pl.ds(r0, rows), :] = val.astype(o_ref.dtype)
        project(store)

    @pl.when(j >= rw_j0)
    def _():
        pad = SUBLANES
        zero = jnp.zeros((pad, p_ref.shape[1]), F32)
        p_ref[0:pad, :] = zero
        p_ref[n_ctx + pad:n_ctx + 2 * pad, :] = zero
        p_ref[S + 2 * pad:S + 3 * pad, :] = zero

        def store(r0, rows, is_latent, val):
            p_ref[pl.ds(r0 + (2 * pad if is_latent else pad), rows), :] = val
        project(store)
        c_cur = 1.0 - mu_ref[...]
        c_nb = 0.5 * mu_ref[...]
        for i in range(nblk):
            r0 = i * rb
            src = r0 + (2 * pad if r0 >= n_ctx else pad)
            nb = p_ref[src - 1:src - 1 + rb, :] + p_ref[src + 1:src + 1 + rb, :]
            o_ref[r0:r0 + rb, :] = (p_ref[src:src + rb, :] * c_cur + nb * c_nb).astype(o_ref.dtype)


def proj_in(xs, mod, g, w, mu, n_ctx):
    B, S, D = xs.shape
    NP = w.shape[1]
    tn = PROJ_TN
    kern = functools.partial(_proj_kernel, n_ctx=n_ctx, rb=n_ctx, mb=min(PROJ_MB, S - n_ctx), rw_j0=COL_RW // tn)
    return pl.pallas_call(
        kern,
        grid=(B, NP // tn),
        in_specs=[pl.BlockSpec((None, S, D), lambda b, j: (b, 0, 0)),
                  pl.BlockSpec((None, 12, D), lambda b, j: (b, 0, 0)),
                  pl.BlockSpec((1, D), lambda b, j: (0, 0)),
                  pl.BlockSpec((D, tn), lambda b, j: (0, j)),
                  pl.BlockSpec((1, tn), lambda b, j: (0, j))],
        out_specs=pl.BlockSpec((None, S, tn), lambda b, j: (b, 0, j)),
        out_shape=jax.ShapeDtypeStruct((B, S, NP), BF16),
        scratch_shapes=[pltpu.VMEM((S, D), BF16), pltpu.VMEM((S + 3 * SUBLANES, tn), F32)],
        compiler_params=_cparams(("arbitrary", "arbitrary")),
        name="proj_in",
    )(xs, mod, g, w, mu)


def _gelu(x):
    return 0.5 * x * (1.0 + lax.erf(x * (1.0 / math.sqrt(2.0))))


def _gmlp_kernel(p_ref, vg_ref, ws_ref, bs_ref, o_ref):
    W = vg_ref.shape[1]
    gd = W // GM_GROUPS
    for c in range(p_ref.shape[0] // CHUNK):
        rows = slice(c * CHUNK, (c + 1) * CHUNK)
        u = _gelu(p_ref[rows, :W].astype(F32))
        v = _gelu(p_ref[rows, W:].astype(F32))
        v = v * lax.rsqrt(jnp.mean(v * v, axis=-1, keepdims=True) + RMS_EPS) * vg_ref[...]
        vb = v.astype(BF16)
        for g in range(GM_GROUPS):
            sl = slice(g * gd, (g + 1) * gd)
            f = _dot(ws_ref[g], vb[:, sl]) + bs_ref[:, sl]
            o_ref[rows, sl] = (u[:, sl] * f).astype(o_ref.dtype)


def gmlp(P, vg, ws, bs_full, tt=PREP_TT):
    B, S, _ = P.shape
    W = vg.shape[1]
    return pl.pallas_call(
        _gmlp_kernel,
        grid=(B, S // tt),
        in_specs=[pl.BlockSpec((None, tt, 2 * W), lambda b, i: (b, i, 0)),
                  pl.BlockSpec((1, W), lambda b, i: (0, 0)),
                  pl.BlockSpec((GM_GROUPS, CHUNK, CHUNK), lambda b, i: (0, 0, 0)),
                  pl.BlockSpec((CHUNK, W), lambda b, i: (0, 0))],
        out_specs=pl.BlockSpec((None, tt, W), lambda b, i: (b, i, 0)),
        out_shape=jax.ShapeDtypeStruct((B, S, W), BF16),
        compiler_params=_cparams(("arbitrary", "arbitrary")),
        name="gmlp",
    )(P, vg, ws, bs_full)


def _qk_kernel(q_ref, k_ref, v_ref, cos_ref, sa_ref, sb_ref, qg_ref, kg_ref, bd_ref,
               qo_ref, ko_ref, vo_ref):
    W = q_ref.shape[1]
    reps = W // LANES
    bd = bd_ref[...]
    sw = bd.shape[0]
    cos = jnp.tile(cos_ref[...], (1, sw // LANES))
    sin_a = jnp.tile(sa_ref[...], (1, sw // LANES))
    sin_b = jnp.tile(sb_ref[...], (1, sw // LANES))
    half = DA_HEAD_DIM // 2

    def prep(src, g_ref, dst, scale):
        for c0 in range(0, W, sw):
            t = src[:, c0:c0 + sw].astype(F32)
            ms = _group_sum(t * t, bd) * (1.0 / DA_HEAD_DIM)
            t = t * lax.rsqrt(ms + RMS_EPS) * g_ref[:, c0:c0 + sw]
            t = t * cos + pltpu.roll(t, sw - half, 1) * sin_a + pltpu.roll(t, half, 1) * sin_b
            dst[:, c0:c0 + sw] = (t * scale if scale != 1.0 else t).astype(dst.dtype)

    prep(q_ref, qg_ref, qo_ref, DA_HEAD_DIM ** -0.5 * math.log2(math.e))
    prep(k_ref, kg_ref, ko_ref, 1.0)
    for h in range(reps):
        vo_ref[h] = v_ref[:, h * LANES:(h + 1) * LANES].astype(F32).T.astype(BF16)


def qk_prep(P, cos, sin_a, sin_b, qg, kg, bd, tt=PREP_TT):
    B, S, _ = P.shape
    W = D_MODEL
    c0 = COL_DA // W
    tok = lambda c: pl.BlockSpec((None, tt, W), lambda b, i, c=c: (b, i, c))
    tab = pl.BlockSpec((tt, LANES), lambda b, i: (i, 0))
    row = pl.BlockSpec((1, W), lambda b, i: (0, 0))
    out = pl.BlockSpec((None, tt, W), lambda b, i: (b, i, 0))
    shp = jax.ShapeDtypeStruct((B, S, W), BF16)
    return pl.pallas_call(
        _qk_kernel,
        grid=(B, S // tt),
        in_specs=[tok(c0), tok(c0 + 1), tok(c0 + 2), tab, tab, tab, row, row,
                  pl.BlockSpec((GROUP_W, GROUP_W), lambda b, i: (0, 0))],
        out_specs=[out, out, pl.BlockSpec((None, W // LANES, LANES, tt), lambda b, i: (b, 0, 0, i))],
        out_shape=[shp, shp, jax.ShapeDtypeStruct((B, W // LANES, LANES, S), BF16)],
        compiler_params=_cparams(("arbitrary", "arbitrary")),
        name="qk_prep",
    )(P, P, P, cos, sin_a, sin_b, qg, kg, bd)


def _stack_components(q):
    lane = lax.broadcasted_iota(jnp.int32, q.shape, 1)
    zero = jnp.zeros_like(q)
    return jnp.concatenate([jnp.where(lane < DA_HEAD_DIM, q, zero),
                            jnp.where(lane >= DA_HEAD_DIM, q, zero)], axis=0)


def _attn_finish(acc, l, lam_ref, sg_ref, o_ref, lam_init):
    tq = o_ref.shape[0]
    lp = lam_ref[...]
    lam = (jnp.exp(jnp.sum(lp[0:1] * lp[1:2], axis=-1, keepdims=True))
           - jnp.exp(jnp.sum(lp[2:3] * lp[3:4], axis=-1, keepdims=True)) + lam_init)
    inv = 1.0 / l
    o = acc[:, :tq] * inv[:, :tq] - acc[:, tq:] * (lam * inv[:, tq:])
    o = o * lax.rsqrt(jnp.mean(o * o, axis=0, keepdims=True) + RMS_EPS)
    o_ref[...] = (o.T * sg_ref[...] * (1.0 - lam_init)).astype(o_ref.dtype)


def _attn_ctx_kernel(q_ref, k_ref, v_ref, lam_ref, sg_ref, o_ref, *, lam_init):
    qs = _stack_components(q_ref[...])
    m = l = acc = None
    nblk = k_ref.shape[0] // ATTN_KB
    blk = lambda j: slice(j * ATTN_KB, (j + 1) * ATTN_KB)
    scores = lambda j: _dot_nt(k_ref[blk(j), :], qs)
    queue = [scores(j) for j in range(min(ATTN_AHEAD, nblk))]
    for j in range(nblk):
        if j + ATTN_AHEAD < nblk:
            queue.append(scores(j + ATTN_AHEAD))
        s = queue.pop(0)
        m_blk = jnp.max(s, axis=0, keepdims=True)
        m_new = m_blk if m is None else jnp.maximum(m, m_blk)
        e = jnp.exp2(s - m_new)
        pv = _dot(v_ref[:, blk(j)], e.astype(BF16))
        if m is None:
            l, acc = jnp.sum(e, axis=0, keepdims=True), pv
        else:
            alpha = jnp.exp2(m - m_new)
            l = alpha * l + jnp.sum(e, axis=0, keepdims=True)
            acc = alpha * acc + pv
        m = m_new
    _attn_finish(acc, l, lam_ref, sg_ref, o_ref, lam_init)


def attention_ctx(qn, kn, vt, lam_p, sg, n_ctx, lam_init, tq=ATTN_TQ_CTX):
    B, S, W = qn.shape
    H = W // LANES
    kern = functools.partial(_attn_ctx_kernel, lam_init=lam_init)
    return pl.pallas_call(
        kern,
        grid=(B, H, n_ctx // tq),
        in_specs=[pl.BlockSpec((None, tq, LANES), lambda b, h, i: (b, i, h)),
                  pl.BlockSpec((None, n_ctx, LANES), lambda b, h, i: (b, 0, h)),
                  pl.BlockSpec((None, None, LANES, n_ctx), lambda b, h, i: (b, h, 0, 0)),
                  pl.BlockSpec((4, DA_HEAD_DIM), lambda b, h, i: (0, 0)),
                  pl.BlockSpec((1, LANES), lambda b, h, i: (0, 0))],
        out_specs=pl.BlockSpec((None, tq, LANES), lambda b, h, i: (b, i, h)),
        out_shape=jax.ShapeDtypeStruct((B, n_ctx, W), BF16),
        compiler_params=_cparams(("arbitrary", "arbitrary", "arbitrary")),
        name="attention_ctx",
    )(qn, kn, vt, lam_p, sg)


def _attn_lat_kernel(*refs, lam_init, n_tiles, n_qsub):
    q_refs = refs[:n_qsub]
    k_ref, v_ref, lam_ref, sg_ref, o_ref, s_ref, m_ref = refs[n_qsub:]
    t = pl.program_id(0)
    nblk = k_ref.shape[0] // ATTN_KB
    blk = lambda j: slice(j * ATTN_KB, (j + 1) * ATTN_KB)

    def run(scores_slot, softmax_slot):
        if scores_slot is not None:
            qs = _stack_components(jnp.concatenate([r[...] for r in q_refs], axis=0))
        m_run = l = acc = None
        if softmax_slot is not None:
            m_prev = m_ref[softmax_slot]
        for j in range(nblk):
            if scores_slot is not None:
                s = _dot_nt(k_ref[blk(j), :], qs)
                s_ref[scores_slot, blk(j), :] = s
                for r in range(0, ATTN_KB, 8):
                    m_run = s[r:r + 8] if m_run is None else jnp.maximum(m_run, s[r:r + 8])
            if softmax_slot is not None:
                e = jnp.exp2(s_ref[softmax_slot, blk(j), :] - m_prev)
                pv = _dot(v_ref[:, blk(j)], e.astype(BF16))
                lsum = jnp.sum(e, axis=0, keepdims=True)
                l, acc = (lsum, pv) if l is None else (l + lsum, acc + pv)
        if scores_slot is not None:
            m_ref[scores_slot] = jnp.max(m_run, axis=0, keepdims=True)
        if softmax_slot is not None:
            _attn_finish(acc, l, lam_ref, sg_ref, o_ref, lam_init)

    @pl.when(t == 0)
    def _():
        run(0, None)

    for par in range(2):
        @pl.when((t > 0) & (t < n_tiles) & (t % 2 == par))
        def _():
            run(par, 1 - par)

    @pl.when(t == n_tiles)
    def _():
        run(None, (n_tiles - 1) % 2)


def attention_lat(qn, kn, vt, lam_p, sg, n_ctx, lam_init, tq=ATTN_TQ):
    B, S, W = qn.shape
    H = W // LANES
    nq = (S - n_ctx) // tq
    n_tiles = B * H * nq
    qb = math.gcd(tq, n_ctx)
    n_qsub = tq // qb
    kern = functools.partial(_attn_lat_kernel, lam_init=lam_init, n_tiles=n_tiles, n_qsub=n_qsub)

    def tile(n):
        return n // (H * nq), n % nq, (n // nq) % H

    def scored(t):
        return tile(jnp.minimum(t, n_tiles - 1))

    def finished(t):
        return tile(jnp.maximum(t - 1, 0))

    return pl.pallas_call(
        kern,
        grid=(n_tiles + 1,),
        in_specs=[*[pl.BlockSpec((None, qb, LANES), lambda t, s=s: (
                        scored(t)[0], n_ctx // qb + scored(t)[1] * n_qsub + s, scored(t)[2]))
                    for s in range(n_qsub)],
                  pl.BlockSpec((None, S, LANES), lambda t: (scored(t)[0], 0, scored(t)[2])),
                  pl.BlockSpec((None, None, LANES, S), lambda t: (finished(t)[0], finished(t)[2], 0, 0)),
                  pl.BlockSpec((4, DA_HEAD_DIM), lambda t: (0, 0)),
                  pl.BlockSpec((1, LANES), lambda t: (0, 0))],
        out_specs=pl.BlockSpec((None, tq, LANES), lambda t: finished(t)),
        out_shape=jax.ShapeDtypeStruct((B, S - n_ctx, W), BF16),
        scratch_shapes=[pltpu.VMEM((2, S, 2 * tq), F32), pltpu.VMEM((2, 1, 2 * tq), F32)],
        compiler_params=_cparams(("arbitrary",)),
        name="attention_lat",
    )(*([qn] * n_qsub), kn, vt, lam_p, sg)


def _rw_prep_kernel(zk_ref, zt_ref, w0_ref, w2_ref, a0_ref, a2_ref, kk_ref, bd_ref,
                    kkn_ref, lw_ref, as_ref):
    tw = jnp.tanh(zt_ref[:, 0:LANES].astype(F32)).astype(BF16)
    za = zt_ref[:, LANES:2 * LANES]
    for d in range(2):
        x = w0_ref[d:d + 1, :] + _dot(tw, w2_ref[d])
        lw_ref[d] = -math.exp(-0.5) * jax.nn.sigmoid(x)
        as_ref[d] = jax.nn.sigmoid(a0_ref[d:d + 1, :] + _dot(za, a2_ref[d])).astype(as_ref.dtype)
    kk = zk_ref[...].astype(F32) * kk_ref[...]
    nrm = jnp.sqrt(_group_sum(kk * kk, bd_ref[...]))
    kkn_ref[...] = (kk / jnp.maximum(nrm, 1e-12)).astype(kkn_ref.dtype)


def rw_prep(P, w0, w2p, a0, a2p, kkp, bd, tt=PREP_TT):
    B, S, _ = P.shape
    W = D_MODEL
    tokw = pl.BlockSpec((None, tt, W), lambda b, i: (b, i, COL_RW // W + 1))
    tail = pl.BlockSpec((None, tt, 512), lambda b, i: (b, i, COL_TAIL // 512))
    full = lambda shape: pl.BlockSpec(shape, lambda b, i: (0,) * len(shape))
    out2 = pl.BlockSpec((2, None, tt, W), lambda b, i: (0, b, i, 0))
    return pl.pallas_call(
        _rw_prep_kernel,
        grid=(B, S // tt),
        in_specs=[tokw, tail, full((2, W)), full((2, LANES, W)), full((2, W)), full((2, LANES, W)),
                  full((1, W)), full((GROUP_W, GROUP_W))],
        out_specs=[pl.BlockSpec((None, tt, W), lambda b, i: (b, i, 0)), out2, out2],
        out_shape=[jax.ShapeDtypeStruct((B, S, W), BF16),
                   jax.ShapeDtypeStruct((2, B, S, W), F32),
                   jax.ShapeDtypeStruct((2, B, S, W), BF16)],
        compiler_params=_cparams(("arbitrary", "arbitrary")),
        name="rw_prep",
    )(P, P, w0, w2p, a0, a2p, kkp, bd)


def _rw_scan_kernel(r_ref, zk_ref, v_ref, kkn_ref, lw_ref, as_ref, ka_ref, y_ref, h_ref):
    @pl.when(pl.program_id(1) == 0)
    def _():
        h_ref[...] = jnp.zeros_like(h_ref)

    C = RW_CHUNK
    n_sub = lw_ref[0].shape[0] // C
    for sub in range(n_sub):
        rows = (pl.ds(sub * C, C), pl.ds((n_sub - 1 - sub) * C, C))
        view = lambda pair: tuple(pair[d].at[rows[d]] for d in range(2))
        _scan_chunk(view(r_ref), view(zk_ref), view(v_ref), view(kkn_ref), view(lw_ref), view(as_ref),
                    ka_ref, view(y_ref), h_ref)


def _scan_chunk(r_ref, zk_ref, v_ref, kkn_ref, lw_ref, as_ref, ka_ref, y_ref, h_ref):
    C = RW_CHUNK
    R2 = 2 * C
    n_pairs = ka_ref.shape[1] // LANES

    ri = lax.broadcasted_iota(jnp.int32, (C, C), 0)
    ci = lax.broadcasted_iota(jnp.int32, (C, C), 1)
    rt = lax.broadcasted_iota(jnp.int32, (C, LANES), 0)
    lane = lax.broadcasted_iota(jnp.int32, (C, LANES), 1)
    cs = lane & (C - 1)
    levels = int(math.log2(C))
    same_blk = [(rt >> k) == (cs >> k) for k in range(levels + 1)]
    joins = [same_blk[k + 1] & ~same_blk[k] for k in range(levels)]
    eye = jnp.where(rt == cs, 1.0, 0.0)
    first = lane < RW_HEAD
    hr = lax.broadcasted_iota(jnp.int32, (LANES, LANES), 0)
    hc = lax.broadcasted_iota(jnp.int32, (LANES, LANES), 1)
    same_head = (hr >= RW_HEAD) == (hc >= RW_HEAD)
    ka = ka_ref[...]

    def stack(x):
        return jnp.concatenate([jnp.where(first, x, 0.0), jnp.where(first, 0.0, x)], axis=0)

    chains = []
    for d in range(2):
        before_c = (ci <= ri) if d == 0 else (ci >= ri)
        strict = (cs < rt) if d == 0 else (cs > rt)
        incl = (cs <= rt) if d == 0 else (cs >= rt)
        lw = lw_ref[d][...]
        L = _split3_dot_left(jnp.where(before_c, 1.0, 0.0).astype(BF16), lw)
        ltot = jnp.sum(lw, axis=0, keepdims=True)
        lmid = 0.5 * ltot
        e_r = jnp.exp(L - lmid)
        e_a = jnp.exp(L - lw - lmid)
        e_b = jnp.exp(lmid - L)
        e_mid = jnp.exp(lmid)
        asig = as_ref[d][...].astype(F32)
        kkn = kkn_ref[d][...].astype(F32)
        kd = zk_ref[d][...].astype(F32) * (1.0 + (asig - 1.0) * ka)
        a_in = -kkn * e_a
        r_in = r_ref[d][...].astype(F32) * e_r
        b_in = kkn * asig * e_b
        k_in = kd * e_b
        a_h = a_in * e_mid
        r_h = r_in * e_mid
        b_o = b_in * e_mid
        k_o = k_in * e_mid
        wtot = jnp.exp(ltot)
        vv = v_ref[d][...].astype(F32)
        for p in range(n_pairs):
            sl = slice(p * LANES, (p + 1) * LANES)
            chains.append(dict(
                d=d, p=p, sl=sl, strict=strict, incl=incl, wtot=wtot[:, sl],
                AR=jnp.concatenate([a_in[:, sl], r_in[:, sl]], axis=0).astype(BF16),
                BK=jnp.concatenate([stack(b_in[:, sl]), stack(k_in[:, sl])], axis=0).astype(BF16),
                ARh=jnp.concatenate([a_h[:, sl], r_h[:, sl]], axis=0).astype(BF16),
                BKo=jnp.concatenate([b_o[:, sl], k_o[:, sl]], axis=0).astype(BF16),
                V=vv[:, sl]))

    for ch in chains:
        G1 = _dot_nt(ch["AR"], ch["BK"])
        ch["Aab"] = jnp.where(ch["strict"], G1[:C, :R2], 0.0)
        ch["Aak"] = jnp.where(ch["strict"], G1[:C, R2:], 0.0).astype(BF16)
        ch["Ar"] = jnp.concatenate([jnp.where(ch["incl"], G1[C:, :R2], 0.0),
                                    jnp.where(ch["incl"], G1[C:, R2:], 0.0)], axis=1).astype(BF16)
        ch["H"] = h_ref[ch["d"], ch["p"]]
        ch["T"] = eye + jnp.where(joins[0], ch["Aab"], 0.0)
        ch["As"] = stack(ch["Aab"]).astype(BF16)
    for ch in chains:
        ch["G2"] = _dot_nt(ch["ARh"], ch["H"].astype(BF16))
        ch["Vs"] = stack(ch["V"]).astype(BF16)
        ch["rhs"] = ch["G2"][:C] + _dot(ch["Aak"], ch["Vs"])
    for join in joins[1:]:
        for ch in chains:
            ch["TA"] = _dot(ch["T"].astype(BF16), ch["As"]).astype(BF16)
        for ch in chains:
            ch["T"] = ch["T"] + jnp.where(join, _dot(ch["TA"], stack(ch["T"]).astype(BF16)), 0.0)
    for ch in chains:
        ch["U"] = _dot(ch["T"].astype(BF16), stack(ch["rhs"]).astype(BF16))
    for ch in chains:
        UVs = jnp.concatenate([stack(ch["U"]).astype(BF16), ch["Vs"]], axis=0)
        y_ref[ch["d"]][:, ch["sl"]] = ch["G2"][C:] + _dot(ch["Ar"], UVs)
    for ch in chains:
        UVt = jnp.concatenate([ch["U"], ch["V"]], axis=0).T.astype(BF16)
        upd = jnp.where(same_head, _dot(UVt, ch["BKo"]), 0.0)
        h_ref[ch["d"], ch["p"]] = ch["H"] * ch["wtot"] + upd


def rw_scan(P, kkn, lw, asig, ka, n_ctx):
    B, S, _ = P.shape
    W = D_MODEL
    C = RW_CHUNK * RW_SUB
    nc = S // C
    ncc = n_ctx // C

    def rev(c):
        return jnp.where(c < ncc, ncc - 1 - c, nc - 1 + ncc - c)

    def both(make):
        return [make(lambda c: c), make(rev)]

    pcol = lambda col: both(lambda f: pl.BlockSpec((None, C, W), lambda b, c, f=f: (b, f(c), col)))
    tok = both(lambda f: pl.BlockSpec((None, C, W), lambda b, c, f=f: (b, f(c), 0)))
    dtok = [pl.BlockSpec((None, None, C, W), lambda b, c: (0, b, c, 0)),
            pl.BlockSpec((None, None, C, W), lambda b, c: (1, b, rev(c), 0))]
    c0 = COL_RW // W

    def kern(rf, rb_, kf, kb_, vf, vb_, nf, nb_, lwf, lwb, asf, asb, ka_ref, yf, yb, h_ref):
        _rw_scan_kernel((rf, rb_), (kf, kb_), (vf, vb_), (nf, nb_), (lwf, lwb), (asf, asb), ka_ref,
                        (yf, yb), h_ref)

    return pl.pallas_call(
        kern,
        grid=(B, nc),
        in_specs=[*pcol(c0), *pcol(c0 + 1), *pcol(c0 + 2), *tok, *dtok, *dtok,
                  pl.BlockSpec((1, W), lambda b, c: (0, 0))],
        out_specs=[pl.BlockSpec((None, C, W), lambda b, c: (b, c, 0)),
                   pl.BlockSpec((None, C, W), lambda b, c: (b, rev(c), 0))],
        out_shape=[jax.ShapeDtypeStruct((B, S, W), F32)] * 2,
        scratch_shapes=[pltpu.VMEM((2, W // LANES, LANES, LANES), F32)],
        compiler_params=_cparams(("arbitrary", "arbitrary")),
        name="rw_scan",
    )(P, P, P, P, P, P, kkn, kkn, lw, lw, asig, asig, ka)


def _merge_kernel(x_ref, m_ref, yf_ref, yb_ref, as_ref, r_ref, zk_ref, v_ref, zt_ref, ga_ref, gb_ref, gc_ref,
                  a_ref, bc_ref, bl_ref, lnw_ref, lnb_ref, rk_ref, ka_ref, g2_ref, bd_ref,
                  wa_ref, wb_ref, wc_ref, wo_ref, o_ref, *, n_ctx):
    tt = x_ref.shape[0]
    row0 = pl.program_id(1) * tt
    bd = bd_ref[...]
    inv_n = 1.0 / RW_HEAD
    y = yf_ref[...] + yb_ref[...]
    mean = _group_sum(y, bd) * inv_n
    yc = y - mean
    var = _group_sum(yc * yc, bd) * inv_n
    yn = yc * lax.rsqrt(var + GN_EPS) * lnw_ref[...] + lnb_ref[...]
    zk = zk_ref[...].astype(F32)
    ka = ka_ref[...]
    kd_sum = (zk * (1.0 + (as_ref[0].astype(F32) - 1.0) * ka)
              + zk * (1.0 + (as_ref[1].astype(F32) - 1.0) * ka))
    vv = v_ref[...].astype(F32)
    bonus = _group_sum(r_ref[...].astype(F32) * kd_sum * rk_ref[...], bd) * vv
    g = _dot(jax.nn.sigmoid(zt_ref[:, 2 * LANES:].astype(F32)).astype(BF16), g2_ref[...])
    c_br = ((yn + bonus) * g).astype(BF16)
    gate = lambda ref: jax.nn.sigmoid(ref[...].astype(F32))
    m = (gate(ga_ref) * _dot(a_ref[...], wa_ref[...])
         + gate(gb_ref) * _dot(jnp.where(row0 < n_ctx, bc_ref[...], bl_ref[...]), wb_ref[...])
         + gate(gc_ref) * _dot(c_br, wc_ref[...]))
    out = _dot(m.astype(BF16), wo_ref[...])
    o_ref[...] = x_ref[...] + _gate_rows(m_ref, 2, row0, tt, n_ctx) * out


def merge(xs, mod, y, asig, P, a_br, b_ctx, b_lat, lnw, lnb, rk, ka, g2p, bd, wa, wb, wc, wo, n_ctx, tt=MERGE_TT):
    B, S, W = xs.shape
    tok = pl.BlockSpec((None, tt, W), lambda b, i: (b, i, 0))
    pcol = lambda col: pl.BlockSpec((None, tt, W), lambda b, i, col=col: (b, i, col))
    dtok = pl.BlockSpec((2, None, tt, W), lambda b, i: (0, b, i, 0))
    row = pl.BlockSpec((1, W), lambda b, i: (0, 0))
    wsq = pl.BlockSpec((W, W), lambda b, i: (0, 0))
    cr, cg = COL_RW // W, COL_GATE // W
    nct = n_ctx // tt
    kern = functools.partial(_merge_kernel, n_ctx=n_ctx)
    return pl.pallas_call(
        kern,
        grid=(B, S // tt),
        in_specs=[tok, pl.BlockSpec((None, 12, W), lambda b, i: (b, 0, 0)), tok, tok, dtok,
                  pcol(cr), pcol(cr + 1), pcol(cr + 2),
                  pl.BlockSpec((None, tt, 512), lambda b, i: (b, i, COL_TAIL // 512)),
                  pcol(cg), pcol(cg + 1), pcol(cg + 2), tok,
                  pl.BlockSpec((None, tt, W), lambda b, i: (b, jnp.minimum(i, nct - 1), 0)),
                  pl.BlockSpec((None, tt, W), lambda b, i: (b, jnp.maximum(i - nct, 0), 0)),
                  row, row, row, row,
                  pl.BlockSpec((2 * LANES, W), lambda b, i: (0, 0)),
                  pl.BlockSpec((GROUP_W, GROUP_W), lambda b, i: (0, 0)),
                  wsq, wsq, wsq, wsq],
        out_specs=tok,
        out_shape=jax.ShapeDtypeStruct((B, S, W), F32),
        compiler_params=_cparams(("arbitrary", "arbitrary")),
        name="merge",
    )(xs, mod, y[0], y[1], asig, P, P, P, P, P, P, P, a_br, b_ctx, b_lat, lnw, lnb, rk, ka, g2p, bd, wa, wb, wc, wo)


def _ffn_kernel(x_ref, m_ref, g_ref, wi_ref, wo_ref, o_ref, *, n_ctx, fc):
    tm = x_ref.shape[0]
    row0 = pl.program_id(1) * tm
    FF = wo_ref.shape[0]
    x = x_ref[...]
    h = _modulated_norm(x, g_ref[...], m_ref, 3, row0, n_ctx).astype(BF16)

    def gate_up(k):
        return (_dot(h, wi_ref[:, k * fc:(k + 1) * fc]), _dot(h, wi_ref[:, FF + k * fc:FF + (k + 1) * fc]))

    nxt = gate_up(0)
    acc = None
    for k in range(FF // fc):
        gt, up = nxt
        if (k + 1) * fc < FF:
            nxt = gate_up(k + 1)
        act = (gt * jax.nn.sigmoid(gt) * up).astype(BF16)
        part = _dot(act, wo_ref[k * fc:(k + 1) * fc, :])
        acc = part if acc is None else acc + part
    o_ref[...] = x + _gate_rows(m_ref, 5, row0, tm, n_ctx) * acc


def ffn(xs, mod, g, wi, wo, n_ctx, tm=FFN_TM, fc=FFN_FC):
    B, S, W = xs.shape
    FF = wo.shape[0]
    kern = functools.partial(_ffn_kernel, n_ctx=n_ctx, fc=fc)
    tok = pl.BlockSpec((None, tm, W), lambda b, i: (b, i, 0))
    return pl.pallas_call(
        kern,
        grid=(B, S // tm),
        in_specs=[tok, pl.BlockSpec((None, 12, W), lambda b, i: (b, 0, 0)),
                  pl.BlockSpec((1, W), lambda b, i: (0, 0)),
                  pl.BlockSpec((W, 2 * FF), lambda b, i: (0, 0)),
                  pl.BlockSpec((FF, W), lambda b, i: (0, 0))],
        out_specs=tok,
        out_shape=jax.ShapeDtypeStruct((B, S, W), F32),
        compiler_params=_cparams(("arbitrary", "arbitrary")),
        name="ffn",
    )(xs, mod, g, wi, wo)


def _rope_tables(n_rows, n_ctx):
    row = jnp.repeat(jnp.arange(n_rows), GRID_W).astype(F32)
    col = jnp.tile(jnp.arange(GRID_W), n_rows).astype(F32)
    axis_dim = DA_HEAD_DIM // 2
    inv_freq = ROPE_BASE ** (-jnp.arange(0, axis_dim, 2, dtype=F32) / axis_dim)
    ang = jnp.concatenate([row[:, None] * inv_freq, col[:, None] * inv_freq], axis=-1)
    ang = jnp.concatenate([ang, ang], axis=-1)
    cos, sin = jnp.cos(ang), jnp.sin(ang)
    cos = jnp.concatenate([jnp.ones((n_ctx, DA_HEAD_DIM), F32), cos], axis=0)
    sin = jnp.concatenate([jnp.zeros((n_ctx, DA_HEAD_DIM), F32), sin], axis=0)
    first = jnp.arange(DA_HEAD_DIM) < DA_HEAD_DIM // 2
    sin_a = jnp.where(first, -sin, 0.0)
    sin_b = jnp.where(first, 0.0, sin)
    dup = lambda t: jnp.concatenate([t, t], axis=-1)
    return dup(cos), dup(sin_a), dup(sin_b)


def kernel(x, c, ctx, c_ctx, ada_w, ada_b, norm1_g, norm2_g, w_in, gm_v_g, gm_ws, gm_bs,
           da_q_g, da_k_g, da_lambda, da_subln_g, rw_mu, rw_w0, rw_w2, rw_a0, rw_a2, rw_g2,
           rw_kk, rw_ka, rw_rk, rw_ln_w, rw_ln_b, w_br_a, w_br_b, w_br_c, w_o, ffn_wi, ffn_wo):
    B, T, D = x.shape
    n_ctx = ctx.shape[1]
    L = ada_w.shape[0]
    cos, sin_a, sin_b = _rope_tables(T // GRID_W, n_ctx)
    xs = jnp.concatenate([ctx, x], axis=1)

    cc = jnp.zeros((16, D), F32).at[:B].set(c).at[B].set(c_ctx)
    mods = ada_mod(cc, ada_w, ada_b)
    mod_l = mods[:, :B].reshape(L, B, 6, 1, D)
    mod_c = jnp.broadcast_to(mods[:, B].reshape(L, 1, 6, 1, D), (L, B, 6, 1, D))
    mod12 = jnp.concatenate([mod_c, mod_l], axis=3).reshape(L, B, 12, D)

    gi = jnp.arange(GROUP_W) // RW_HEAD
    bd = (gi[:, None] == gi[None, :]).astype(BF16)
    row = lambda t: t.reshape(1, -1)
    d0, r0 = 2 * D, 5 * D
    t0 = r0 + 3 * D
    k0 = r0 + rw_mu.shape[1]
    tail_pad = COL_TAIL + 4 * LANES - COL_RW - rw_mu.shape[1]

    for l in range(L):
        lam_init = 0.8 - 0.6 * math.exp(-0.3 * l)
        wl = w_in[l]
        w_cat = jnp.concatenate([wl[:, a:b].astype(BF16) for a, b in
                                 ((0, d0), (d0, r0), (k0, wl.shape[1]), (r0, t0), (t0, k0))]
                                + [jnp.zeros((D, tail_pad), BF16)], axis=1)
        mu = jnp.concatenate([jnp.zeros((COL_RW,), F32), rw_mu[l], jnp.zeros((tail_pad,), F32)]).reshape(1, -1)
        P = proj_in(xs, mod12[l], row(norm1_g[l]), w_cat, mu, n_ctx)

        bs_full = jnp.repeat(gm_bs[l].T, CHUNK, axis=1)
        a_br = gmlp(P, row(gm_v_g[l]), gm_ws[l].astype(BF16), bs_full)

        tile = lambda t, n: jnp.tile(t, n).reshape(1, -1)
        qn, kn, vb = qk_prep(P, cos, sin_a, sin_b, tile(da_q_g[l], 16), tile(da_k_g[l], 16), bd)
        if l < L - 1:
            b_ctx = attention_ctx(qn, kn, vb, da_lambda[l], row(da_subln_g[l]), n_ctx, lam_init)
        else:
            b_ctx = jnp.zeros((B, n_ctx, D), BF16)
        b_lat = attention_lat(qn, kn, vb, da_lambda[l], row(da_subln_g[l]), n_ctx, lam_init)

        zpad = jnp.zeros((DECAY_LORA, D), F32)
        w2p = jnp.stack([jnp.concatenate([rw_w2[l, 0], zpad]), jnp.concatenate([zpad, rw_w2[l, 1]])]).astype(BF16)
        a2p = jnp.stack([jnp.concatenate([rw_a2[l, 0], zpad]), jnp.concatenate([zpad, rw_a2[l, 1]])]).astype(BF16)
        kkn, lw, asig = rw_prep(P, rw_w0[l], w2p, rw_a0[l], a2p, row(rw_kk[l]), bd)
        y = rw_scan(P, kkn, lw, asig, row(rw_ka[l]), n_ctx)

        g2p = jnp.concatenate([rw_g2[l], jnp.zeros((2 * LANES - GATE_LORA, D), F32)]).astype(BF16)
        xs = merge(xs, mod12[l], y, asig, P, a_br, b_ctx, b_lat, row(rw_ln_w[l]), row(rw_ln_b[l]),
                   row(rw_rk[l]), row(rw_ka[l]), g2p, bd,
                   w_br_a[l].astype(BF16), w_br_b[l].astype(BF16), w_br_c[l].astype(BF16),
                   w_o[l].astype(BF16), n_ctx)
        xs = ffn(xs, mod12[l], row(norm2_g[l]), ffn_wi[l].astype(BF16), ffn_wo[l].astype(BF16), n_ctx)
    return xs[:, n_ctx:]
```
